```python
import jax
import jax.numpy as jnp
from jax import lax
import numpy as np

D_MODEL = 1024
BATCH = 2
SEQ = 8192
DEPTH = 4

GRID_W = 64
CTX_LEN = 256
HEAD_DIM = 128
A_Q_HEADS = 4
A_KV_HEADS = 2
B_Q_HEADS = 4
B_KV_HEADS = 2
A_GROUP = A_Q_HEADS // A_KV_HEADS
B_GROUP = B_Q_HEADS // B_KV_HEADS
A_QW = A_Q_HEADS * HEAD_DIM
A_KVW = A_KV_HEADS * HEAD_DIM
B_QW = B_Q_HEADS * HEAD_DIM
B_KVW = B_KV_HEADS * HEAD_DIM
ATTN_SECTIONS = (A_QW, A_KVW, A_KVW, B_QW, B_KVW, B_KVW)
ATTN_IN = A_QW + 2 * A_KVW + B_QW + 2 * B_KVW
ATTN_MIX = A_QW + B_QW
ATTN_SCALE = HEAD_DIM ** -0.5
Q_BLOCK = 128
WINDOW = 128
ROPE_THETA = 10000.0
ROPE_AXIS_DIM = HEAD_DIM // 2
GLA_HEADS = 4
GLA_KEY_DIM = D_MODEL // 2
GLA_VAL_DIM = D_MODEL
GLA_DK = GLA_KEY_DIM // GLA_HEADS
GLA_DV = GLA_VAL_DIM // GLA_HEADS
GLA_IN = 2 * GLA_KEY_DIM + 2 * GLA_VAL_DIM
GLA_GATE_RANK = 16
GLA_GATE_NORM = 16.0
GLA_CHUNK = 64
D_FF = 2816
CONV_W = 3
N_ATTN_LAYERS = (DEPTH + 1) // 2
N_GLA_LAYERS = DEPTH // 2
DEEPNORM_ALPHA = (2 * DEPTH) ** 0.25
DEEPNORM_BETA = (8 * DEPTH) ** -0.25
EPS = 1e-6

kernel_name = 'hybrid_diffusion_trunk'


def layer_norm(x, g, b):
    xf = x.astype(jnp.float32)
    mu = jnp.mean(xf, axis=-1, keepdims=True)
    var = jnp.mean(jnp.square(xf - mu), axis=-1, keepdims=True)
    return ((xf - mu) * lax.rsqrt(var + EPS) * g + b).astype(x.dtype)


def rms_norm(x, g):
    xf = x.astype(jnp.float32)
    return (xf * lax.rsqrt(jnp.mean(jnp.square(xf), axis=-1, keepdims=True) + EPS) * g).astype(x.dtype)


def to_heads(z, n_heads, head_dim):
    b, t, _ = z.shape
    return z.reshape(b, t, n_heads, head_dim).transpose(0, 2, 1, 3)


def from_heads(z):
    b, h, t, d = z.shape
    return z.transpose(0, 2, 1, 3).reshape(b, t, h * d)


def merge_groups(o):
    b, hk, g, t, d = o.shape
    return from_heads(o.reshape(b, hk * g, t, d))


def axial_rope_tables(rows):
    row = jnp.repeat(jnp.arange(rows, dtype=jnp.float32), GRID_W)
    col = jnp.tile(jnp.arange(GRID_W, dtype=jnp.float32), rows)
    inv_freq = jnp.power(ROPE_THETA, -jnp.arange(ROPE_AXIS_DIM // 2, dtype=jnp.float32) * 2.0 / ROPE_AXIS_DIM)
    ang = jnp.concatenate([row[:, None] * inv_freq, col[:, None] * inv_freq], axis=-1)
    return jnp.cos(ang), jnp.sin(ang)


def apply_rope(z, cos, sin):
    zf = z.astype(jnp.float32)
    z1, z2 = zf[..., :HEAD_DIM // 2], zf[..., HEAD_DIM // 2:]
    return jnp.concatenate([z1 * cos - z2 * sin, z1 * sin + z2 * cos], axis=-1).astype(z.dtype)


def scores(q, k):
    return jnp.einsum('bhgqd,bhkd->bhgqk', q, k).astype(jnp.float32) * ATTN_SCALE


def ada_modulation(cond, w, b):
    return jnp.split(jax.nn.silu(cond) @ w + b, 6, axis=-1)


def attn_project(h, w_in, q_norm, k_norm, rope):
    p = h @ w_in
    parts, off = [], 0
    for width in ATTN_SECTIONS:
        parts.append(p[..., off:off + width])
        off += width
    aq, ak, av, bq, bk, bv = parts
    aq = rms_norm(to_heads(aq, A_Q_HEADS, HEAD_DIM), q_norm)
    ak = rms_norm(to_heads(ak, A_KV_HEADS, HEAD_DIM), k_norm)
    av = to_heads(av, A_KV_HEADS, HEAD_DIM)
    bq = to_heads(bq, B_Q_HEADS, HEAD_DIM)
    bk = to_heads(bk, B_KV_HEADS, HEAD_DIM)
    bv = to_heads(bv, B_KV_HEADS, HEAD_DIM)
    if rope is not None:
        cos, sin = rope
        aq, ak, bq, bk = [apply_rope(z, cos, sin) for z in (aq, ak, bq, bk)]
    b, _, t, _ = aq.shape
    aq = aq.reshape(b, A_KV_HEADS, A_GROUP, t, HEAD_DIM)
    bq = bq.reshape(b, B_KV_HEADS, B_GROUP, t, HEAD_DIM)
    return aq, ak, av, bq, bk, bv


def dense_attention(q, k, v):
    p = jax.nn.softmax(scores(q, k), axis=-1)
    return jnp.einsum('bhgqk,bhkd->bhgqd', p.astype(v.dtype), v)


def global_attention_latent(q, k_lat, v_lat, k_ctx, v_ctx):
    b, hk, g, n, d = q.shape
    nb = n // Q_BLOCK
    k_all = jnp.concatenate([k_lat, k_ctx], axis=2)
    v_all = jnp.concatenate([v_lat, v_ctx], axis=2)
    q_blocks = jnp.moveaxis(q.reshape(b, hk, g, nb, Q_BLOCK, d), 3, 0)
    o = lax.map(lambda qb: dense_attention(qb, k_all, v_all), q_blocks)
    return jnp.moveaxis(o, 0, 3).reshape(b, hk, g, n, d)


def window_attention_latent(q, k_lat, v_lat, k_ctx, v_ctx, sink):
    b, hk, g, n, d = q.shape
    nb = n // Q_BLOCK

    def bands(z):
        zp = jnp.pad(z, ((0, 0), (0, 0), (Q_BLOCK, Q_BLOCK), (0, 0))).reshape(b, hk, nb + 2, Q_BLOCK, d)
        return jnp.concatenate([zp[:, :, :-2], zp[:, :, 1:-1], zp[:, :, 2:]], axis=3)

    kb, vb = bands(k_lat), bands(v_lat)
    qb = q.reshape(b, hk, g, nb, Q_BLOCK, d)
    s_band = jnp.einsum('bhgnqd,bhnkd->bhgnqk', qb, kb).astype(jnp.float32) * ATTN_SCALE
    q_pos = jnp.arange(nb)[:, None, None] * Q_BLOCK + jnp.arange(Q_BLOCK)[None, :, None]
    k_pos = jnp.arange(nb)[:, None, None] * Q_BLOCK - Q_BLOCK + jnp.arange(3 * Q_BLOCK)[None, None, :]
    valid = (jnp.abs(k_pos - q_pos) <= WINDOW) & (k_pos >= 0) & (k_pos < n)
    s_band = jnp.where(valid, s_band, -jnp.inf)
    s_ctx = jnp.einsum('bhgnqd,bhcd->bhgnqc', qb, k_ctx).astype(jnp.float32) * ATTN_SCALE
    s_sink = jnp.broadcast_to(sink.astype(jnp.float32).reshape(1, hk, g, 1, 1, 1), s_ctx.shape[:-1] + (1,))
    p = jax.nn.softmax(jnp.concatenate([s_band, s_ctx, s_sink], axis=-1), axis=-1).astype(v_lat.dtype)
    n_band = 3 * Q_BLOCK
    n_ctx = k_ctx.shape[2]
    o = (jnp.einsum('bhgnqk,bhnkd->bhgnqd', p[..., :n_band], vb)
         + jnp.einsum('bhgnqc,bhcd->bhgnqd', p[..., n_band:n_band + n_ctx], v_ctx))
    return o.reshape(b, hk, g, n, d)


def sink_attention_context(q, k, v, sink):
    b, hk, g, t, d = q.shape
    s = scores(q, k)
    s_sink = jnp.broadcast_to(sink.astype(jnp.float32).reshape(1, hk, g, 1, 1), (b, hk, g, t, 1))
    p = jax.nn.softmax(jnp.concatenate([s, s_sink], axis=-1), axis=-1)[..., :-1]
    return jnp.einsum('bhgqk,bhkd->bhgqd', p.astype(v.dtype), v)


def attn_mixer(h_lat, h_ctx, w_in, q_norm, k_norm, sink, w_out, rope, need_ctx):
    aq, ak, av, bq, bk, bv = attn_project(h_lat, w_in, q_norm, k_norm, rope)
    caq, cak, cav, cbq, cbk, cbv = attn_project(h_ctx, w_in, q_norm, k_norm, None)
    o_a = global_attention_latent(aq, ak, av, cak, cav)
    o_b = window_attention_latent(bq, bk, bv, cbk, cbv, sink)
    y_lat = jnp.concatenate([merge_groups(o_a), merge_groups(o_b)], axis=-1) @ w_out
    y_ctx = None
    if need_ctx:
        co_a = dense_attention(caq, cak, cav)
        co_b = sink_attention_context(cbq, cbk, cbv, sink)
        y_ctx = jnp.concatenate([merge_groups(co_a), merge_groups(co_b)], axis=-1) @ w_out
    return y_lat, y_ctx


def gla_project(h, w_in, w_a1, w_a2, b_a):
    b, t, _ = h.shape
    p = h @ w_in
    q = p[..., :GLA_KEY_DIM]
    k = p[..., GLA_KEY_DIM:2 * GLA_KEY_DIM]
    v = p[..., 2 * GLA_KEY_DIM:2 * GLA_KEY_DIM + GLA_VAL_DIM]
    g = p[..., 2 * GLA_KEY_DIM + GLA_VAL_DIM:]
    q = to_heads(q, GLA_HEADS, GLA_DK) * (GLA_DK ** -0.5)
    k = to_heads(k, GLA_HEADS, GLA_DK)
    v = to_heads(v, GLA_HEADS, GLA_DV)
    low = jnp.einsum('btd,zdr->zbtr', h, w_a1)
    logits = jnp.einsum('zbtr,zrk->zbtk', low, w_a2) + b_a[:, None, None, :]
    log_decay = jax.nn.log_sigmoid(logits.astype(jnp.float32)) / GLA_GATE_NORM
    log_decay = log_decay.reshape(2, b, t, GLA_HEADS, GLA_DK).transpose(0, 1, 3, 2, 4)
    return q, k, v, g, log_decay


def gla_chunk_scan(q, k, v, log_decay, s0):
    b, h, t, dk = q.shape
    dv = v.shape[-1]
    nc = t // GLA_CHUNK

    def chunks(z):
        return jnp.moveaxis(z.astype(jnp.float32).reshape(b, h, nc, GLA_CHUNK, z.shape[-1]), 2, 0)

    lower = jnp.tril(jnp.ones((GLA_CHUNK, GLA_CHUNK), dtype=bool))[:, :, None]

    def step(state, inp):
        qc, kc, vc, lc = inp
        cum = jnp.cumsum(lc, axis=2)
        rel = jnp.where(lower, cum[:, :, :, None, :] - cum[:, :, None, :, :], -jnp.inf)
        attn = jnp.einsum('bhtd,bhsd,bhtsd->bhts', qc, kc, jnp.exp(rel))
        last = cum[:, :, -1:, :]
        out = attn @ vc + jnp.einsum('bhtd,bhdv->bhtv', qc * jnp.exp(cum), state)
        state = (jnp.exp(last[:, :, 0, :, None]) * state
                 + jnp.einsum('bhsd,bhsv->bhdv', kc * jnp.exp(last - cum), vc))
        return state, out

    s_fin, out = lax.scan(step, s0.astype(jnp.float32), (chunks(q), chunks(k), chunks(v), chunks(log_decay)))
    out = jnp.moveaxis(out, 0, 2).reshape(b, h, t, dv)
    return out.astype(v.dtype), s_fin


def gla_output(o, g, head_norm, w_out):
    o = from_heads(rms_norm(o, head_norm))
    return (o * jax.nn.silu(g)) @ w_out


def gla_mixer(h_lat, h_ctx, w_in, w_a1, w_a2, b_a, head_norm, w_out, need_ctx):
    flip = lambda z: jnp.flip(z, axis=2)
    q, k, v, g, ld = gla_project(h_lat, w_in, w_a1, w_a2, b_a)
    cq, ck, cv, cg, cld = gla_project(h_ctx, w_in, w_a1, w_a2, b_a)
    s0 = jnp.zeros((q.shape[0], GLA_HEADS, GLA_DK, GLA_DV), jnp.float32)
    co_f, cs_f = gla_chunk_scan(cq, ck, cv, cld[0], s0)
    co_b, cs_b = gla_chunk_scan(flip(cq), flip(ck), flip(cv), flip(cld[1]), s0)
    o_f, _ = gla_chunk_scan(q, k, v, ld[0], cs_f)
    o_b, _ = gla_chunk_scan(flip(q), flip(k), flip(v), flip(ld[1]), cs_b)
    y_lat = gla_output(o_f + flip(o_b), g, head_norm, w_out)
    y_ctx = None
    if need_ctx:
        y_ctx = gla_output(co_f + flip(co_b), cg, head_norm, w_out)
    return y_lat, y_ctx


def conv_ffn(h, w_up, conv_w, conv_b, w_down):
    t = h.shape[1]
    u = h @ w_up
    half = CONV_W // 2
    up = jnp.pad(u, ((0, 0), (half, half), (0, 0)))
    acc = conv_b + conv_w[0] * up[:, 0:t]
    for j in range(1, CONV_W):
        acc = acc + conv_w[j] * up[:, j:j + t]
    gate, val = jnp.split(acc, 2, axis=-1)
    return (jax.nn.silu(gate) * val) @ w_down


def setup_inputs(seed: int = 0) -> dict:
    key = jax.random.key(seed)
    ks = jax.random.split(key, 23)

    def nrm(k, shape, scale):
        return jax.random.normal(k, shape, jnp.float32) * scale

    d, f = D_MODEL, D_FF
    na, ng = N_ATTN_LAYERS, N_GLA_LAYERS
    return {
        'x': nrm(ks[0], (BATCH, SEQ, d), 1.0),
        'c': nrm(ks[1], (BATCH, d), 1.0),
        'ctx': nrm(ks[2], (BATCH, CTX_LEN, d), 1.0),
        'c_ctx': nrm(ks[3], (d,), 1.0),
        'ada_w': nrm(ks[4], (DEPTH, d, 6 * d), 0.5 * d ** -0.5),
        'ada_b': nrm(ks[5], (DEPTH, 6 * d), 0.01),
        'ln_g': 1.0 + nrm(ks[6], (DEPTH, 2, d), 0.02),
        'ln_b': nrm(ks[7], (DEPTH, 2, d), 0.02),
        'ffn_w_up': nrm(ks[8], (DEPTH, d, 2 * f), d ** -0.5),
        'ffn_conv_w': nrm(ks[9], (DEPTH, CONV_W, 2 * f), CONV_W ** -0.5),
        'ffn_conv_b': nrm(ks[10], (DEPTH, 2 * f), 0.01),
        'ffn_w_down': nrm(ks[11], (DEPTH, f, d), DEEPNORM_BETA * f ** -0.5),
        'attn_w_in': nrm(ks[12], (na, d, ATTN_IN), d ** -0.5),
        'attn_q_norm': 1.0 + nrm(ks[13], (na, HEAD_DIM), 0.02),
        'attn_k_norm': 1.0 + nrm(ks[14], (na, HEAD_DIM), 0.02),
        'attn_sink': nrm(ks[15], (na, B_Q_HEADS), 0.5),
        'attn_w_out': nrm(ks[16], (na, ATTN_MIX, d), DEEPNORM_BETA * ATTN_MIX ** -0.5),
        'gla_w_in': nrm(ks[17], (ng, d, GLA_IN), d ** -0.5),
        'gla_w_a1': nrm(ks[18], (ng, 2, d, GLA_GATE_RANK), d ** -0.5),
        'gla_w_a2': nrm(ks[19], (ng, 2, GLA_GATE_RANK, GLA_KEY_DIM), GLA_GATE_RANK ** -0.5),
        'gla_b_a': 1.0 + nrm(ks[20], (ng, 2, GLA_KEY_DIM), 0.5),
        'gla_head_norm': 1.0 + nrm(ks[21], (ng, GLA_DV), 0.02),
        'gla_w_out': nrm(ks[22], (ng, GLA_VAL_DIM, d), DEEPNORM_BETA * GLA_VAL_DIM ** -0.5),
    }


def reference(x, c, ctx, c_ctx, ada_w, ada_b, ln_g, ln_b, ffn_w_up, ffn_conv_w, ffn_conv_b, ffn_w_down,
              attn_w_in, attn_q_norm, attn_k_norm, attn_sink, attn_w_out,
              gla_w_in, gla_w_a1, gla_w_a2, gla_b_a, gla_head_norm, gla_w_out):
    rows = x.shape[1] // GRID_W
    rope = axial_rope_tables(rows)
    h, hc = x, ctx
    for i in range(DEPTH):
        need_ctx = i < DEPTH - 1
        sh1, sc1, g1, sh2, sc2, g2 = [m[:, None, :] for m in ada_modulation(c, ada_w[i], ada_b[i])]
        csh1, csc1, cg1, csh2, csc2, cg2 = ada_modulation(c_ctx, ada_w[i], ada_b[i])
        a_lat = h * (1 + sc1) + sh1
        a_ctx = hc * (1 + csc1) + csh1
        j = i // 2
        if i % 2 == 0:
            y, yc = attn_mixer(a_lat, a_ctx, attn_w_in[j], attn_q_norm[j], attn_k_norm[j], attn_sink[j],
                               attn_w_out[j], rope, need_ctx)
        else:
            y, yc = gla_mixer(a_lat, a_ctx, gla_w_in[j], gla_w_a1[j], gla_w_a2[j], gla_b_a[j],
                              gla_head_norm[j], gla_w_out[j], need_ctx)
        h = layer_norm(DEEPNORM_ALPHA * h + g1 * y, ln_g[i, 0], ln_b[i, 0])
        f = conv_ffn(h * (1 + sc2) + sh2, ffn_w_up[i], ffn_conv_w[i], ffn_conv_b[i], ffn_w_down[i])
        h = layer_norm(DEEPNORM_ALPHA * h + g2 * f, ln_g[i, 1], ln_b[i, 1])
        if need_ctx:
            hc = layer_norm(DEEPNORM_ALPHA * hc + cg1 * yc, ln_g[i, 0], ln_b[i, 0])
            fc = conv_ffn(hc * (1 + csc2) + csh2, ffn_w_up[i], ffn_conv_w[i], ffn_conv_b[i], ffn_w_down[i])
            hc = layer_norm(DEEPNORM_ALPHA * hc + cg2 * fc, ln_g[i, 1], ln_b[i, 1])
    return h
```

```python
import functools

import jax
import jax.numpy as jnp
from jax import lax
from jax.experimental import pallas as pl
from jax.experimental.pallas import tpu as pltpu

GRID_W = 64
HEAD_DIM = 128
A_Q_HEADS = 4
A_KV_HEADS = 2
B_Q_HEADS = 4
B_KV_HEADS = 2
A_QW = A_Q_HEADS * HEAD_DIM
A_KVW = A_KV_HEADS * HEAD_DIM
B_QW = B_Q_HEADS * HEAD_DIM
B_KVW = B_KV_HEADS * HEAD_DIM
ATTN_IN = A_QW + 2 * A_KVW + B_QW + 2 * B_KVW
ATTN_SCALE = HEAD_DIM ** -0.5
WINDOW = 128
ROPE_THETA = 10000.0
ROPE_AXIS_DIM = HEAD_DIM // 2
GLA_HEADS = 4
GLA_DK = 128
GLA_DV = 256
GLA_KEY_DIM = GLA_HEADS * GLA_DK
GLA_VAL_DIM = GLA_HEADS * GLA_DV
GLA_IN = 2 * GLA_KEY_DIM + 2 * GLA_VAL_DIM
GLA_GATE_RANK = 16
GLA_GATE_NORM = 16.0
CONV_W = 3
EPS = 1e-6

LANES = 128
MXU_N = 256
VMEM_BYTES = 64 * 2 ** 20

MXU_DTYPE = jnp.bfloat16
F32 = jnp.float32

ROW_TILE = 256
HALO = 16
FF_CHUNK = 256
KV_CHUNK = 256
GLA_CHUNK = 128
MOD_COLS = 1536


def _cparams(sem, vmem_mib):
    return pltpu.CompilerParams(dimension_semantics=sem, vmem_limit_bytes=vmem_mib * 2 ** 20)


def _dot(a, b):
    return jnp.dot(a, b, preferred_element_type=F32)


def _dot_nt(a, b):
    return lax.dot_general(a, b, (((1,), (1,)), ((), ())), preferred_element_type=F32)


def _dot_tn(a, b):
    return lax.dot_general(a, b, (((0,), (0,)), ((), ())), preferred_element_type=F32)


def _silu(x):
    return x / (1.0 + jnp.exp(-x))


def _residual_layer_norm(h, y, gate, ln_g, ln_b, alpha):
    r = alpha * h + gate * y
    mu = jnp.mean(r, axis=-1, keepdims=True)
    d = r - mu
    var = jnp.mean(d * d, axis=-1, keepdims=True)
    return d * lax.rsqrt(var + EPS) * ln_g + ln_b


def _mod_row(t_lat, tm):
    n_lat = t_lat // tm

    def row(b, i):
        return jnp.where(i < n_lat, b, MOD_CTX_ROW)
    return row


MOD_ROWS = 8
MOD_CTX_ROW = MOD_ROWS - 1


def _mod_kernel(cond_ref, w_ref, b_ref, o_ref):
    a = _silu(cond_ref[...]).astype(MXU_DTYPE)
    o_ref[...] = _dot(a, w_ref[...].astype(MXU_DTYPE)) + b_ref[...]


def _modulation(cond, ada_w, ada_b):
    depth, d, n = ada_w.shape
    return pl.pallas_call(
        _mod_kernel,
        grid=(depth, n // MOD_COLS),
        in_specs=[
            pl.BlockSpec((MOD_ROWS, d), lambda l, j: (0, 0)),
            pl.BlockSpec((None, d, MOD_COLS), lambda l, j: (l, 0, j)),
            pl.BlockSpec((None, 1, MOD_COLS), lambda l, j: (l, 0, j)),
        ],
        out_specs=pl.BlockSpec((None, MOD_ROWS, MOD_COLS), lambda l, j: (l, 0, j)),
        out_shape=jax.ShapeDtypeStruct((depth, MOD_ROWS, n), F32),
        compiler_params=_cparams(("parallel", "parallel"), 32),
        name="ada_modulation",
    )(cond, ada_w, ada_b.reshape(depth, 1, n))


def _attn_proj_kernel(h_ref, mod_ref, w_ref, qn_ref, kn_ref, cos_ref, sin_ref, o_ref):
    a = (h_ref[...] * (1.0 + mod_ref[1:2, :]) + mod_ref[0:1, :]).astype(MXU_DTYPE)
    cos = cos_ref[...]
    sin = sin_ref[...]

    def rms(z, g):
        return z * lax.rsqrt(jnp.mean(z * z, axis=-1, keepdims=True) + EPS) * g

    def rope(z):
        return z * cos + pltpu.roll(z, HEAD_DIM // 2, 1) * sin

    a_k0 = A_QW // HEAD_DIM
    a_v0 = a_k0 + A_KV_HEADS
    b_q0 = a_v0 + A_KV_HEADS
    b_k0 = b_q0 + B_Q_HEADS
    b_v0 = b_k0 + B_KV_HEADS
    heads_per_dot = MXU_N // HEAD_DIM
    for cb in range(ATTN_IN // MXU_N):
        z2 = _dot(a, w_ref[:, cb * MXU_N:(cb + 1) * MXU_N])
        for half in range(heads_per_dot):
            hb = cb * heads_per_dot + half
            z = z2[:, half * HEAD_DIM:(half + 1) * HEAD_DIM]
            if hb < a_k0:
                z = rope(rms(z, qn_ref[...])) * ATTN_SCALE
            elif hb < a_v0:
                z = rope(rms(z, kn_ref[...]))
            elif hb < b_q0:
                pass
            elif hb < b_k0:
                z = rope(z) * ATTN_SCALE
            elif hb < b_v0:
                z = rope(z)
            o_ref[:, hb * HEAD_DIM:(hb + 1) * HEAD_DIM] = z.astype(o_ref.dtype)


def _attn_project(h, mods, layer, w_in, q_norm, k_norm, cos, sin, t_lat):
    b, s, d = h.shape
    tm = ROW_TILE
    row = _mod_row(t_lat, tm)
    return pl.pallas_call(
        _attn_proj_kernel,
        grid=(b, s // tm),
        in_specs=[
            pl.BlockSpec((None, tm, d), lambda bi, i: (bi, i, 0)),
            pl.BlockSpec((None, None, 6, d), lambda bi, i: (layer, row(bi, i), 0, 0)),
            pl.BlockSpec((d, ATTN_IN), lambda bi, i: (0, 0)),
            pl.BlockSpec((1, HEAD_DIM), lambda bi, i: (0, 0)),
            pl.BlockSpec((1, HEAD_DIM), lambda bi, i: (0, 0)),
            pl.BlockSpec((tm, HEAD_DIM), lambda bi, i: (i, 0)),
            pl.BlockSpec((tm, HEAD_DIM), lambda bi, i: (i, 0)),
        ],
        out_specs=pl.BlockSpec((None, tm, ATTN_IN), lambda bi, i: (bi, i, 0)),
        out_shape=jax.ShapeDtypeStruct((b, s, ATTN_IN), MXU_DTYPE),
        compiler_params=_cparams(("parallel", "parallel"), 40),
        name="attn_project",
    )(h, mods, w_in, q_norm.reshape(1, HEAD_DIM), k_norm.reshape(1, HEAD_DIM), cos, sin)


def _stack_heads(q):
    g = q.shape[1] // HEAD_DIM
    return jnp.concatenate([q[:, i * HEAD_DIM:(i + 1) * HEAD_DIM] for i in range(g)], axis=0)


def _store_heads(o_ref, o, tq):
    for i in range(o.shape[0] // tq):
        o_ref[:, i * HEAD_DIM:(i + 1) * HEAD_DIM] = o[i * tq:(i + 1) * tq, :].astype(o_ref.dtype)


def _global_attn_kernel(q_ref, k_ref, v_ref, o_ref, m_scr, l_scr, acc_scr, *, t_lat, tq):
    s_tot = k_ref.shape[0]
    n_chunks = s_tot // KV_CHUNK
    first = jnp.where(pl.program_id(2) * tq >= t_lat, t_lat // KV_CHUNK, 0)
    q2 = _stack_heads(q_ref[...])
    m_scr[...] = jnp.full(m_scr.shape, -jnp.inf, F32)
    l_scr[...] = jnp.zeros(l_scr.shape, F32)
    acc_scr[...] = jnp.zeros(acc_scr.shape, F32)

    def step(c, carry):
        start = pl.multiple_of(c * KV_CHUNK, KV_CHUNK)
        kc = k_ref[pl.ds(start, KV_CHUNK), :]
        vc = v_ref[pl.ds(start, KV_CHUNK), :]
        sc = _dot_nt(q2, kc)
        m_prev = m_scr[...]
        m_new = jnp.maximum(m_prev, jnp.max(sc, axis=-1, keepdims=True))
        alpha = jnp.exp(m_prev - m_new)
        p = jnp.exp(sc - m_new)
        l_scr[...] = alpha * l_scr[...] + jnp.sum(p, axis=-1, keepdims=True)
        acc_scr[...] = alpha * acc_scr[...] + _dot(p.astype(MXU_DTYPE), vc)
        m_scr[...] = m_new
        return carry

    lax.fori_loop(first, n_chunks, step, 0)
    _store_heads(o_ref, acc_scr[...] / l_scr[...], tq)


def _global_attention(qkv, t_lat):
    b, s, _ = qkv.shape
    tq = ROW_TILE
    group = A_Q_HEADS // A_KV_HEADS
    k0 = A_QW // HEAD_DIM
    v0 = k0 + A_KV_HEADS
    return pl.pallas_call(
        functools.partial(_global_attn_kernel, t_lat=t_lat, tq=tq),
        grid=(b, A_KV_HEADS, s // tq),
        in_specs=[
            pl.BlockSpec((None, tq, group * HEAD_DIM), lambda bi, hk, i: (bi, i, hk)),
            pl.BlockSpec((None, s, HEAD_DIM), lambda bi, hk, i: (bi, 0, k0 + hk)),
            pl.BlockSpec((None, s, HEAD_DIM), lambda bi, hk, i: (bi, 0, v0 + hk)),
        ],
        out_specs=pl.BlockSpec((None, tq, group * HEAD_DIM), lambda bi, hk, i: (bi, i, hk)),
        out_shape=jax.ShapeDtypeStruct((b, s, A_QW), MXU_DTYPE),
        scratch_shapes=[
            pltpu.VMEM((group * tq, 1), F32),
            pltpu.VMEM((group * tq, 1), F32),
            pltpu.VMEM((group * tq, HEAD_DIM), F32),
        ],
        compiler_params=_cparams(("parallel", "parallel", "arbitrary"), 40),
        name="global_attention",
    )(qkv, qkv, qkv)


def _window_attn_kernel(sink_ref, q_ref, k_ref, v_ref, o_ref, *, t_lat, tq):
    s_tot = k_ref.shape[0]
    c_len = s_tot - t_lat
    band = tq + 2 * WINDOW
    hk = pl.program_id(1)
    q0 = pl.program_id(2) * tq
    is_ctx = q0 >= t_lat
    start = pl.multiple_of(jnp.clip(q0 - WINDOW, 0, t_lat - band), WINDOW)
    q2 = _stack_heads(q_ref[...])
    rows = q2.shape[0]
    group = rows // tq
    kb = k_ref[pl.ds(start, band), :]
    vb = v_ref[pl.ds(start, band), :]
    kc = k_ref[pl.ds(t_lat, c_len), :]
    vc = v_ref[pl.ds(t_lat, c_len), :]

    r = lax.broadcasted_iota(jnp.int32, (rows, band), 0)
    q_pos = q0 + (r & (tq - 1))
    k_pos = start + lax.broadcasted_iota(jnp.int32, (rows, band), 1)
    reach = jnp.where(is_ctx, -1, WINDOW)
    s_band = jnp.where(jnp.abs(k_pos - q_pos) <= reach, _dot_nt(q2, kb), -jnp.inf)
    s_ctx = _dot_nt(q2, kc)
    r1 = lax.broadcasted_iota(jnp.int32, (rows, 1), 0)
    s_sink = jnp.zeros((rows, 1), F32)
    for g in range(group):
        s_sink = jnp.where((r1 >= g * tq) & (r1 < (g + 1) * tq), sink_ref[hk * group + g], s_sink)
    m = jnp.maximum(jnp.maximum(jnp.max(s_band, axis=-1, keepdims=True),
                                jnp.max(s_ctx, axis=-1, keepdims=True)), s_sink)
    p_band = jnp.exp(s_band - m)
    p_ctx = jnp.exp(s_ctx - m)
    denom = (jnp.sum(p_band, axis=-1, keepdims=True) + jnp.sum(p_ctx, axis=-1, keepdims=True)
             + jnp.exp(s_sink - m))
    o = _dot(p_band.astype(MXU_DTYPE), vb) + _dot(p_ctx.astype(MXU_DTYPE), vc)
    _store_heads(o_ref, o / denom, tq)


def _window_attention(qkv, sink, t_lat):
    b, s, _ = qkv.shape
    tq = ROW_TILE
    assert tq & (tq - 1) == 0 and t_lat >= tq + 2 * WINDOW
    group = B_Q_HEADS // B_KV_HEADS
    q0 = (A_QW + 2 * A_KVW) // (group * HEAD_DIM)
    k0 = (A_QW + 2 * A_KVW + B_QW) // HEAD_DIM
    v0 = k0 + B_KV_HEADS
    return pl.pallas_call(
        functools.partial(_window_attn_kernel, t_lat=t_lat, tq=tq),
        grid=(b, B_KV_HEADS, s // tq),
        in_specs=[
            pl.BlockSpec(memory_space=pltpu.SMEM),
            pl.BlockSpec((None, tq, group * HEAD_DIM), lambda bi, hk, i: (bi, i, q0 + hk)),
            pl.BlockSpec((None, s, HEAD_DIM), lambda bi, hk, i: (bi, 0, k0 + hk)),
            pl.BlockSpec((None, s, HEAD_DIM), lambda bi, hk, i: (bi, 0, v0 + hk)),
        ],
        out_specs=pl.BlockSpec((None, tq, group * HEAD_DIM), lambda bi, hk, i: (bi, i, hk)),
        out_shape=jax.ShapeDtypeStruct((b, s, B_QW), MXU_DTYPE),
        compiler_params=_cparams(("parallel", "parallel", "arbitrary"), 40),
        name="window_attention",
    )(sink, qkv, qkv, qkv)


def _attn_out_kernel(oa_ref, ob_ref, h_ref, mod_ref, w_ref, lng_ref, lnb_ref, out_ref, *, alpha):
    y = _dot(oa_ref[...], w_ref[0:A_QW, :]) + _dot(ob_ref[...], w_ref[A_QW:A_QW + B_QW, :])
    out_ref[...] = _residual_layer_norm(h_ref[...], y, mod_ref[2:3, :], lng_ref[...], lnb_ref[...], alpha)


def _attn_out(o_a, o_b, h, mods, layer, w_out, ln_g, ln_b, t_lat, n_rows, alpha):
    b, s, d = h.shape
    tm = ROW_TILE
    row = _mod_row(t_lat, tm)
    return pl.pallas_call(
        functools.partial(_attn_out_kernel, alpha=alpha),
        grid=(b, n_rows // tm),
        in_specs=[
            pl.BlockSpec((None, tm, A_QW), lambda bi, i: (bi, i, 0)),
            pl.BlockSpec((None, tm, B_QW), lambda bi, i: (bi, i, 0)),
            pl.BlockSpec((None, tm, d), lambda bi, i: (bi, i, 0)),
            pl.BlockSpec((None, None, 6, d), lambda bi, i: (layer, row(bi, i), 0, 0)),
            pl.BlockSpec((A_QW + B_QW, d), lambda bi, i: (0, 0)),
            pl.BlockSpec((1, d), lambda bi, i: (0, 0)),
            pl.BlockSpec((1, d), lambda bi, i: (0, 0)),
        ],
        out_specs=pl.BlockSpec((None, tm, d), lambda bi, i: (bi, i, 0)),
        out_shape=jax.ShapeDtypeStruct((b, n_rows, d), F32),
        compiler_params=_cparams(("parallel", "parallel"), 40),
        name="attn_out_norm",
    )(o_a, o_b, h, mods, w_out, ln_g.reshape(1, d), ln_b.reshape(1, d))


def _log_sigmoid(x):
    return jnp.minimum(x, 0.0) - jnp.log(1.0 + jnp.exp(-jnp.abs(x)))


def _gla_proj_kernel(h_ref, mod_ref, w_ref, wa1_ref, wa2_ref, ba_ref, p_ref, ld_ref):
    a = (h_ref[...] * (1.0 + mod_ref[1:2, :]) + mod_ref[0:1, :]).astype(MXU_DTYPE)
    for cb in range(GLA_IN // MXU_N):
        z = _dot(a, w_ref[:, cb * MXU_N:(cb + 1) * MXU_N])
        if cb < GLA_KEY_DIM // MXU_N:
            z = z * (GLA_DK ** -0.5)
        p_ref[:, cb * MXU_N:(cb + 1) * MXU_N] = z.astype(p_ref.dtype)
    low = _dot(a, wa1_ref[...]).astype(MXU_DTYPE)
    for cb in range(2 * GLA_KEY_DIM // MXU_N):
        cols = slice(cb * MXU_N, (cb + 1) * MXU_N)
        logits = _dot(low, wa2_ref[:, cols]) + ba_ref[:, cols]
        ld_ref[:, cols] = _log_sigmoid(logits) / GLA_GATE_NORM


def _gla_project(h, mods, layer, w_in, w_a1, w_a2, b_a, t_lat):
    b, s, d = h.shape
    tm = ROW_TILE
    row = _mod_row(t_lat, tm)
    return pl.pallas_call(
        _gla_proj_kernel,
        grid=(b, s // tm),
        in_specs=[
            pl.BlockSpec((None, tm, d), lambda bi, i: (bi, i, 0)),
            pl.BlockSpec((None, None, 6, d), lambda bi, i: (layer, row(bi, i), 0, 0)),
            pl.BlockSpec((d, GLA_IN), lambda bi, i: (0, 0)),
            pl.BlockSpec((d, LANES), lambda bi, i: (0, 0)),
            pl.BlockSpec((LANES, 2 * GLA_KEY_DIM), lambda bi, i: (0, 0)),
            pl.BlockSpec((1, 2 * GLA_KEY_DIM), lambda bi, i: (0, 0)),
        ],
        out_specs=[
            pl.BlockSpec((None, tm, GLA_IN), lambda bi, i: (bi, i, 0)),
            pl.BlockSpec((None, tm, 2 * GLA_KEY_DIM), lambda bi, i: (bi, i, 0)),
        ],
        out_shape=[
            jax.ShapeDtypeStruct((b, s, GLA_IN), MXU_DTYPE),
            jax.ShapeDtypeStruct((b, s, 2 * GLA_KEY_DIM), F32),
        ],
        compiler_params=_cparams(("parallel", "parallel"), 48),
        name="gla_project",
    )(h, mods, w_in, w_a1, w_a2, b_a)


def _cumsum_rows(x):
    n = x.shape[0]
    row = lax.broadcasted_iota(jnp.int32, x.shape, 0)
    shift = 1
    while shift < n:
        x = x + jnp.where(row >= shift, pltpu.roll(x, shift, 0), 0.0)
        shift *= 2
    return x


def _block_anchor(p, n):
    rows, cols = p.shape
    half = n // 2
    if n >= 16:
        p3 = p.reshape(rows // n, n, cols)
        return jnp.broadcast_to(p3[:, half - 1:half, :], p3.shape).reshape(rows, cols)
    off = (lax.broadcasted_iota(jnp.int32, p.shape, 0) & (n - 1)) - (half - 1)
    a = p
    for d in range(-(half - 1), half + 1):
        if d != 0:
            a = jnp.where(off == d, pltpu.roll(p, d % rows, 0), a)
    return a


def _gla_scan_kernel(q_ref, k_ref, v_ref, ld_ref, o_ref, st_ref, *, reverse):
    chunk = q_ref.shape[0]

    @pl.when(pl.program_id(2) == 0)
    def _():
        st_ref[...] = jnp.zeros(st_ref.shape, F32)

    q = q_ref[...].astype(F32)
    k = k_ref[...].astype(F32)
    v = v_ref[...]
    ld = ld_ref[...]
    p_inc = _cumsum_rows(ld)
    p_tot = p_inc[chunk - 1:chunk, :]
    px = p_inc - ld if reverse else p_inc

    row = lax.broadcasted_iota(jnp.int32, (chunk, chunk), 0)
    col = lax.broadcasted_iota(jnp.int32, (chunk, chunk), 1)
    row_d = lax.broadcasted_iota(jnp.int32, q.shape, 0)
    att = jnp.where(row == col, jnp.sum(q * k, axis=-1, keepdims=True), 0.0)
    n = chunk
    while n >= 2:
        shift = n.bit_length() - 1
        upper = (row_d & (n - 1)) >= n // 2
        anchor = _block_anchor(p_inc, n)
        e = jnp.exp(jnp.where(upper, px - anchor, anchor - px))
        q_role = jnp.logical_not(upper) if reverse else upper
        qt = jnp.where(q_role, q * e, 0.0).astype(MXU_DTYPE)
        kt = jnp.where(q_role, 0.0, k * e).astype(MXU_DTYPE)
        s_lvl = _dot_nt(qt, kt)
        if n < chunk:
            s_lvl = jnp.where((row >> shift) == (col >> shift), s_lvl, 0.0)
        att = att + s_lvl
        n //= 2

    if reverse:
        qd = q * jnp.exp(p_tot - px)
        kd = k * jnp.exp(px)
    else:
        qd = q * jnp.exp(px)
        kd = k * jnp.exp(p_tot - px)
    state = st_ref[...]
    o = _dot(att.astype(MXU_DTYPE), v) + _dot_nt(qd.astype(MXU_DTYPE), state.astype(MXU_DTYPE))
    o_ref[...] = o.astype(o_ref.dtype)
    st_ref[...] = state * jnp.exp(p_tot) + _dot_tn(v, kd.astype(MXU_DTYPE))


def _gla_scan(proj, ld, t_lat, reverse):
    b, s, _ = proj.shape
    ch = GLA_CHUNK
    n_chunks = s // ch
    lat_chunks = t_lat // ch
    if reverse:
        def chunk_of(c):
            return n_chunks - 1 - c
    else:
        def chunk_of(c):
            return (c + lat_chunks) % n_chunks
    k0 = GLA_KEY_DIM // GLA_DK
    v0 = 2 * GLA_KEY_DIM // GLA_DV
    d0 = GLA_HEADS if reverse else 0
    return pl.pallas_call(
        functools.partial(_gla_scan_kernel, reverse=reverse),
        grid=(b, GLA_HEADS, n_chunks),
        in_specs=[
            pl.BlockSpec((None, ch, GLA_DK), lambda bi, h, c: (bi, chunk_of(c), h)),
            pl.BlockSpec((None, ch, GLA_DK), lambda bi, h, c: (bi, chunk_of(c), k0 + h)),
            pl.BlockSpec((None, ch, GLA_DV), lambda bi, h, c: (bi, chunk_of(c), v0 + h)),
            pl.BlockSpec((None, ch, GLA_DK), lambda bi, h, c: (bi, chunk_of(c), d0 + h)),
        ],
        out_specs=pl.BlockSpec((None, ch, GLA_DV), lambda bi, h, c: (bi, chunk_of(c), h)),
        out_shape=jax.ShapeDtypeStruct((b, s, GLA_VAL_DIM), F32),
        scratch_shapes=[pltpu.VMEM((GLA_DV, GLA_DK), F32)],
        compiler_params=_cparams(("parallel", "parallel", "arbitrary"), 32),
        name="gla_scan_bwd" if reverse else "gla_scan_fwd",
    )(proj, proj, proj, ld)


def _gla_out_kernel(of_ref, ob_ref, g_ref, h_ref, mod_ref, hn_ref, w_ref, lng_ref, lnb_ref, out_ref, *, alpha):
    o = of_ref[...] + ob_ref[...]
    gate = _silu(g_ref[...].astype(F32))
    y = jnp.zeros(h_ref.shape, F32)
    for hd in range(GLA_HEADS):
        cols = slice(hd * GLA_DV, (hd + 1) * GLA_DV)
        oh = o[:, cols]
        oh = oh * lax.rsqrt(jnp.mean(oh * oh, axis=-1, keepdims=True) + EPS) * hn_ref[...]
        y = y + _dot((oh * gate[:, cols]).astype(MXU_DTYPE), w_ref[cols, :])
    out_ref[...] = _residual_layer_norm(h_ref[...], y, mod_ref[2:3, :], lng_ref[...], lnb_ref[...], alpha)


def _gla_out(o_f, o_b, proj, h, mods, layer, head_norm, w_out, ln_g, ln_b, t_lat, n_rows, alpha):
    b, s, d = h.shape
    tm = ROW_TILE
    row = _mod_row(t_lat, tm)
    g_blk = (2 * GLA_KEY_DIM + GLA_VAL_DIM) // GLA_VAL_DIM
    return pl.pallas_call(
        functools.partial(_gla_out_kernel, alpha=alpha),
        grid=(b, n_rows // tm),
        in_specs=[
            pl.BlockSpec((None, tm, GLA_VAL_DIM), lambda bi, i: (bi, i, 0)),
            pl.BlockSpec((None, tm, GLA_VAL_DIM), lambda bi, i: (bi, i, 0)),
            pl.BlockSpec((None, tm, GLA_VAL_DIM), lambda bi, i: (bi, i, g_blk)),
            pl.BlockSpec((None, tm, d), lambda bi, i: (bi, i, 0)),
            pl.BlockSpec((None, None, 6, d), lambda bi, i: (layer, row(bi, i), 0, 0)),
            pl.BlockSpec((1, GLA_DV), lambda bi, i: (0, 0)),
            pl.BlockSpec((GLA_VAL_DIM, d), lambda bi, i: (0, 0)),
            pl.BlockSpec((1, d), lambda bi, i: (0, 0)),
            pl.BlockSpec((1, d), lambda bi, i: (0, 0)),
        ],
        out_specs=pl.BlockSpec((None, tm, d), lambda bi, i: (bi, i, 0)),
        out_shape=jax.ShapeDtypeStruct((b, n_rows, d), F32),
        compiler_params=_cparams(("parallel", "parallel"), 40),
        name="gla_out_norm",
    )(o_f, o_b, proj, h, mods, head_norm.reshape(1, GLA_DV), w_out, ln_g.reshape(1, d), ln_b.reshape(1, d))


def _ffn_kernel(h_ref, hp_ref, hn_ref, mod_ref, wup_ref, cw_ref, cb_ref, wdn_ref, lng_ref, lnb_ref,
                out_ref, a_scr, ug_scr, uv_scr, *, t_lat, s_tot, alpha):
    tm = h_ref.shape[0]
    d_ff = wdn_ref.shape[0]
    row0 = pl.program_id(1) * tm
    at_start = (row0 == 0) | (row0 == t_lat)
    at_end = (row0 + tm == t_lat) | (row0 + tm == s_tot)
    shift = mod_ref[3:4, :]
    scale = 1.0 + mod_ref[4:5, :]
    h = h_ref[...]
    a_scr[0:HALO, :] = jnp.where(at_start, 0.0, hp_ref[...] * scale + shift).astype(a_scr.dtype)
    a_scr[HALO:HALO + tm, :] = (h * scale + shift).astype(a_scr.dtype)
    a_scr[HALO + tm:2 * HALO + tm, :] = jnp.where(at_end, 0.0, hn_ref[...] * scale + shift).astype(a_scr.dtype)
    a = a_scr[...]

    def conv(u_scr, c0):
        cols = slice(c0, c0 + FF_CHUNK)
        acc = cb_ref[:, cols] + cw_ref[0:1, cols] * u_scr[pl.ds(HALO - 1, tm), :]
        for j in range(1, CONV_W):
            acc = acc + cw_ref[j:j + 1, cols] * u_scr[pl.ds(HALO - 1 + j, tm), :]
        return acc

    y = jnp.zeros(h.shape, F32)
    for c in range(d_ff // FF_CHUNK):
        g0 = c * FF_CHUNK
        v0 = d_ff + c * FF_CHUNK
        ug_scr[...] = _dot(a, wup_ref[:, g0:g0 + FF_CHUNK])
        uv_scr[...] = _dot(a, wup_ref[:, v0:v0 + FF_CHUNK])
        act = _silu(conv(ug_scr, g0)) * conv(uv_scr, v0)
        y = y + _dot(act.astype(MXU_DTYPE), wdn_ref[g0:g0 + FF_CHUNK, :])
    out_ref[...] = _residual_layer_norm(h, y, mod_ref[5:6, :], lng_ref[...], lnb_ref[...], alpha)


def _conv_ffn(h, mods, layer, w_up, conv_w, conv_b, w_down, ln_g, ln_b, t_lat, n_rows, alpha):
    b, s, d = h.shape
    tm = ROW_TILE
    d_ff = w_down.shape[0]
    row = _mod_row(t_lat, tm)
    hb = tm // HALO
    n_halo = s // HALO
    return pl.pallas_call(
        functools.partial(_ffn_kernel, t_lat=t_lat, s_tot=s, alpha=alpha),
        grid=(b, n_rows // tm),
        in_specs=[
            pl.BlockSpec((None, tm, d), lambda bi, i: (bi, i, 0)),
            pl.BlockSpec((None, HALO, d), lambda bi, i: (bi, jnp.maximum(i * hb - 1, 0), 0)),
            pl.BlockSpec((None, HALO, d), lambda bi, i: (bi, jnp.minimum((i + 1) * hb, n_halo - 1), 0)),
            pl.BlockSpec((None, None, 6, d), lambda bi, i: (layer, row(bi, i), 0, 0)),
            pl.BlockSpec((d, 2 * d_ff), lambda bi, i: (0, 0)),
            pl.BlockSpec((CONV_W, 2 * d_ff), lambda bi, i: (0, 0)),
            pl.BlockSpec((1, 2 * d_ff), lambda bi, i: (0, 0)),
            pl.BlockSpec((d_ff, d), lambda bi, i: (0, 0)),
            pl.BlockSpec((1, d), lambda bi, i: (0, 0)),
            pl.BlockSpec((1, d), lambda bi, i: (0, 0)),
        ],
        out_specs=pl.BlockSpec((None, tm, d), lambda bi, i: (bi, i, 0)),
        out_shape=jax.ShapeDtypeStruct((b, n_rows, d), F32),
        scratch_shapes=[
            pltpu.VMEM((tm + 2 * HALO, d), MXU_DTYPE),
            pltpu.VMEM((tm + 2 * HALO, FF_CHUNK), F32),
            pltpu.VMEM((tm + 2 * HALO, FF_CHUNK), F32),
        ],
        compiler_params=_cparams(("parallel", "parallel"), 56),
        name="conv_ffn_norm",
    )(h, h, h, mods, w_up, conv_w, conv_b.reshape(1, 2 * d_ff), w_down, ln_g.reshape(1, d), ln_b.reshape(1, d))


def _rope_tables(t_lat, c_len):
    rows = t_lat // GRID_W
    row = jnp.repeat(jnp.arange(rows, dtype=F32), GRID_W)
    col = jnp.tile(jnp.arange(GRID_W, dtype=F32), rows)
    inv_freq = jnp.power(ROPE_THETA, -jnp.arange(ROPE_AXIS_DIM // 2, dtype=F32) * 2.0 / ROPE_AXIS_DIM)
    ang = jnp.concatenate([row[:, None] * inv_freq, col[:, None] * inv_freq], axis=-1)
    ang = jnp.concatenate([ang, jnp.zeros((c_len, HEAD_DIM // 2), F32)], axis=0)
    cos, sin = jnp.cos(ang), jnp.sin(ang)
    return jnp.concatenate([cos, cos], axis=-1), jnp.concatenate([-sin, sin], axis=-1)


def kernel(x, c, ctx, c_ctx, ada_w, ada_b, ln_g, ln_b, ffn_w_up, ffn_conv_w, ffn_conv_b, ffn_w_down, attn_w_in, attn_q_norm, attn_k_norm, attn_sink, attn_w_out, gla_w_in, gla_w_a1, gla_w_a2, gla_b_a, gla_head_norm, gla_w_out):
    b, t_lat, d = x.shape
    c_len = ctx.shape[1]
    s = t_lat + c_len
    depth = ada_w.shape[0]
    assert b < MOD_CTX_ROW and t_lat % ROW_TILE == 0 and c_len % ROW_TILE == 0
    assert t_lat % GLA_CHUNK == 0 and c_len % GLA_CHUNK == 0 and c_len % KV_CHUNK == 0
    alpha = (2 * depth) ** 0.25

    cond = jnp.zeros((MOD_ROWS, d), F32).at[:b].set(c).at[MOD_CTX_ROW].set(c_ctx)
    mods = _modulation(cond, ada_w, ada_b).reshape(depth, MOD_ROWS, 6, d)
    cos, sin = _rope_tables(t_lat, c_len)
    h = jnp.concatenate([x, ctx], axis=1)

    for i in range(depth):
        last = i == depth - 1
        n_rows = t_lat if last else s
        j = i // 2
        if i % 2 == 0:
            qkv = _attn_project(h, mods, i, attn_w_in[j].astype(MXU_DTYPE), attn_q_norm[j], attn_k_norm[j],
                                cos, sin, t_lat)
            o_a = _global_attention(qkv, t_lat)
            o_b = _window_attention(qkv, attn_sink[j], t_lat)
            h1 = _attn_out(o_a, o_b, h, mods, i, attn_w_out[j].astype(MXU_DTYPE), ln_g[i, 0], ln_b[i, 0],
                           t_lat, n_rows, alpha)
        else:
            w_a1 = jnp.concatenate([gla_w_a1[j, 0], gla_w_a1[j, 1]], axis=1)
            w_a1 = jnp.pad(w_a1, ((0, 0), (0, LANES - 2 * GLA_GATE_RANK))).astype(MXU_DTYPE)
            w_a2 = jnp.zeros((LANES, 2 * GLA_KEY_DIM), F32)
            w_a2 = w_a2.at[:GLA_GATE_RANK, :GLA_KEY_DIM].set(gla_w_a2[j, 0])
            w_a2 = w_a2.at[GLA_GATE_RANK:2 * GLA_GATE_RANK, GLA_KEY_DIM:].set(gla_w_a2[j, 1]).astype(MXU_DTYPE)
            proj, ld = _gla_project(h, mods, i, gla_w_in[j].astype(MXU_DTYPE), w_a1, w_a2,
                                    gla_b_a[j].reshape(1, 2 * GLA_KEY_DIM), t_lat)
            o_f = _gla_scan(proj, ld, t_lat, reverse=False)
            o_r = _gla_scan(proj, ld, t_lat, reverse=True)
            h1 = _gla_out(o_f, o_r, proj, h, mods, i, gla_head_norm[j], gla_w_out[j].astype(MXU_DTYPE),
                          ln_g[i, 0], ln_b[i, 0], t_lat, n_rows, alpha)
        h = _conv_ffn(h1, mods, i, ffn_w_up[i].astype(MXU_DTYPE), ffn_conv_w[i], ffn_conv_b[i],
                      ffn_w_down[i].astype(MXU_DTYPE), ln_g[i, 1], ln_b[i, 1], t_lat, n_rows, alpha)
    return h
```

```python
import functools

import jax
import jax.numpy as jnp
from jax import lax
from jax.experimental import pallas as pl
from jax.experimental.pallas import tpu as pltpu

GRID_W = 64
HEAD_DIM = 128
A_Q_HEADS = 4
A_KV_HEADS = 2
B_Q_HEADS = 4
B_KV_HEADS = 2
A_QW = A_Q_HEADS * HEAD_DIM
A_KVW = A_KV_HEADS * HEAD_DIM
B_QW = B_Q_HEADS * HEAD_DIM
B_KVW = B_KV_HEADS * HEAD_DIM
ATTN_IN = A_QW + 2 * A_KVW + B_QW + 2 * B_KVW
ATTN_SCALE = HEAD_DIM ** -0.5
WINDOW = 128
ROPE_THETA = 10000.0
ROPE_AXIS_DIM = HEAD_DIM // 2
GLA_HEADS = 4
GLA_DK = 128
GLA_DV = 256
GLA_KEY_DIM = GLA_HEADS * GLA_DK
GLA_VAL_DIM = GLA_HEADS * GLA_DV
GLA_IN = 2 * GLA_KEY_DIM + 2 * GLA_VAL_DIM
GLA_GATE_RANK = 16
GLA_GATE_NORM = 16.0
CONV_W = 3
EPS = 1e-6

LANES = 128
MXU_N = 256
VMEM_BYTES = 64 * 2 ** 20

MXU_DTYPE = jnp.bfloat16
F32 = jnp.float32

ROW_TILE = 256
HALO = 16
FF_CHUNK = 256
KV_CHUNK = 768
GLA_CHUNK = 128
MOD_COLS = 1536


def _cparams(sem, vmem_mib):
    return pltpu.CompilerParams(dimension_semantics=sem, vmem_limit_bytes=vmem_mib * 2 ** 20)


def _dot(a, b):
    return jnp.dot(a, b, preferred_element_type=F32)


def _dot_nt(a, b):
    return lax.dot_general(a, b, (((1,), (1,)), ((), ())), preferred_element_type=F32)


def _dot_tn(a, b):
    return lax.dot_general(a, b, (((0,), (0,)), ((), ())), preferred_element_type=F32)


def _silu(x):
    return x / (1.0 + jnp.exp(-x))


def _residual_layer_norm(h, y, gate, ln_g, ln_b, alpha):
    r = alpha * h + gate * y
    mu = jnp.mean(r, axis=-1, keepdims=True)
    d = r - mu
    var = jnp.mean(d * d, axis=-1, keepdims=True)
    return d * lax.rsqrt(var + EPS) * ln_g + ln_b


def _mod_row(t_lat, tm):
    n_lat = t_lat // tm

    def row(b, i):
        return jnp.where(i < n_lat, b, MOD_CTX_ROW)
    return row


MOD_ROWS = 8
MOD_CTX_ROW = MOD_ROWS - 1


def _mod_kernel(cond_ref, w_ref, b_ref, o_ref):
    a = _silu(cond_ref[...]).astype(MXU_DTYPE)
    o_ref[...] = _dot(a, w_ref[...].astype(MXU_DTYPE)) + b_ref[...]


def _modulation(cond, ada_w, ada_b):
    depth, d, n = ada_w.shape
    return pl.pallas_call(
        _mod_kernel,
        grid=(depth, n // MOD_COLS),
        in_specs=[
            pl.BlockSpec((MOD_ROWS, d), lambda l, j: (0, 0)),
            pl.BlockSpec((None, d, MOD_COLS), lambda l, j: (l, 0, j)),
            pl.BlockSpec((None, 1, MOD_COLS), lambda l, j: (l, 0, j)),
        ],
        out_specs=pl.BlockSpec((None, MOD_ROWS, MOD_COLS), lambda l, j: (l, 0, j)),
        out_shape=jax.ShapeDtypeStruct((depth, MOD_ROWS, n), F32),
        compiler_params=_cparams(("parallel", "parallel"), 32),
        name="ada_modulation",
    )(cond, ada_w, ada_b.reshape(depth, 1, n))


def _attn_proj_kernel(h_ref, mod_ref, w_ref, qn_ref, kn_ref, cos_ref, sin_ref, o_ref):
    a = (h_ref[...] * (1.0 + mod_ref[1:2, :]) + mod_ref[0:1, :]).astype(MXU_DTYPE)
    cos = cos_ref[...]
    sin = sin_ref[...]

    def rms(z, g):
        return z * lax.rsqrt(jnp.mean(z * z, axis=-1, keepdims=True) + EPS) * g

    def rope(z):
        return z * cos + pltpu.roll(z, HEAD_DIM // 2, 1) * sin

    a_k0 = A_QW // HEAD_DIM
    a_v0 = a_k0 + A_KV_HEADS
    b_q0 = a_v0 + A_KV_HEADS
    b_k0 = b_q0 + B_Q_HEADS
    b_v0 = b_k0 + B_KV_HEADS
    heads_per_dot = MXU_N // HEAD_DIM
    for cb in range(ATTN_IN // MXU_N):
        z2 = _dot(a, w_ref[:, cb * MXU_N:(cb + 1) * MXU_N])
        for half in range(heads_per_dot):
            hb = cb * heads_per_dot + half
            z = z2[:, half * HEAD_DIM:(half + 1) * HEAD_DIM]
            if hb < a_k0:
                z = rope(rms(z, qn_ref[...])) * ATTN_SCALE
            elif hb < a_v0:
                z = rope(rms(z, kn_ref[...]))
            elif hb < b_q0:
                pass
            elif hb < b_k0:
                z = rope(z) * ATTN_SCALE
            elif hb < b_v0:
                z = rope(z)
            o_ref[:, hb * HEAD_DIM:(hb + 1) * HEAD_DIM] = z.astype(o_ref.dtype)


def _attn_project(h, mods, layer, w_in, q_norm, k_norm, cos, sin, t_lat):
    b, s, d = h.shape
    tm = ROW_TILE
    row = _mod_row(t_lat, tm)
    return pl.pallas_call(
        _attn_proj_kernel,
        grid=(b, s // tm),
        in_specs=[
            pl.BlockSpec((None, tm, d), lambda bi, i: (bi, i, 0)),
            pl.BlockSpec((None, None, 6, d), lambda bi, i: (layer, row(bi, i), 0, 0)),
            pl.BlockSpec((d, ATTN_IN), lambda bi, i: (0, 0)),
            pl.BlockSpec((1, HEAD_DIM), lambda bi, i: (0, 0)),
            pl.BlockSpec((1, HEAD_DIM), lambda bi, i: (0, 0)),
            pl.BlockSpec((tm, HEAD_DIM), lambda bi, i: (i, 0)),
            pl.BlockSpec((tm, HEAD_DIM), lambda bi, i: (i, 0)),
        ],
        out_specs=pl.BlockSpec((None, tm, ATTN_IN), lambda bi, i: (bi, i, 0)),
        out_shape=jax.ShapeDtypeStruct((b, s, ATTN_IN), MXU_DTYPE),
        compiler_params=_cparams(("parallel", "parallel"), 40),
        name="attn_project",
    )(h, mods, w_in, q_norm.reshape(1, HEAD_DIM), k_norm.reshape(1, HEAD_DIM), cos, sin)


def _stack_heads(q):
    g = q.shape[1] // HEAD_DIM
    return jnp.concatenate([q[:, i * HEAD_DIM:(i + 1) * HEAD_DIM] for i in range(g)], axis=0)


def _store_heads(o_ref, o, tq):
    for i in range(o.shape[0] // tq):
        o_ref[:, i * HEAD_DIM:(i + 1) * HEAD_DIM] = o[i * tq:(i + 1) * tq, :].astype(o_ref.dtype)


def _online_softmax_step(q2, kc, vc, m_scr, l_scr, acc_scr):
    sc = _dot_nt(q2, kc)
    blocks = [sc[:, j * LANES:(j + 1) * LANES] for j in range(sc.shape[1] // LANES)]
    mx = blocks[0]
    for blk in blocks[1:]:
        mx = jnp.maximum(mx, blk)
    m_prev = m_scr[...]
    m_new = jnp.maximum(m_prev, jnp.max(mx, axis=-1, keepdims=True))
    alpha = jnp.exp(m_prev - m_new)
    ps = [jnp.exp(blk - m_new) for blk in blocks]
    l_new = alpha * l_scr[...]
    for p in ps:
        l_new = l_new + p
    p_all = jnp.concatenate([p.astype(MXU_DTYPE) for p in ps], axis=1)
    acc_scr[...] = alpha * acc_scr[...] + _dot(p_all, vc)
    l_scr[...] = l_new
    m_scr[...] = m_new


def _global_attn_kernel(q_ref, k_ref, v_ref, o_ref, m_scr, l_scr, acc_scr, *, t_lat, tq, kv_chunk):
    s_tot = k_ref.shape[0]
    is_ctx = pl.program_id(2) * tq >= t_lat
    q2 = _stack_heads(q_ref[...])
    m_scr[...] = jnp.full(m_scr.shape, -jnp.inf, F32)
    l_scr[...] = jnp.zeros(l_scr.shape, F32)
    acc_scr[...] = jnp.zeros(acc_scr.shape, F32)

    @pl.when(jnp.logical_not(is_ctx))
    def _():
        for c in range(s_tot // kv_chunk):
            rows = slice(c * kv_chunk, (c + 1) * kv_chunk)
            _online_softmax_step(q2, k_ref[rows, :], v_ref[rows, :], m_scr, l_scr, acc_scr)

    @pl.when(is_ctx)
    def _():
        _online_softmax_step(q2, k_ref[t_lat:s_tot, :], v_ref[t_lat:s_tot, :], m_scr, l_scr, acc_scr)

    denom = jnp.sum(l_scr[...], axis=-1, keepdims=True)
    _store_heads(o_ref, acc_scr[...] / denom, tq)


def _kv_chunk(s_tot):
    return max(n for n in range(MXU_N, KV_CHUNK + 1, MXU_N) if s_tot % n == 0)


def _global_attention(qkv, t_lat):
    b, s, _ = qkv.shape
    tq = ROW_TILE
    group = A_Q_HEADS // A_KV_HEADS
    k0 = A_QW // HEAD_DIM
    v0 = k0 + A_KV_HEADS
    return pl.pallas_call(
        functools.partial(_global_attn_kernel, t_lat=t_lat, tq=tq, kv_chunk=_kv_chunk(s)),
        grid=(b, A_KV_HEADS, s // tq),
        in_specs=[
            pl.BlockSpec((None, tq, group * HEAD_DIM), lambda bi, hk, i: (bi, i, hk)),
            pl.BlockSpec((None, s, HEAD_DIM), lambda bi, hk, i: (bi, 0, k0 + hk)),
            pl.BlockSpec((None, s, HEAD_DIM), lambda bi, hk, i: (bi, 0, v0 + hk)),
        ],
        out_specs=pl.BlockSpec((None, tq, group * HEAD_DIM), lambda bi, hk, i: (bi, i, hk)),
        out_shape=jax.ShapeDtypeStruct((b, s, A_QW), MXU_DTYPE),
        scratch_shapes=[
            pltpu.VMEM((group * tq, LANES), F32),
            pltpu.VMEM((group * tq, LANES), F32),
            pltpu.VMEM((group * tq, HEAD_DIM), F32),
        ],
        compiler_params=_cparams(("parallel", "parallel", "arbitrary"), 40),
        name="global_attention",
    )(qkv, qkv, qkv)


def _window_attn_kernel(sink_ref, q_ref, k_ref, v_ref, o_ref, *, t_lat, tq):
    s_tot = k_ref.shape[0]
    c_len = s_tot - t_lat
    band = tq + 2 * WINDOW
    hk = pl.program_id(1)
    q0 = pl.program_id(2) * tq
    is_ctx = q0 >= t_lat
    start = pl.multiple_of(jnp.clip(q0 - WINDOW, 0, t_lat - band), WINDOW)
    q2 = _stack_heads(q_ref[...])
    rows = q2.shape[0]
    group = rows // tq
    kb = k_ref[pl.ds(start, band), :]
    vb = v_ref[pl.ds(start, band), :]
    kc = k_ref[pl.ds(t_lat, c_len), :]
    vc = v_ref[pl.ds(t_lat, c_len), :]

    r = lax.broadcasted_iota(jnp.int32, (rows, band), 0)
    q_pos = q0 + (r & (tq - 1))
    k_pos = start + lax.broadcasted_iota(jnp.int32, (rows, band), 1)
    reach = jnp.where(is_ctx, -1, WINDOW)
    s_band = jnp.where(jnp.abs(k_pos - q_pos) <= reach, _dot_nt(q2, kb), -jnp.inf)
    s_ctx = _dot_nt(q2, kc)
    r1 = lax.broadcasted_iota(jnp.int32, (rows, 1), 0)
    s_sink = jnp.zeros((rows, 1), F32)
    for g in range(group):
        s_sink = jnp.where((r1 >= g * tq) & (r1 < (g + 1) * tq), sink_ref[hk * group + g], s_sink)
    m = jnp.maximum(jnp.maximum(jnp.max(s_band, axis=-1, keepdims=True),
                                jnp.max(s_ctx, axis=-1, keepdims=True)), s_sink)
    p_band = jnp.exp(s_band - m)
    p_ctx = jnp.exp(s_ctx - m)
    denom = (jnp.sum(p_band, axis=-1, keepdims=True) + jnp.sum(p_ctx, axis=-1, keepdims=True)
             + jnp.exp(s_sink - m))
    o = _dot(p_band.astype(MXU_DTYPE), vb) + _dot(p_ctx.astype(MXU_DTYPE), vc)
    _store_heads(o_ref, o / denom, tq)


def _window_attention(qkv, sink, t_lat):
    b, s, _ = qkv.shape
    tq = ROW_TILE
    assert tq & (tq - 1) == 0 and t_lat >= tq + 2 * WINDOW
    group = B_Q_HEADS // B_KV_HEADS
    q0 = (A_QW + 2 * A_KVW) // (group * HEAD_DIM)
    k0 = (A_QW + 2 * A_KVW + B_QW) // HEAD_DIM
    v0 = k0 + B_KV_HEADS
    return pl.pallas_call(
        functools.partial(_window_attn_kernel, t_lat=t_lat, tq=tq),
        grid=(b, B_KV_HEADS, s // tq),
        in_specs=[
            pl.BlockSpec(memory_space=pltpu.SMEM),
            pl.BlockSpec((None, tq, group * HEAD_DIM), lambda bi, hk, i: (bi, i, q0 + hk)),
            pl.BlockSpec((None, s, HEAD_DIM), lambda bi, hk, i: (bi, 0, k0 + hk)),
            pl.BlockSpec((None, s, HEAD_DIM), lambda bi, hk, i: (bi, 0, v0 + hk)),
        ],
        out_specs=pl.BlockSpec((None, tq, group * HEAD_DIM), lambda bi, hk, i: (bi, i, hk)),
        out_shape=jax.ShapeDtypeStruct((b, s, B_QW), MXU_DTYPE),
        compiler_params=_cparams(("parallel", "parallel", "arbitrary"), 40),
        name="window_attention",
    )(sink, qkv, qkv, qkv)


def _attn_out_kernel(oa_ref, ob_ref, h_ref, mod_ref, w_ref, lng_ref, lnb_ref, out_ref, *, alpha):
    y = _dot(oa_ref[...], w_ref[0:A_QW, :]) + _dot(ob_ref[...], w_ref[A_QW:A_QW + B_QW, :])
    out_ref[...] = _residual_layer_norm(h_ref[...], y, mod_ref[2:3, :], lng_ref[...], lnb_ref[...], alpha)


def _attn_out(o_a, o_b, h, mods, layer, w_out, ln_g, ln_b, t_lat, n_rows, alpha):
    b, s, d = h.shape
    tm = ROW_TILE
    row = _mod_row(t_lat, tm)
    return pl.pallas_call(
        functools.partial(_attn_out_kernel, alpha=alpha),
        grid=(b, n_rows // tm),
        in_specs=[
            pl.BlockSpec((None, tm, A_QW), lambda bi, i: (bi, i, 0)),
            pl.BlockSpec((None, tm, B_QW), lambda bi, i: (bi, i, 0)),
            pl.BlockSpec((None, tm, d), lambda bi, i: (bi, i, 0)),
            pl.BlockSpec((None, None, 6, d), lambda bi, i: (layer, row(bi, i), 0, 0)),
            pl.BlockSpec((A_QW + B_QW, d), lambda bi, i: (0, 0)),
            pl.BlockSpec((1, d), lambda bi, i: (0, 0)),
            pl.BlockSpec((1, d), lambda bi, i: (0, 0)),
        ],
        out_specs=pl.BlockSpec((None, tm, d), lambda bi, i: (bi, i, 0)),
        out_shape=jax.ShapeDtypeStruct((b, n_rows, d), F32),
        compiler_params=_cparams(("parallel", "parallel"), 40),
        name="attn_out_norm",
    )(o_a, o_b, h, mods, w_out, ln_g.reshape(1, d), ln_b.reshape(1, d))


def _log_sigmoid(x):
    return jnp.minimum(x, 0.0) - jnp.log(1.0 + jnp.exp(-jnp.abs(x)))


def _gla_proj_kernel(h_ref, mod_ref, w_ref, wa1_ref, wa2_ref, ba_ref, p_ref, ld_ref):
    a = (h_ref[...] * (1.0 + mod_ref[1:2, :]) + mod_ref[0:1, :]).astype(MXU_DTYPE)
    for cb in range(GLA_IN // MXU_N):
        z = _dot(a, w_ref[:, cb * MXU_N:(cb + 1) * MXU_N])
        if cb < GLA_KEY_DIM // MXU_N:
            z = z * (GLA_DK ** -0.5)
        p_ref[:, cb * MXU_N:(cb + 1) * MXU_N] = z.astype(p_ref.dtype)
    low = _dot(a, wa1_ref[...]).astype(MXU_DTYPE)
    for cb in range(2 * GLA_KEY_DIM // MXU_N):
        cols = slice(cb * MXU_N, (cb + 1) * MXU_N)
        logits = _dot(low, wa2_ref[:, cols]) + ba_ref[:, cols]
        ld_ref[:, cols] = _log_sigmoid(logits) / GLA_GATE_NORM


def _gla_project(h, mods, layer, w_in, w_a1, w_a2, b_a, t_lat):
    b, s, d = h.shape
    tm = ROW_TILE
    row = _mod_row(t_lat, tm)
    return pl.pallas_call(
        _gla_proj_kernel,
        grid=(b, s // tm),
        in_specs=[
            pl.BlockSpec((None, tm, d), lambda bi, i: (bi, i, 0)),
            pl.BlockSpec((None, None, 6, d), lambda bi, i: (layer, row(bi, i), 0, 0)),
            pl.BlockSpec((d, GLA_IN), lambda bi, i: (0, 0)),
            pl.BlockSpec((d, LANES), lambda bi, i: (0, 0)),
            pl.BlockSpec((LANES, 2 * GLA_KEY_DIM), lambda bi, i: (0, 0)),
            pl.BlockSpec((1, 2 * GLA_KEY_DIM), lambda bi, i: (0, 0)),
        ],
        out_specs=[
            pl.BlockSpec((None, tm, GLA_IN), lambda bi, i: (bi, i, 0)),
            pl.BlockSpec((None, tm, 2 * GLA_KEY_DIM), lambda bi, i: (bi, i, 0)),
        ],
        out_shape=[
            jax.ShapeDtypeStruct((b, s, GLA_IN), MXU_DTYPE),
            jax.ShapeDtypeStruct((b, s, 2 * GLA_KEY_DIM), F32),
        ],
        compiler_params=_cparams(("parallel", "parallel"), 48),
        name="gla_project",
    )(h, mods, w_in, w_a1, w_a2, b_a)


def _cumsum_rows(x):
    n = x.shape[0]
    row = lax.broadcasted_iota(jnp.int32, x.shape, 0)
    shift = 1
    while shift < n:
        x = x + jnp.where(row >= shift, pltpu.roll(x, shift, 0), 0.0)
        shift *= 2
    return x


def _block_anchor(p, n):
    rows, cols = p.shape
    half = n // 2
    if n >= 16:
        p3 = p.reshape(rows // n, n, cols)
        return jnp.broadcast_to(p3[:, half - 1:half, :], p3.shape).reshape(rows, cols)
    off = (lax.broadcasted_iota(jnp.int32, p.shape, 0) & (n - 1)) - (half - 1)
    a = p
    for d in range(-(half - 1), half + 1):
        if d != 0:
            a = jnp.where(off == d, pltpu.roll(p, d % rows, 0), a)
    return a


def _gla_scan_kernel(q_ref, k_ref, v_ref, ld_ref, o_ref, st_ref, *, reverse):
    chunk = q_ref.shape[0]

    @pl.when(pl.program_id(2) == 0)
    def _():
        st_ref[...] = jnp.zeros(st_ref.shape, F32)

    q = q_ref[...].astype(F32)
    k = k_ref[...].astype(F32)
    v = v_ref[...]
    ld = ld_ref[...]
    p_inc = _cumsum_rows(ld)
    p_tot = p_inc[chunk - 1:chunk, :]
    px = p_inc - ld if reverse else p_inc

    row = lax.broadcasted_iota(jnp.int32, (chunk, chunk), 0)
    col = lax.broadcasted_iota(jnp.int32, (chunk, chunk), 1)
    row_d = lax.broadcasted_iota(jnp.int32, q.shape, 0)
    att = jnp.where(row == col, jnp.sum(q * k, axis=-1, keepdims=True), 0.0)
    n = chunk
    while n >= 2:
        shift = n.bit_length() - 1
        upper = (row_d & (n - 1)) >= n // 2
        anchor = _block_anchor(p_inc, n)
        e = jnp.exp(jnp.where(upper, px - anchor, anchor - px))
        q_role = jnp.logical_not(upper) if reverse else upper
        qt = jnp.where(q_role, q * e, 0.0).astype(MXU_DTYPE)
        kt = jnp.where(q_role, 0.0, k * e).astype(MXU_DTYPE)
        s_lvl = _dot_nt(qt, kt)
        if n < chunk:
            s_lvl = jnp.where((row >> shift) == (col >> shift), s_lvl, 0.0)
        att = att + s_lvl
        n //= 2

    if reverse:
        qd = q * jnp.exp(p_tot - px)
        kd = k * jnp.exp(px)
    else:
        qd = q * jnp.exp(px)
        kd = k * jnp.exp(p_tot - px)
    state = st_ref[...]
    o = _dot(att.astype(MXU_DTYPE), v) + _dot_nt(qd.astype(MXU_DTYPE), state.astype(MXU_DTYPE))
    o_ref[...] = o.astype(o_ref.dtype)
    st_ref[...] = state * jnp.exp(p_tot) + _dot_tn(v, kd.astype(MXU_DTYPE))


def _gla_scan(proj, ld, t_lat, reverse):
    b, s, _ = proj.shape
    ch = GLA_CHUNK
    n_chunks = s // ch
    lat_chunks = t_lat // ch
    if reverse:
        def chunk_of(c):
            return n_chunks - 1 - c
    else:
        def chunk_of(c):
            return (c + lat_chunks) % n_chunks
    k0 = GLA_KEY_DIM // GLA_DK
    v0 = 2 * GLA_KEY_DIM // GLA_DV
    d0 = GLA_HEADS if reverse else 0
    return pl.pallas_call(
        functools.partial(_gla_scan_kernel, reverse=reverse),
        grid=(b, GLA_HEADS, n_chunks),
        in_specs=[
            pl.BlockSpec((None, ch, GLA_DK), lambda bi, h, c: (bi, chunk_of(c), h)),
            pl.BlockSpec((None, ch, GLA_DK), lambda bi, h, c: (bi, chunk_of(c), k0 + h)),
            pl.BlockSpec((None, ch, GLA_DV), lambda bi, h, c: (bi, chunk_of(c), v0 + h)),
            pl.BlockSpec((None, ch, GLA_DK), lambda bi, h, c: (bi, chunk_of(c), d0 + h)),
        ],
        out_specs=pl.BlockSpec((None, ch, GLA_DV), lambda bi, h, c: (bi, chunk_of(c), h)),
        out_shape=jax.ShapeDtypeStruct((b, s, GLA_VAL_DIM), F32),
        scratch_shapes=[pltpu.VMEM((GLA_DV, GLA_DK), F32)],
        compiler_params=_cparams(("parallel", "parallel", "arbitrary"), 32),
        name="gla_scan_bwd" if reverse else "gla_scan_fwd",
    )(proj, proj, proj, ld)


def _gla_out_kernel(of_ref, ob_ref, g_ref, h_ref, mod_ref, hn_ref, w_ref, lng_ref, lnb_ref, out_ref, *, alpha):
    o = of_ref[...] + ob_ref[...]
    gate = _silu(g_ref[...].astype(F32))
    y = jnp.zeros(h_ref.shape, F32)
    for hd in range(GLA_HEADS):
        cols = slice(hd * GLA_DV, (hd + 1) * GLA_DV)
        oh = o[:, cols]
        oh = oh * lax.rsqrt(jnp.mean(oh * oh, axis=-1, keepdims=True) + EPS) * hn_ref[...]
        y = y + _dot((oh * gate[:, cols]).astype(MXU_DTYPE), w_ref[cols, :])
    out_ref[...] = _residual_layer_norm(h_ref[...], y, mod_ref[2:3, :], lng_ref[...], lnb_ref[...], alpha)


def _gla_out(o_f, o_b, proj, h, mods, layer, head_norm, w_out, ln_g, ln_b, t_lat, n_rows, alpha):
    b, s, d = h.shape
    tm = ROW_TILE
    row = _mod_row(t_lat, tm)
    g_blk = (2 * GLA_KEY_DIM + GLA_VAL_DIM) // GLA_VAL_DIM
    return pl.pallas_call(
        functools.partial(_gla_out_kernel, alpha=alpha),
        grid=(b, n_rows // tm),
        in_specs=[
            pl.BlockSpec((None, tm, GLA_VAL_DIM), lambda bi, i: (bi, i, 0)),
            pl.BlockSpec((None, tm, GLA_VAL_DIM), lambda bi, i: (bi, i, 0)),
            pl.BlockSpec((None, tm, GLA_VAL_DIM), lambda bi, i: (bi, i, g_blk)),
            pl.BlockSpec((None, tm, d), lambda bi, i: (bi, i, 0)),
            pl.BlockSpec((None, None, 6, d), lambda bi, i: (layer, row(bi, i), 0, 0)),
            pl.BlockSpec((1, GLA_DV), lambda bi, i: (0, 0)),
            pl.BlockSpec((GLA_VAL_DIM, d), lambda bi, i: (0, 0)),
            pl.BlockSpec((1, d), lambda bi, i: (0, 0)),
            pl.BlockSpec((1, d), lambda bi, i: (0, 0)),
        ],
        out_specs=pl.BlockSpec((None, tm, d), lambda bi, i: (bi, i, 0)),
        out_shape=jax.ShapeDtypeStruct((b, n_rows, d), F32),
        compiler_params=_cparams(("parallel", "parallel"), 40),
        name="gla_out_norm",
    )(o_f, o_b, proj, h, mods, head_norm.reshape(1, GLA_DV), w_out, ln_g.reshape(1, d), ln_b.reshape(1, d))


def _ffn_kernel(h_ref, hp_ref, hn_ref, mod_ref, wup_ref, cw_ref, cb_ref, wdn_ref, lng_ref, lnb_ref,
                out_ref, a_scr, ug_scr, uv_scr, *, t_lat, s_tot, alpha):
    tm = h_ref.shape[0]
    d_ff = wdn_ref.shape[0]
    row0 = pl.program_id(1) * tm
    at_start = (row0 == 0) | (row0 == t_lat)
    at_end = (row0 + tm == t_lat) | (row0 + tm == s_tot)
    shift = mod_ref[3:4, :]
    scale = 1.0 + mod_ref[4:5, :]
    h = h_ref[...]
    a_scr[0:HALO, :] = jnp.where(at_start, 0.0, hp_ref[...] * scale + shift).astype(a_scr.dtype)
    a_scr[HALO:HALO + tm, :] = (h * scale + shift).astype(a_scr.dtype)
    a_scr[HALO + tm:2 * HALO + tm, :] = jnp.where(at_end, 0.0, hn_ref[...] * scale + shift).astype(a_scr.dtype)
    a = a_scr[...]

    def conv(u_scr, c0):
        cols = slice(c0, c0 + FF_CHUNK)
        acc = cb_ref[:, cols] + cw_ref[0:1, cols] * u_scr[pl.ds(HALO - 1, tm), :]
        for j in range(1, CONV_W):
            acc = acc + cw_ref[j:j + 1, cols] * u_scr[pl.ds(HALO - 1 + j, tm), :]
        return acc

    y = jnp.zeros(h.shape, F32)
    for c in range(d_ff // FF_CHUNK):
        g0 = c * FF_CHUNK
        v0 = d_ff + c * FF_CHUNK
        ug_scr[...] = _dot(a, wup_ref[:, g0:g0 + FF_CHUNK])
        uv_scr[...] = _dot(a, wup_ref[:, v0:v0 + FF_CHUNK])
        act = _silu(conv(ug_scr, g0)) * conv(uv_scr, v0)
        y = y + _dot(act.astype(MXU_DTYPE), wdn_ref[g0:g0 + FF_CHUNK, :])
    out_ref[...] = _residual_layer_norm(h, y, mod_ref[5:6, :], lng_ref[...], lnb_ref[...], alpha)


def _conv_ffn(h, mods, layer, w_up, conv_w, conv_b, w_down, ln_g, ln_b, t_lat, n_rows, alpha):
    b, s, d = h.shape
    tm = ROW_TILE
    d_ff = w_down.shape[0]
    row = _mod_row(t_lat, tm)
    hb = tm // HALO
    n_halo = s // HALO
    return pl.pallas_call(
        functools.partial(_ffn_kernel, t_lat=t_lat, s_tot=s, alpha=alpha),
        grid=(b, n_rows // tm),
        in_specs=[
            pl.BlockSpec((None, tm, d), lambda bi, i: (bi, i, 0)),
            pl.BlockSpec((None, HALO, d), lambda bi, i: (bi, jnp.maximum(i * hb - 1, 0), 0)),
            pl.BlockSpec((None, HALO, d), lambda bi, i: (bi, jnp.minimum((i + 1) * hb, n_halo - 1), 0)),
            pl.BlockSpec((None, None, 6, d), lambda bi, i: (layer, row(bi, i), 0, 0)),
            pl.BlockSpec((d, 2 * d_ff), lambda bi, i: (0, 0)),
            pl.BlockSpec((CONV_W, 2 * d_ff), lambda bi, i: (0, 0)),
            pl.BlockSpec((1, 2 * d_ff), lambda bi, i: (0, 0)),
            pl.BlockSpec((d_ff, d), lambda bi, i: (0, 0)),
            pl.BlockSpec((1, d), lambda bi, i: (0, 0)),
            pl.BlockSpec((1, d), lambda bi, i: (0, 0)),
        ],
        out_specs=pl.BlockSpec((None, tm, d), lambda bi, i: (bi, i, 0)),
        out_shape=jax.ShapeDtypeStruct((b, n_rows, d), F32),
        scratch_shapes=[
            pltpu.VMEM((tm + 2 * HALO, d), MXU_DTYPE),
            pltpu.VMEM((tm + 2 * HALO, FF_CHUNK), F32),
            pltpu.VMEM((tm + 2 * HALO, FF_CHUNK), F32),
        ],
        compiler_params=_cparams(("parallel", "parallel"), 56),
        name="conv_ffn_norm",
    )(h, h, h, mods, w_up, conv_w, conv_b.reshape(1, 2 * d_ff), w_down, ln_g.reshape(1, d), ln_b.reshape(1, d))


def _rope_tables(t_lat, c_len):
    rows = t_lat // GRID_W
    row = jnp.repeat(jnp.arange(rows, dtype=F32), GRID_W)
    col = jnp.tile(jnp.arange(GRID_W, dtype=F32), rows)
    inv_freq = jnp.power(ROPE_THETA, -jnp.arange(ROPE_AXIS_DIM // 2, dtype=F32) * 2.0 / ROPE_AXIS_DIM)
    ang = jnp.concatenate([row[:, None] * inv_freq, col[:, None] * inv_freq], axis=-1)
    ang = jnp.concatenate([ang, jnp.zeros((c_len, HEAD_DIM // 2), F32)], axis=0)
    cos, sin = jnp.cos(ang), jnp.sin(ang)
    return jnp.concatenate([cos, cos], axis=-1), jnp.concatenate([-sin, sin], axis=-1)


def kernel(x, c, ctx, c_ctx, ada_w, ada_b, ln_g, ln_b, ffn_w_up, ffn_conv_w, ffn_conv_b, ffn_w_down, attn_w_in, attn_q_norm, attn_k_norm, attn_sink, attn_w_out, gla_w_in, gla_w_a1, gla_w_a2, gla_b_a, gla_head_norm, gla_w_out):
    b, t_lat, d = x.shape
    c_len = ctx.shape[1]
    s = t_lat + c_len
    depth = ada_w.shape[0]
    assert b < MOD_CTX_ROW and t_lat % ROW_TILE == 0 and c_len % ROW_TILE == 0
    assert t_lat % GLA_CHUNK == 0 and c_len % GLA_CHUNK == 0 and s % MXU_N == 0
    alpha = (2 * depth) ** 0.25

    cond = jnp.zeros((MOD_ROWS, d), F32).at[:b].set(c).at[MOD_CTX_ROW].set(c_ctx)
    mods = _modulation(cond, ada_w, ada_b).reshape(depth, MOD_ROWS, 6, d)
    cos, sin = _rope_tables(t_lat, c_len)
    h = jnp.concatenate([x, ctx], axis=1)

    for i in range(depth):
        last = i == depth - 1
        n_rows = t_lat if last else s
        j = i // 2
        if i % 2 == 0:
            qkv = _attn_project(h, mods, i, attn_w_in[j].astype(MXU_DTYPE), attn_q_norm[j], attn_k_norm[j],
                                cos, sin, t_lat)
            o_a = _global_attention(qkv, t_lat)
            o_b = _window_attention(qkv, attn_sink[j], t_lat)
            h1 = _attn_out(o_a, o_b, h, mods, i, attn_w_out[j].astype(MXU_DTYPE), ln_g[i, 0], ln_b[i, 0],
                           t_lat, n_rows, alpha)
        else:
            w_a1 = jnp.concatenate([gla_w_a1[j, 0], gla_w_a1[j, 1]], axis=1)
            w_a1 = jnp.pad(w_a1, ((0, 0), (0, LANES - 2 * GLA_GATE_RANK))).astype(MXU_DTYPE)
            w_a2 = jnp.zeros((LANES, 2 * GLA_KEY_DIM), F32)
            w_a2 = w_a2.at[:GLA_GATE_RANK, :GLA_KEY_DIM].set(gla_w_a2[j, 0])
            w_a2 = w_a2.at[GLA_GATE_RANK:2 * GLA_GATE_RANK, GLA_KEY_DIM:].set(gla_w_a2[j, 1]).astype(MXU_DTYPE)
            proj, ld = _gla_project(h, mods, i, gla_w_in[j].astype(MXU_DTYPE), w_a1, w_a2,
                                    gla_b_a[j].reshape(1, 2 * GLA_KEY_DIM), t_lat)
            o_f = _gla_scan(proj, ld, t_lat, reverse=False)
            o_r = _gla_scan(proj, ld, t_lat, reverse=True)
            h1 = _gla_out(o_f, o_r, proj, h, mods, i, gla_head_norm[j], gla_w_out[j].astype(MXU_DTYPE),
                          ln_g[i, 0], ln_b[i, 0], t_lat, n_rows, alpha)
        h = _conv_ffn(h1, mods, i, ffn_w_up[i].astype(MXU_DTYPE), ffn_conv_w[i], ffn_conv_b[i],
                      ffn_w_down[i].astype(MXU_DTYPE), ln_g[i, 1], ln_b[i, 1], t_lat, n_rows, alpha)
    return h
```

```python
import functools

import jax
import jax.numpy as jnp
from jax import lax
from jax.experimental import pallas as pl
from jax.experimental.pallas import tpu as pltpu

GRID_W = 64
HEAD_DIM = 128
A_Q_HEADS = 4
A_KV_HEADS = 2
B_Q_HEADS = 4
B_KV_HEADS = 2
A_QW = A_Q_HEADS * HEAD_DIM
A_KVW = A_KV_HEADS * HEAD_DIM
B_QW = B_Q_HEADS * HEAD_DIM
B_KVW = B_KV_HEADS * HEAD_DIM
ATTN_IN = A_QW + 2 * A_KVW + B_QW + 2 * B_KVW
ATTN_SCALE = HEAD_DIM ** -0.5
WINDOW = 128
ROPE_THETA = 10000.0
ROPE_AXIS_DIM = HEAD_DIM // 2
GLA_HEADS = 4
GLA_DK = 128
GLA_DV = 256
GLA_KEY_DIM = GLA_HEADS * GLA_DK
GLA_VAL_DIM = GLA_HEADS * GLA_DV
GLA_IN = 2 * GLA_KEY_DIM + 2 * GLA_VAL_DIM
GLA_GATE_RANK = 16
GLA_GATE_NORM = 16.0
CONV_W = 3
EPS = 1e-6

LANES = 128
SUBLANES = 8
MXU_N = 256
VMEM_BYTES = 64 * 2 ** 20

MXU_DTYPE = jnp.bfloat16
F32 = jnp.float32

ROW_TILE = 256
HALO = 16
FF_CHUNK = 256
U_SLOTS = 3
ACT_SLOTS = 2
KV_CHUNK = 768
KV_SLOTS = 2
GLA_CHUNK = 128
MOD_COLS = 1536


def _cparams(sem, vmem_mib):
    return pltpu.CompilerParams(dimension_semantics=sem, vmem_limit_bytes=vmem_mib * 2 ** 20)


def _dot(a, b):
    return jnp.dot(a, b, preferred_element_type=F32)


def _dot_nt(a, b):
    return lax.dot_general(a, b, (((1,), (1,)), ((), ())), preferred_element_type=F32)


def _dot_tn(a, b):
    return lax.dot_general(a, b, (((0,), (0,)), ((), ())), preferred_element_type=F32)


def _silu(x):
    return x / (1.0 + jnp.exp(-x))


def _residual_layer_norm(h, y, gate, ln_g, ln_b, alpha):
    r = alpha * h + gate * y
    mu = jnp.mean(r, axis=-1, keepdims=True)
    d = r - mu
    var = jnp.mean(d * d, axis=-1, keepdims=True)
    return d * lax.rsqrt(var + EPS) * ln_g + ln_b


def _mod_row(t_lat, tm):
    n_lat = t_lat // tm

    def row(b, i):
        return jnp.where(i < n_lat, b, MOD_CTX_ROW)
    return row


MOD_ROWS = 8
MOD_CTX_ROW = MOD_ROWS - 1


def _mod_kernel(cond_ref, w_ref, b_ref, o_ref):
    a = _silu(cond_ref[...]).astype(MXU_DTYPE)
    o_ref[...] = _dot(a, w_ref[...].astype(MXU_DTYPE)) + b_ref[...]


def _modulation(cond, ada_w, ada_b):
    depth, d, n = ada_w.shape
    return pl.pallas_call(
        _mod_kernel,
        grid=(depth, n // MOD_COLS),
        in_specs=[
            pl.BlockSpec((MOD_ROWS, d), lambda l, j: (0, 0)),
            pl.BlockSpec((None, d, MOD_COLS), lambda l, j: (l, 0, j)),
            pl.BlockSpec((None, 1, MOD_COLS), lambda l, j: (l, 0, j)),
        ],
        out_specs=pl.BlockSpec((None, MOD_ROWS, MOD_COLS), lambda l, j: (l, 0, j)),
        out_shape=jax.ShapeDtypeStruct((depth, MOD_ROWS, n), F32),
        compiler_params=_cparams(("parallel", "parallel"), 32),
        name="ada_modulation",
    )(cond, ada_w, ada_b.reshape(depth, 1, n))


def _attn_proj_kernel(h_ref, mod_ref, w_ref, qn_ref, kn_ref, cos_ref, sin_ref, o_ref):
    a = (h_ref[...] * (1.0 + mod_ref[1:2, :]) + mod_ref[0:1, :]).astype(MXU_DTYPE)
    cos = cos_ref[...]
    sin = sin_ref[...]

    def rms(z, g):
        return z * lax.rsqrt(jnp.mean(z * z, axis=-1, keepdims=True) + EPS) * g

    def rope(z):
        return z * cos + pltpu.roll(z, HEAD_DIM // 2, 1) * sin

    a_k0 = A_QW // HEAD_DIM
    a_v0 = a_k0 + A_KV_HEADS
    b_q0 = a_v0 + A_KV_HEADS
    b_k0 = b_q0 + B_Q_HEADS
    b_v0 = b_k0 + B_KV_HEADS
    heads_per_dot = MXU_N // HEAD_DIM
    for cb in range(ATTN_IN // MXU_N):
        z2 = _dot(a, w_ref[:, cb * MXU_N:(cb + 1) * MXU_N])
        for half in range(heads_per_dot):
            hb = cb * heads_per_dot + half
            z = z2[:, half * HEAD_DIM:(half + 1) * HEAD_DIM]
            if hb < a_k0:
                z = rope(rms(z, qn_ref[...])) * ATTN_SCALE
            elif hb < a_v0:
                z = rope(rms(z, kn_ref[...]))
            elif hb < b_q0:
                pass
            elif hb < b_k0:
                z = rope(z) * ATTN_SCALE
            elif hb < b_v0:
                z = rope(z)
            o_ref[:, hb * HEAD_DIM:(hb + 1) * HEAD_DIM] = z.astype(o_ref.dtype)


def _attn_project(h, mods, layer, w_in, q_norm, k_norm, cos, sin, t_lat):
    b, s, d = h.shape
    tm = ROW_TILE
    row = _mod_row(t_lat, tm)
    return pl.pallas_call(
        _attn_proj_kernel,
        grid=(b, s // tm),
        in_specs=[
            pl.BlockSpec((None, tm, d), lambda bi, i: (bi, i, 0)),
            pl.BlockSpec((None, None, 6, d), lambda bi, i: (layer, row(bi, i), 0, 0)),
            pl.BlockSpec((d, ATTN_IN), lambda bi, i: (0, 0)),
            pl.BlockSpec((1, HEAD_DIM), lambda bi, i: (0, 0)),
            pl.BlockSpec((1, HEAD_DIM), lambda bi, i: (0, 0)),
            pl.BlockSpec((tm, HEAD_DIM), lambda bi, i: (i, 0)),
            pl.BlockSpec((tm, HEAD_DIM), lambda bi, i: (i, 0)),
        ],
        out_specs=pl.BlockSpec((None, tm, ATTN_IN), lambda bi, i: (bi, i, 0)),
        out_shape=jax.ShapeDtypeStruct((b, s, ATTN_IN), MXU_DTYPE),
        compiler_params=_cparams(("parallel", "parallel"), 40),
        name="attn_project",
    )(h, mods, w_in, q_norm.reshape(1, HEAD_DIM), k_norm.reshape(1, HEAD_DIM), cos, sin)


def _stack_heads(q):
    g = q.shape[1] // HEAD_DIM
    return jnp.concatenate([q[:, i * HEAD_DIM:(i + 1) * HEAD_DIM] for i in range(g)], axis=0)


def _store_heads(o_ref, o, tq):
    for i in range(o.shape[0] // tq):
        o_ref[:, i * HEAD_DIM:(i + 1) * HEAD_DIM] = o[i * tq:(i + 1) * tq, :].astype(o_ref.dtype)


def _softmax_chunk(sc, m_scr, l_scr, alpha_ref, p_ref):
    blocks = [sc[:, j * LANES:(j + 1) * LANES] for j in range(sc.shape[1] // LANES)]
    mx = blocks[0]
    for blk in blocks[1:]:
        mx = jnp.maximum(mx, blk)
    m_prev = m_scr[...]
    m_new = jnp.maximum(m_prev, jnp.max(mx, axis=-1, keepdims=True))
    alpha = jnp.exp(m_prev - m_new)
    l_new = alpha * l_scr[...]
    for j, blk in enumerate(blocks):
        p = jnp.exp(blk - m_new)
        l_new = l_new + p
        p_ref[:, j * LANES:(j + 1) * LANES] = p.astype(p_ref.dtype)
    alpha_ref[...] = alpha
    l_scr[...] = l_new
    m_scr[...] = m_new


def _global_attn_kernel(q_ref, k_ref, v_ref, o_ref, m_scr, l_scr, acc_scr, s_scr, p_scr, alpha_scr,
                        *, t_lat, tq, kv_chunk):
    s_tot = k_ref.shape[0]
    c_len = s_tot - t_lat
    is_ctx = pl.program_id(2) * tq >= t_lat
    q2 = _stack_heads(q_ref[...])
    m_scr[...] = jnp.full(m_scr.shape, -jnp.inf, F32)
    l_scr[...] = jnp.zeros(l_scr.shape, F32)
    acc_scr[...] = jnp.zeros(acc_scr.shape, F32)

    def keys(c):
        return slice(c * kv_chunk, (c + 1) * kv_chunk)

    def accumulate(c):
        pv = _dot(p_scr[c % KV_SLOTS], v_ref[keys(c), :])
        acc_scr[...] = alpha_scr[c % KV_SLOTS] * acc_scr[...] + pv

    @pl.when(jnp.logical_not(is_ctx))
    def _():
        n_chunks = s_tot // kv_chunk
        s_scr[0] = _dot_nt(q2, k_ref[keys(0), :])
        for c in range(n_chunks):
            if c + 1 < n_chunks:
                s_scr[(c + 1) % KV_SLOTS] = _dot_nt(q2, k_ref[keys(c + 1), :])
            if c > 0:
                accumulate(c - 1)
            _softmax_chunk(s_scr[c % KV_SLOTS], m_scr, l_scr, alpha_scr.at[c % KV_SLOTS], p_scr.at[c % KV_SLOTS])
        accumulate(n_chunks - 1)

    @pl.when(is_ctx)
    def _():
        sc = _dot_nt(q2, k_ref[t_lat:s_tot, :])
        _softmax_chunk(sc, m_scr, l_scr, alpha_scr.at[0], p_scr.at[0, :, 0:c_len])
        acc_scr[...] = _dot(p_scr[0, :, 0:c_len], v_ref[t_lat:s_tot, :])

    denom = jnp.sum(l_scr[...], axis=-1, keepdims=True)
    _store_heads(o_ref, acc_scr[...] / denom, tq)


def _kv_chunk(s_tot):
    return max(n for n in range(MXU_N, KV_CHUNK + 1, MXU_N) if s_tot % n == 0)


def _global_attention(qkv, t_lat):
    b, s, _ = qkv.shape
    tq = ROW_TILE
    group = A_Q_HEADS // A_KV_HEADS
    k0 = A_QW // HEAD_DIM
    v0 = k0 + A_KV_HEADS
    kv_chunk = _kv_chunk(s)
    assert s - t_lat <= kv_chunk
    return pl.pallas_call(
        functools.partial(_global_attn_kernel, t_lat=t_lat, tq=tq, kv_chunk=kv_chunk),
        grid=(b, A_KV_HEADS, s // tq),
        in_specs=[
            pl.BlockSpec((None, tq, group * HEAD_DIM), lambda bi, hk, i: (bi, i, hk)),
            pl.BlockSpec((None, s, HEAD_DIM), lambda bi, hk, i: (bi, 0, k0 + hk)),
            pl.BlockSpec((None, s, HEAD_DIM), lambda bi, hk, i: (bi, 0, v0 + hk)),
        ],
        out_specs=pl.BlockSpec((None, tq, group * HEAD_DIM), lambda bi, hk, i: (bi, i, hk)),
        out_shape=jax.ShapeDtypeStruct((b, s, A_QW), MXU_DTYPE),
        scratch_shapes=[
            pltpu.VMEM((group * tq, LANES), F32),
            pltpu.VMEM((group * tq, LANES), F32),
            pltpu.VMEM((group * tq, HEAD_DIM), F32),
            pltpu.VMEM((KV_SLOTS, group * tq, kv_chunk), F32),
            pltpu.VMEM((KV_SLOTS, group * tq, kv_chunk), MXU_DTYPE),
            pltpu.VMEM((KV_SLOTS, group * tq, LANES), F32),
        ],
        compiler_params=_cparams(("parallel", "parallel", "arbitrary"), 40),
        name="global_attention",
    )(qkv, qkv, qkv)


def _window_attn_kernel(sink_ref, q_ref, k_ref, v_ref, o_ref, *, t_lat, tq):
    s_tot = k_ref.shape[0]
    c_len = s_tot - t_lat
    band = tq + 2 * WINDOW
    hk = pl.program_id(1)
    q0 = pl.program_id(2) * tq
    is_ctx = q0 >= t_lat
    start = pl.multiple_of(jnp.clip(q0 - WINDOW, 0, t_lat - band), WINDOW)
    q2 = _stack_heads(q_ref[...])
    rows = q2.shape[0]
    group = rows // tq
    kb = k_ref[pl.ds(start, band), :]
    vb = v_ref[pl.ds(start, band), :]
    kc = k_ref[pl.ds(t_lat, c_len), :]
    vc = v_ref[pl.ds(t_lat, c_len), :]

    delta = (lax.broadcasted_iota(jnp.int32, (tq, band), 1) + (start - q0)
             - lax.broadcasted_iota(jnp.int32, (tq, band), 0))
    reach = jnp.where(is_ctx, -1, WINDOW)
    valid = jnp.abs(delta) <= reach
    s_band = _dot_nt(q2, kb)
    s_ctx = _dot_nt(q2, kc)
    p_band, p_ctx, denom = [], [], []
    for g in range(group):
        head = slice(g * tq, (g + 1) * tq)
        sb = jnp.where(valid, s_band[head, :], -jnp.inf)
        sc = s_ctx[head, :]
        sink = sink_ref[hk * group + g]
        m = jnp.maximum(jnp.maximum(jnp.max(sb, axis=-1, keepdims=True),
                                    jnp.max(sc, axis=-1, keepdims=True)), sink)
        pb = jnp.exp(sb - m)
        pc = jnp.exp(sc - m)
        denom.append(jnp.sum(pb, axis=-1, keepdims=True) + jnp.sum(pc, axis=-1, keepdims=True)
                     + jnp.exp(sink - m))
        p_band.append(pb.astype(MXU_DTYPE))
        p_ctx.append(pc.astype(MXU_DTYPE))
    o = _dot(jnp.concatenate(p_band, axis=0), vb) + _dot(jnp.concatenate(p_ctx, axis=0), vc)
    _store_heads(o_ref, o / jnp.concatenate(denom, axis=0), tq)


def _window_attention(qkv, sink, t_lat):
    b, s, _ = qkv.shape
    tq = ROW_TILE
    assert tq & (tq - 1) == 0 and t_lat >= tq + 2 * WINDOW
    group = B_Q_HEADS // B_KV_HEADS
    q0 = (A_QW + 2 * A_KVW) // (group * HEAD_DIM)
    k0 = (A_QW + 2 * A_KVW + B_QW) // HEAD_DIM
    v0 = k0 + B_KV_HEADS
    return pl.pallas_call(
        functools.partial(_window_attn_kernel, t_lat=t_lat, tq=tq),
        grid=(b, B_KV_HEADS, s // tq),
        in_specs=[
            pl.BlockSpec(memory_space=pltpu.SMEM),
            pl.BlockSpec((None, tq, group * HEAD_DIM), lambda bi, hk, i: (bi, i, q0 + hk)),
            pl.BlockSpec((None, s, HEAD_DIM), lambda bi, hk, i: (bi, 0, k0 + hk)),
            pl.BlockSpec((None, s, HEAD_DIM), lambda bi, hk, i: (bi, 0, v0 + hk)),
        ],
        out_specs=pl.BlockSpec((None, tq, group * HEAD_DIM), lambda bi, hk, i: (bi, i, hk)),
        out_shape=jax.ShapeDtypeStruct((b, s, B_QW), MXU_DTYPE),
        compiler_params=_cparams(("parallel", "parallel", "arbitrary"), 40),
        name="window_attention",
    )(sink, qkv, qkv, qkv)


def _attn_out_kernel(oa_ref, ob_ref, h_ref, mod_ref, w_ref, lng_ref, lnb_ref, out_ref, *, alpha):
    y = _dot(oa_ref[...], w_ref[0:A_QW, :]) + _dot(ob_ref[...], w_ref[A_QW:A_QW + B_QW, :])
    out_ref[...] = _residual_layer_norm(h_ref[...], y, mod_ref[2:3, :], lng_ref[...], lnb_ref[...], alpha)


def _attn_out(o_a, o_b, h, mods, layer, w_out, ln_g, ln_b, t_lat, n_rows, alpha):
    b, s, d = h.shape
    tm = ROW_TILE
    row = _mod_row(t_lat, tm)
    return pl.pallas_call(
        functools.partial(_attn_out_kernel, alpha=alpha),
        grid=(b, n_rows // tm),
        in_specs=[
            pl.BlockSpec((None, tm, A_QW), lambda bi, i: (bi, i, 0)),
            pl.BlockSpec((None, tm, B_QW), lambda bi, i: (bi, i, 0)),
            pl.BlockSpec((None, tm, d), lambda bi, i: (bi, i, 0)),
            pl.BlockSpec((None, None, 6, d), lambda bi, i: (layer, row(bi, i), 0, 0)),
            pl.BlockSpec((A_QW + B_QW, d), lambda bi, i: (0, 0)),
            pl.BlockSpec((1, d), lambda bi, i: (0, 0)),
            pl.BlockSpec((1, d), lambda bi, i: (0, 0)),
        ],
        out_specs=pl.BlockSpec((None, tm, d), lambda bi, i: (bi, i, 0)),
        out_shape=jax.ShapeDtypeStruct((b, n_rows, d), F32),
        compiler_params=_cparams(("parallel", "parallel"), 40),
        name="attn_out_norm",
    )(o_a, o_b, h, mods, w_out, ln_g.reshape(1, d), ln_b.reshape(1, d))


def _log_sigmoid(x):
    return jnp.minimum(x, 0.0) - jnp.log(1.0 + jnp.exp(-jnp.abs(x)))


def _gla_proj_kernel(h_ref, mod_ref, w_ref, wa1_ref, wa2_ref, ba_ref, p_ref, ld_ref):
    a = (h_ref[...] * (1.0 + mod_ref[1:2, :]) + mod_ref[0:1, :]).astype(MXU_DTYPE)
    for cb in range(GLA_IN // MXU_N):
        z = _dot(a, w_ref[:, cb * MXU_N:(cb + 1) * MXU_N])
        if cb < GLA_KEY_DIM // MXU_N:
            z = z * (GLA_DK ** -0.5)
        p_ref[:, cb * MXU_N:(cb + 1) * MXU_N] = z.astype(p_ref.dtype)
    low = _dot(a, wa1_ref[...]).astype(MXU_DTYPE)
    for cb in range(2 * GLA_KEY_DIM // MXU_N):
        cols = slice(cb * MXU_N, (cb + 1) * MXU_N)
        logits = _dot(low, wa2_ref[:, cols]) + ba_ref[:, cols]
        ld_ref[:, cols] = _log_sigmoid(logits) / GLA_GATE_NORM


def _gla_project(h, mods, layer, w_in, w_a1, w_a2, b_a, t_lat):
    b, s, d = h.shape
    tm = ROW_TILE
    row = _mod_row(t_lat, tm)
    return pl.pallas_call(
        _gla_proj_kernel,
        grid=(b, s // tm),
        in_specs=[
            pl.BlockSpec((None, tm, d), lambda bi, i: (bi, i, 0)),
            pl.BlockSpec((None, None, 6, d), lambda bi, i: (layer, row(bi, i), 0, 0)),
            pl.BlockSpec((d, GLA_IN), lambda bi, i: (0, 0)),
            pl.BlockSpec((d, LANES), lambda bi, i: (0, 0)),
            pl.BlockSpec((LANES, 2 * GLA_KEY_DIM), lambda bi, i: (0, 0)),
            pl.BlockSpec((1, 2 * GLA_KEY_DIM), lambda bi, i: (0, 0)),
        ],
        out_specs=[
            pl.BlockSpec((None, tm, GLA_IN), lambda bi, i: (bi, i, 0)),
            pl.BlockSpec((None, tm, 2 * GLA_KEY_DIM), lambda bi, i: (bi, i, 0)),
        ],
        out_shape=[
            jax.ShapeDtypeStruct((b, s, GLA_IN), MXU_DTYPE),
            jax.ShapeDtypeStruct((b, s, 2 * GLA_KEY_DIM), F32),
        ],
        compiler_params=_cparams(("parallel", "parallel"), 48),
        name="gla_project",
    )(h, mods, w_in, w_a1, w_a2, b_a)


def _cumsum_rows(x):
    rows, cols = x.shape
    sub = lax.broadcasted_iota(jnp.int32, x.shape, 0) & (SUBLANES - 1)
    shift = 1
    while shift < SUBLANES:
        x = x + jnp.where(sub >= shift, pltpu.roll(x, shift, 0), 0.0)
        shift *= 2
    x3 = x.reshape(rows // SUBLANES, SUBLANES, cols)
    carry = jnp.zeros((1, cols), F32)
    tiles = []
    for i in range(rows // SUBLANES):
        tile = x3[i] + carry
        tiles.append(tile)
        carry = tile[SUBLANES - 1:SUBLANES, :]
    return jnp.concatenate(tiles, axis=0)


def _block_anchor(p, n):
    rows, cols = p.shape
    half = n // 2
    if n >= SUBLANES:
        p3 = p.reshape(rows // n, n, cols)
        return jnp.broadcast_to(p3[:, half - 1:half, :], p3.shape).reshape(rows, cols)
    if n == 2:
        odd = (lax.broadcasted_iota(jnp.int32, p.shape, 0) & 1) == 1
        return jnp.where(odd, pltpu.roll(p, 1, 0), p)
    p3 = p.reshape(rows // SUBLANES, SUBLANES, cols)
    sub = lax.broadcasted_iota(jnp.int32, p3.shape, 1)
    a = None
    for b0 in range(0, SUBLANES, n):
        row = jnp.broadcast_to(p3[:, b0 + half - 1:b0 + half, :], p3.shape)
        a = row if a is None else jnp.where(sub >= b0, row, a)
    return a.reshape(rows, cols)


def _gla_chunk(q, k, v, ld, st_ref, reverse):
    chunk = q.shape[0]
    q = q.astype(F32)
    k = k.astype(F32)
    p_inc = _cumsum_rows(ld)
    p_tot = p_inc[chunk - 1:chunk, :]
    px = p_inc - ld if reverse else p_inc

    row = lax.broadcasted_iota(jnp.int32, (chunk, chunk), 0)
    col = lax.broadcasted_iota(jnp.int32, (chunk, chunk), 1)
    differ = row ^ col
    row_d = lax.broadcasted_iota(jnp.int32, q.shape, 0)
    att = jnp.broadcast_to(jnp.sum(q * k, axis=-1, keepdims=True), (chunk, chunk))
    n = 2
    while n <= chunk:
        upper = (row_d & (n - 1)) >= n // 2
        anchor = _block_anchor(p_inc, n)
        e = jnp.exp(jnp.where(upper, px - anchor, anchor - px))
        s_lvl = _dot_nt((q * e).astype(MXU_DTYPE), (k * e).astype(MXU_DTYPE))
        att = jnp.where((differ >> (n.bit_length() - 2)) == 1, s_lvl, att)
        n *= 2
    att = jnp.where(row <= col if reverse else row >= col, att, 0.0)

    if reverse:
        qd = q * jnp.exp(p_tot - px)
        kd = k * jnp.exp(px)
    else:
        qd = q * jnp.exp(px)
        kd = k * jnp.exp(p_tot - px)
    state = st_ref[...]
    o = _dot(att.astype(MXU_DTYPE), v) + _dot_nt(qd.astype(MXU_DTYPE), state.astype(MXU_DTYPE))
    st_ref[...] = state * jnp.exp(p_tot) + _dot_tn(v, kd.astype(MXU_DTYPE))
    return o


def _gla_scan_kernel(qf_ref, kf_ref, vf_ref, ldf_ref, qr_ref, kr_ref, vr_ref, ldr_ref,
                     of_ref, or_ref, st_ref):
    @pl.when(pl.program_id(1) == 0)
    def _():
        st_ref[...] = jnp.zeros(st_ref.shape, F32)

    operands = ((qf_ref, kf_ref, vf_ref, ldf_ref, of_ref), (qr_ref, kr_ref, vr_ref, ldr_ref, or_ref))
    for direction, (q_ref, k_ref, v_ref, ld_ref, o_ref) in enumerate(operands):
        for hd in range(GLA_HEADS):
            kc = slice(hd * GLA_DK, (hd + 1) * GLA_DK)
            vc = slice(hd * GLA_DV, (hd + 1) * GLA_DV)
            o = _gla_chunk(q_ref[:, kc], k_ref[:, kc], v_ref[:, vc], ld_ref[:, kc],
                           st_ref.at[direction, hd], reverse=direction == 1)
            o_ref[:, vc] = o.astype(o_ref.dtype)


def _gla_scan(proj, ld, t_lat):
    b, s, _ = proj.shape
    ch = GLA_CHUNK
    n_chunks = s // ch
    lat_chunks = t_lat // ch

    def fwd(c):
        return (c + lat_chunks) % n_chunks

    def rev(c):
        return n_chunks - 1 - c

    def specs(chunk_of, direction):
        return [
            pl.BlockSpec((None, ch, GLA_KEY_DIM), lambda bi, c: (bi, chunk_of(c), 0)),
            pl.BlockSpec((None, ch, GLA_KEY_DIM), lambda bi, c: (bi, chunk_of(c), 1)),
            pl.BlockSpec((None, ch, GLA_VAL_DIM), lambda bi, c: (bi, chunk_of(c), 1)),
            pl.BlockSpec((None, ch, GLA_KEY_DIM), lambda bi, c: (bi, chunk_of(c), direction)),
        ]

    o_shape = jax.ShapeDtypeStruct((b, s, GLA_VAL_DIM), F32)
    return pl.pallas_call(
        _gla_scan_kernel,
        grid=(b, n_chunks),
        in_specs=specs(fwd, 0) + specs(rev, 1),
        out_specs=[
            pl.BlockSpec((None, ch, GLA_VAL_DIM), lambda bi, c: (bi, fwd(c), 0)),
            pl.BlockSpec((None, ch, GLA_VAL_DIM), lambda bi, c: (bi, rev(c), 0)),
        ],
        out_shape=[o_shape, o_shape],
        scratch_shapes=[pltpu.VMEM((2, GLA_HEADS, GLA_DV, GLA_DK), F32)],
        compiler_params=_cparams(("parallel", "arbitrary"), 32),
        name="gla_scan",
    )(proj, proj, proj, ld, proj, proj, proj, ld)


def _gla_out_kernel(of_ref, ob_ref, g_ref, h_ref, mod_ref, hn_ref, w_ref, lng_ref, lnb_ref, out_ref, *, alpha):
    o = of_ref[...] + ob_ref[...]
    gate = _silu(g_ref[...].astype(F32))
    y = jnp.zeros(h_ref.shape, F32)
    for hd in range(GLA_HEADS):
        cols = slice(hd * GLA_DV, (hd + 1) * GLA_DV)
        oh = o[:, cols]
        oh = oh * lax.rsqrt(jnp.mean(oh * oh, axis=-1, keepdims=True) + EPS) * hn_ref[...]
        y = y + _dot((oh * gate[:, cols]).astype(MXU_DTYPE), w_ref[cols, :])
    out_ref[...] = _residual_layer_norm(h_ref[...], y, mod_ref[2:3, :], lng_ref[...], lnb_ref[...], alpha)


def _gla_out(o_f, o_b, proj, h, mods, layer, head_norm, w_out, ln_g, ln_b, t_lat, n_rows, alpha):
    b, s, d = h.shape
    tm = ROW_TILE
    row = _mod_row(t_lat, tm)
    g_blk = (2 * GLA_KEY_DIM + GLA_VAL_DIM) // GLA_VAL_DIM
    return pl.pallas_call(
        functools.partial(_gla_out_kernel, alpha=alpha),
        grid=(b, n_rows // tm),
        in_specs=[
            pl.BlockSpec((None, tm, GLA_VAL_DIM), lambda bi, i: (bi, i, 0)),
            pl.BlockSpec((None, tm, GLA_VAL_DIM), lambda bi, i: (bi, i, 0)),
            pl.BlockSpec((None, tm, GLA_VAL_DIM), lambda bi, i: (bi, i, g_blk)),
            pl.BlockSpec((None, tm, d), lambda bi, i: (bi, i, 0)),
            pl.BlockSpec((None, None, 6, d), lambda bi, i: (layer, row(bi, i), 0, 0)),
            pl.BlockSpec((1, GLA_DV), lambda bi, i: (0, 0)),
            pl.BlockSpec((GLA_VAL_DIM, d), lambda bi, i: (0, 0)),
            pl.BlockSpec((1, d), lambda bi, i: (0, 0)),
            pl.BlockSpec((1, d), lambda bi, i: (0, 0)),
        ],
        out_specs=pl.BlockSpec((None, tm, d), lambda bi, i: (bi, i, 0)),
        out_shape=jax.ShapeDtypeStruct((b, n_rows, d), F32),
        compiler_params=_cparams(("parallel", "parallel"), 40),
        name="gla_out_norm",
    )(o_f, o_b, proj, h, mods, head_norm.reshape(1, GLA_DV), w_out, ln_g.reshape(1, d), ln_b.reshape(1, d))


def _ffn_kernel(h_ref, hp_ref, hn_ref, mod_ref, wup_ref, cw_ref, cb_ref, wdn_ref, lng_ref, lnb_ref,
                out_ref, a_scr, u_scr, act_scr, *, t_lat, s_tot, alpha):
    tm = h_ref.shape[0]
    d_ff = wdn_ref.shape[0]
    row0 = pl.program_id(1) * tm
    at_start = (row0 == 0) | (row0 == t_lat)
    at_end = (row0 + tm == t_lat) | (row0 + tm == s_tot)
    shift = mod_ref[3:4, :]
    scale = 1.0 + mod_ref[4:5, :]
    h = h_ref[...]
    a_scr[0:HALO, :] = jnp.where(at_start, 0.0, hp_ref[...] * scale + shift).astype(a_scr.dtype)
    a_scr[HALO:HALO + tm, :] = (h * scale + shift).astype(a_scr.dtype)
    a_scr[HALO + tm:2 * HALO + tm, :] = jnp.where(at_end, 0.0, hn_ref[...] * scale + shift).astype(a_scr.dtype)
    a = a_scr[...]

    def conv(u_ref, c0):
        cols = slice(c0, c0 + FF_CHUNK)
        acc = cb_ref[:, cols] + cw_ref[0:1, cols] * u_ref[pl.ds(HALO - 1, tm), :]
        for j in range(1, CONV_W):
            acc = acc + cw_ref[j:j + 1, cols] * u_ref[pl.ds(HALO - 1 + j, tm), :]
        return acc

    def up_project(c):
        g0 = c * FF_CHUNK
        v0 = d_ff + c * FF_CHUNK
        u_scr[c % U_SLOTS, 0] = _dot(a, wup_ref[:, g0:g0 + FF_CHUNK])
        u_scr[c % U_SLOTS, 1] = _dot(a, wup_ref[:, v0:v0 + FF_CHUNK])

    def down_project(c):
        return _dot(act_scr[c % ACT_SLOTS], wdn_ref[c * FF_CHUNK:(c + 1) * FF_CHUNK, :])

    n_chunks = d_ff // FF_CHUNK
    ahead = U_SLOTS - 1
    y = jnp.zeros(h.shape, F32)
    for c in range(min(ahead, n_chunks)):
        up_project(c)
    for c in range(n_chunks):
        if c + ahead < n_chunks:
            up_project(c + ahead)
        if c > 0:
            y = y + down_project(c - 1)
        g0 = c * FF_CHUNK
        act = _silu(conv(u_scr.at[c % U_SLOTS, 0], g0)) * conv(u_scr.at[c % U_SLOTS, 1], d_ff + g0)
        act_scr[c % ACT_SLOTS] = act.astype(act_scr.dtype)
    y = y + down_project(n_chunks - 1)
    out_ref[...] = _residual_layer_norm(h, y, mod_ref[5:6, :], lng_ref[...], lnb_ref[...], alpha)


def _conv_ffn(h, mods, layer, w_up, conv_w, conv_b, w_down, ln_g, ln_b, t_lat, n_rows, alpha):
    b, s, d = h.shape
    tm = ROW_TILE
    d_ff = w_down.shape[0]
    row = _mod_row(t_lat, tm)
    hb = tm // HALO
    n_halo = s // HALO
    return pl.pallas_call(
        functools.partial(_ffn_kernel, t_lat=t_lat, s_tot=s, alpha=alpha),
        grid=(b, n_rows // tm),
        in_specs=[
            pl.BlockSpec((None, tm, d), lambda bi, i: (bi, i, 0)),
            pl.BlockSpec((None, HALO, d), lambda bi, i: (bi, jnp.maximum(i * hb - 1, 0), 0)),
            pl.BlockSpec((None, HALO, d), lambda bi, i: (bi, jnp.minimum((i + 1) * hb, n_halo - 1), 0)),
            pl.BlockSpec((None, None, 6, d), lambda bi, i: (layer, row(bi, i), 0, 0)),
            pl.BlockSpec((d, 2 * d_ff), lambda bi, i: (0, 0)),
            pl.BlockSpec((CONV_W, 2 * d_ff), lambda bi, i: (0, 0)),
            pl.BlockSpec((1, 2 * d_ff), lambda bi, i: (0, 0)),
            pl.BlockSpec((d_ff, d), lambda bi, i: (0, 0)),
            pl.BlockSpec((1, d), lambda bi, i: (0, 0)),
            pl.BlockSpec((1, d), lambda bi, i: (0, 0)),
        ],
        out_specs=pl.BlockSpec((None, tm, d), lambda bi, i: (bi, i, 0)),
        out_shape=jax.ShapeDtypeStruct((b, n_rows, d), F32),
        scratch_shapes=[
            pltpu.VMEM((tm + 2 * HALO, d), MXU_DTYPE),
            pltpu.VMEM((U_SLOTS, 2, tm + 2 * HALO, FF_CHUNK), F32),
            pltpu.VMEM((ACT_SLOTS, tm, FF_CHUNK), MXU_DTYPE),
        ],
        compiler_params=_cparams(("parallel", "parallel"), 56),
        name="conv_ffn_norm",
    )(h, h, h, mods, w_up, conv_w, conv_b.reshape(1, 2 * d_ff), w_down, ln_g.reshape(1, d), ln_b.reshape(1, d))


def _rope_tables(t_lat, c_len):
    rows = t_lat // GRID_W
    row = jnp.repeat(jnp.arange(rows, dtype=F32), GRID_W)
    col = jnp.tile(jnp.arange(GRID_W, dtype=F32), rows)
    inv_freq = jnp.power(ROPE_THETA, -jnp.arange(ROPE_AXIS_DIM // 2, dtype=F32) * 2.0 / ROPE_AXIS_DIM)
    ang = jnp.concatenate([row[:, None] * inv_freq, col[:, None] * inv_freq], axis=-1)
    ang = jnp.concatenate([ang, jnp.zeros((c_len, HEAD_DIM // 2), F32)], axis=0)
    cos, sin = jnp.cos(ang), jnp.sin(ang)
    return jnp.concatenate([cos, cos], axis=-1), jnp.concatenate([-sin, sin], axis=-1)


def kernel(x, c, ctx, c_ctx, ada_w, ada_b, ln_g, ln_b, ffn_w_up, ffn_conv_w, ffn_conv_b, ffn_w_down, attn_w_in, attn_q_norm, attn_k_norm, attn_sink, attn_w_out, gla_w_in, gla_w_a1, gla_w_a2, gla_b_a, gla_head_norm, gla_w_out):
    b, t_lat, d = x.shape
    c_len = ctx.shape[1]
    s = t_lat + c_len
    depth = ada_w.shape[0]
    assert b < MOD_CTX_ROW and t_lat % ROW_TILE == 0 and c_len % ROW_TILE == 0
    assert t_lat % GLA_CHUNK == 0 and c_len % GLA_CHUNK == 0 and s % MXU_N == 0
    alpha = (2 * depth) ** 0.25

    cond = jnp.zeros((MOD_ROWS, d), F32).at[:b].set(c).at[MOD_CTX_ROW].set(c_ctx)
    mods = _modulation(cond, ada_w, ada_b).reshape(depth, MOD_ROWS, 6, d)
    cos, sin = _rope_tables(t_lat, c_len)
    h = jnp.concatenate([x, ctx], axis=1)

    for i in range(depth):
        last = i == depth - 1
        n_rows = t_lat if last else s
        j = i // 2
        if i % 2 == 0:
            qkv = _attn_project(h, mods, i, attn_w_in[j].astype(MXU_DTYPE), attn_q_norm[j], attn_k_norm[j],
                                cos, sin, t_lat)
            o_a = _global_attention(qkv, t_lat)
            o_b = _window_attention(qkv, attn_sink[j], t_lat)
            h1 = _attn_out(o_a, o_b, h, mods, i, attn_w_out[j].astype(MXU_DTYPE), ln_g[i, 0], ln_b[i, 0],
                           t_lat, n_rows, alpha)
        else:
            w_a1 = jnp.concatenate([gla_w_a1[j, 0], gla_w_a1[j, 1]], axis=1)
            w_a1 = jnp.pad(w_a1, ((0, 0), (0, LANES - 2 * GLA_GATE_RANK))).astype(MXU_DTYPE)
            w_a2 = jnp.zeros((LANES, 2 * GLA_KEY_DIM), F32)
            w_a2 = w_a2.at[:GLA_GATE_RANK, :GLA_KEY_DIM].set(gla_w_a2[j, 0])
            w_a2 = w_a2.at[GLA_GATE_RANK:2 * GLA_GATE_RANK, GLA_KEY_DIM:].set(gla_w_a2[j, 1]).astype(MXU_DTYPE)
            proj, ld = _gla_project(h, mods, i, gla_w_in[j].astype(MXU_DTYPE), w_a1, w_a2,
                                    gla_b_a[j].reshape(1, 2 * GLA_KEY_DIM), t_lat)
            o_f, o_r = _gla_scan(proj, ld, t_lat)
            h1 = _gla_out(o_f, o_r, proj, h, mods, i, gla_head_norm[j], gla_w_out[j].astype(MXU_DTYPE),
                          ln_g[i, 0], ln_b[i, 0], t_lat, n_rows, alpha)
        h = _conv_ffn(h1, mods, i, ffn_w_up[i].astype(MXU_DTYPE), ffn_conv_w[i], ffn_conv_b[i],
                      ffn_w_down[i].astype(MXU_DTYPE), ln_g[i, 1], ln_b[i, 1], t_lat, n_rows, alpha)
    return h
```

```python
import functools

import jax
import jax.numpy as jnp
from jax import lax
from jax.experimental import pallas as pl
from jax.experimental.pallas import tpu as pltpu

GRID_W = 64
HEAD_DIM = 128
A_Q_HEADS = 4
A_KV_HEADS = 2
B_Q_HEADS = 4
B_KV_HEADS = 2
A_QW = A_Q_HEADS * HEAD_DIM
A_KVW = A_KV_HEADS * HEAD_DIM
B_QW = B_Q_HEADS * HEAD_DIM
B_KVW = B_KV_HEADS * HEAD_DIM
ATTN_IN = A_QW + 2 * A_KVW + B_QW + 2 * B_KVW
ATTN_SCALE = HEAD_DIM ** -0.5
WINDOW = 128
ROPE_THETA = 10000.0
ROPE_AXIS_DIM = HEAD_DIM // 2
GLA_HEADS = 4
GLA_DK = 128
GLA_DV = 256
GLA_KEY_DIM = GLA_HEADS * GLA_DK
GLA_VAL_DIM = GLA_HEADS * GLA_DV
GLA_IN = 2 * GLA_KEY_DIM + 2 * GLA_VAL_DIM
GLA_GATE_RANK = 16
GLA_GATE_NORM = 16.0
CONV_W = 3
EPS = 1e-6

LANES = 128
SUBLANES = 8
MXU_N = 256
VMEM_BYTES = 64 * 2 ** 20

MXU_DTYPE = jnp.bfloat16
F32 = jnp.float32

ROW_TILE = 256
HALO = 16
FF_CHUNK = 256
U_SLOTS = 3
ACT_SLOTS = 2
KV_CHUNK = 768
KV_SLOTS = 2
SOFTMAX_ROWS = 16
LOG2_E = 1.4426950408889634
GLA_CHUNK = 128
MOD_COLS = 1536


def _cparams(sem, vmem_mib):
    return pltpu.CompilerParams(dimension_semantics=sem, vmem_limit_bytes=vmem_mib * 2 ** 20)


def _dot(a, b):
    return jnp.dot(a, b, preferred_element_type=F32)


def _dot_nt(a, b):
    return lax.dot_general(a, b, (((1,), (1,)), ((), ())), preferred_element_type=F32)


def _dot_tn(a, b):
    return lax.dot_general(a, b, (((0,), (0,)), ((), ())), preferred_element_type=F32)


def _silu(x):
    return x / (1.0 + jnp.exp(-x))


def _residual_layer_norm(h, y, gate, ln_g, ln_b, alpha):
    r = alpha * h + gate * y
    mu = jnp.mean(r, axis=-1, keepdims=True)
    d = r - mu
    var = jnp.mean(d * d, axis=-1, keepdims=True)
    return d * lax.rsqrt(var + EPS) * ln_g + ln_b


def _mod_row(t_lat, tm):
    n_lat = t_lat // tm

    def row(b, i):
        return jnp.where(i < n_lat, b, MOD_CTX_ROW)
    return row


MOD_ROWS = 8
MOD_CTX_ROW = MOD_ROWS - 1


def _mod_kernel(cond_ref, w_ref, b_ref, o_ref):
    a = _silu(cond_ref[...]).astype(MXU_DTYPE)
    o_ref[...] = _dot(a, w_ref[...].astype(MXU_DTYPE)) + b_ref[...]


def _modulation(cond, ada_w, ada_b):
    depth, d, n = ada_w.shape
    return pl.pallas_call(
        _mod_kernel,
        grid=(depth, n // MOD_COLS),
        in_specs=[
            pl.BlockSpec((MOD_ROWS, d), lambda l, j: (0, 0)),
            pl.BlockSpec((None, d, MOD_COLS), lambda l, j: (l, 0, j)),
            pl.BlockSpec((None, 1, MOD_COLS), lambda l, j: (l, 0, j)),
        ],
        out_specs=pl.BlockSpec((None, MOD_ROWS, MOD_COLS), lambda l, j: (l, 0, j)),
        out_shape=jax.ShapeDtypeStruct((depth, MOD_ROWS, n), F32),
        compiler_params=_cparams(("parallel", "parallel"), 32),
        name="ada_modulation",
    )(cond, ada_w, ada_b.reshape(depth, 1, n))


def _attn_proj_kernel(h_ref, mod_ref, w_ref, qn_ref, kn_ref, cos_ref, sin_ref, o_ref):
    a = (h_ref[...] * (1.0 + mod_ref[1:2, :]) + mod_ref[0:1, :]).astype(MXU_DTYPE)
    cos = cos_ref[...]
    sin = sin_ref[...]

    def rms(z, g):
        return z * lax.rsqrt(jnp.mean(z * z, axis=-1, keepdims=True) + EPS) * g

    def rope(z):
        return z * cos + pltpu.roll(z, HEAD_DIM // 2, 1) * sin

    a_k0 = A_QW // HEAD_DIM
    a_v0 = a_k0 + A_KV_HEADS
    b_q0 = a_v0 + A_KV_HEADS
    b_k0 = b_q0 + B_Q_HEADS
    b_v0 = b_k0 + B_KV_HEADS
    heads_per_dot = MXU_N // HEAD_DIM
    for cb in range(ATTN_IN // MXU_N):
        z2 = _dot(a, w_ref[:, cb * MXU_N:(cb + 1) * MXU_N])
        for half in range(heads_per_dot):
            hb = cb * heads_per_dot + half
            z = z2[:, half * HEAD_DIM:(half + 1) * HEAD_DIM]
            if hb < a_k0:
                z = rope(rms(z, qn_ref[...])) * (ATTN_SCALE * LOG2_E)
            elif hb < a_v0:
                z = rope(rms(z, kn_ref[...]))
            elif hb < b_q0:
                pass
            elif hb < b_k0:
                z = rope(z) * ATTN_SCALE
            elif hb < b_v0:
                z = rope(z)
            o_ref[:, hb * HEAD_DIM:(hb + 1) * HEAD_DIM] = z.astype(o_ref.dtype)


def _attn_project(h, mods, layer, w_in, q_norm, k_norm, cos, sin, t_lat):
    b, s, d = h.shape
    tm = ROW_TILE
    row = _mod_row(t_lat, tm)
    return pl.pallas_call(
        _attn_proj_kernel,
        grid=(b, s // tm),
        in_specs=[
            pl.BlockSpec((None, tm, d), lambda bi, i: (bi, i, 0)),
            pl.BlockSpec((None, None, 6, d), lambda bi, i: (layer, row(bi, i), 0, 0)),
            pl.BlockSpec((d, ATTN_IN), lambda bi, i: (0, 0)),
            pl.BlockSpec((1, HEAD_DIM), lambda bi, i: (0, 0)),
            pl.BlockSpec((1, HEAD_DIM), lambda bi, i: (0, 0)),
            pl.BlockSpec((tm, HEAD_DIM), lambda bi, i: (i, 0)),
            pl.BlockSpec((tm, HEAD_DIM), lambda bi, i: (i, 0)),
        ],
        out_specs=pl.BlockSpec((None, tm, ATTN_IN), lambda bi, i: (bi, i, 0)),
        out_shape=jax.ShapeDtypeStruct((b, s, ATTN_IN), MXU_DTYPE),
        compiler_params=_cparams(("parallel", "parallel"), 40),
        name="attn_project",
    )(h, mods, w_in, q_norm.reshape(1, HEAD_DIM), k_norm.reshape(1, HEAD_DIM), cos, sin)


def _stack_heads(q):
    g = q.shape[1] // HEAD_DIM
    return jnp.concatenate([q[:, i * HEAD_DIM:(i + 1) * HEAD_DIM] for i in range(g)], axis=0)


def _store_heads(o_ref, o, tq):
    for i in range(o.shape[0] // tq):
        o_ref[:, i * HEAD_DIM:(i + 1) * HEAD_DIM] = o[i * tq:(i + 1) * tq, :].astype(o_ref.dtype)


def _softmax_chunk(s_ref, m_scr, alpha_ref, p_ref):
    rows, n_keys = s_ref.shape
    for g in range(rows // SOFTMAX_ROWS):
        rg = slice(g * SOFTMAX_ROWS, (g + 1) * SOFTMAX_ROWS)
        blocks = [s_ref[rg, j * LANES:(j + 1) * LANES] for j in range(n_keys // LANES)]
        mx = blocks[0]
        for blk in blocks[1:]:
            mx = jnp.maximum(mx, blk)
        m_prev = m_scr[rg, :]
        m_new = jnp.maximum(m_prev, jnp.max(mx, axis=-1, keepdims=True))
        alpha_ref[rg, :] = jnp.exp2(m_prev - m_new)
        for j, blk in enumerate(blocks):
            p_ref[rg, j * LANES:(j + 1) * LANES] = jnp.exp2(blk - m_new).astype(p_ref.dtype)
        m_scr[rg, :] = m_new


def _global_attn_kernel(q_ref, k_ref, v_ref, o_ref, vext_scr, m_scr, acc_scr, s_scr, p_scr, alpha_scr,
                        *, t_lat, tq, kv_chunk):
    s_tot = k_ref.shape[0]
    c_len = s_tot - t_lat
    is_ctx = pl.program_id(2) * tq >= t_lat

    @pl.when(pl.program_id(2) == 0)
    def _():
        ones_col = jnp.where(lax.broadcasted_iota(jnp.int32, (MXU_N, HEAD_DIM), 1) == 0, 1.0, 0.0)

        def fill(blk, carry):
            rows = pl.ds(pl.multiple_of(blk * MXU_N, MXU_N), MXU_N)
            vext_scr[rows, 0:HEAD_DIM] = v_ref[rows, :]
            vext_scr[rows, HEAD_DIM:2 * HEAD_DIM] = ones_col.astype(vext_scr.dtype)
            return carry

        lax.fori_loop(0, s_tot // MXU_N, fill, 0)

    q2 = _stack_heads(q_ref[...])
    m_scr[...] = jnp.full(m_scr.shape, -jnp.inf, F32)
    acc_scr[...] = jnp.zeros(acc_scr.shape, F32)

    def keys(c):
        return slice(c * kv_chunk, (c + 1) * kv_chunk)

    def accumulate(p, v_ext, alpha):
        acc_scr[...] = jnp.concatenate([alpha, alpha], axis=1) * acc_scr[...] + _dot(p, v_ext)

    @pl.when(jnp.logical_not(is_ctx))
    def _():
        n_chunks = s_tot // kv_chunk
        s_scr[0] = _dot_nt(q2, k_ref[keys(0), :])
        for c in range(n_chunks):
            slot = c % KV_SLOTS
            if c + 1 < n_chunks:
                s_scr[(c + 1) % KV_SLOTS] = _dot_nt(q2, k_ref[keys(c + 1), :])
            if c > 0:
                prev = (c - 1) % KV_SLOTS
                accumulate(p_scr[prev], vext_scr[keys(c - 1), :], alpha_scr[prev])
            _softmax_chunk(s_scr.at[slot], m_scr, alpha_scr.at[slot], p_scr.at[slot])
        last = (n_chunks - 1) % KV_SLOTS
        accumulate(p_scr[last], vext_scr[keys(n_chunks - 1), :], alpha_scr[last])

    @pl.when(is_ctx)
    def _():
        s_scr[0, :, 0:c_len] = _dot_nt(q2, k_ref[t_lat:s_tot, :])
        _softmax_chunk(s_scr.at[0, :, 0:c_len], m_scr, alpha_scr.at[0], p_scr.at[0, :, 0:c_len])
        accumulate(p_scr[0, :, 0:c_len], vext_scr[t_lat:s_tot, :], alpha_scr[0])

    acc = acc_scr[...]
    _store_heads(o_ref, acc[:, 0:HEAD_DIM] / acc[:, HEAD_DIM:HEAD_DIM + 1], tq)


def _kv_chunk(s_tot):
    return max(n for n in range(MXU_N, KV_CHUNK + 1, MXU_N) if s_tot % n == 0)


def _global_attention(qkv, t_lat):
    b, s, _ = qkv.shape
    tq = ROW_TILE
    group = A_Q_HEADS // A_KV_HEADS
    k0 = A_QW // HEAD_DIM
    v0 = k0 + A_KV_HEADS
    kv_chunk = _kv_chunk(s)
    assert s - t_lat <= kv_chunk
    return pl.pallas_call(
        functools.partial(_global_attn_kernel, t_lat=t_lat, tq=tq, kv_chunk=kv_chunk),
        grid=(b, A_KV_HEADS, s // tq),
        in_specs=[
            pl.BlockSpec((None, tq, group * HEAD_DIM), lambda bi, hk, i: (bi, i, hk)),
            pl.BlockSpec((None, s, HEAD_DIM), lambda bi, hk, i: (bi, 0, k0 + hk)),
            pl.BlockSpec((None, s, HEAD_DIM), lambda bi, hk, i: (bi, 0, v0 + hk)),
        ],
        out_specs=pl.BlockSpec((None, tq, group * HEAD_DIM), lambda bi, hk, i: (bi, i, hk)),
        out_shape=jax.ShapeDtypeStruct((b, s, A_QW), MXU_DTYPE),
        scratch_shapes=[
            pltpu.VMEM((s, 2 * HEAD_DIM), MXU_DTYPE),
            pltpu.VMEM((group * tq, LANES), F32),
            pltpu.VMEM((group * tq, 2 * HEAD_DIM), F32),
            pltpu.VMEM((KV_SLOTS, group * tq, kv_chunk), F32),
            pltpu.VMEM((KV_SLOTS, group * tq, kv_chunk), MXU_DTYPE),
            pltpu.VMEM((KV_SLOTS, group * tq, LANES), F32),
        ],
        compiler_params=_cparams(("parallel", "parallel", "arbitrary"), 40),
        name="global_attention",
    )(qkv, qkv, qkv)


def _window_attn_kernel(sink_ref, q_ref, k_ref, v_ref, o_ref, *, t_lat, tq):
    s_tot = k_ref.shape[0]
    c_len = s_tot - t_lat
    band = tq + 2 * WINDOW
    q0 = pl.program_id(1) * tq
    is_ctx = q0 >= t_lat
    start = pl.multiple_of(jnp.clip(q0 - WINDOW, 0, t_lat - band), WINDOW)
    group = B_Q_HEADS // B_KV_HEADS

    delta = (lax.broadcasted_iota(jnp.int32, (tq, band), 1) + (start - q0)
             - lax.broadcasted_iota(jnp.int32, (tq, band), 0))
    reach = jnp.where(is_ctx, -1, WINDOW)
    valid = jnp.abs(delta) <= reach
    for hk in range(B_KV_HEADS):
        kv = slice(hk * HEAD_DIM, (hk + 1) * HEAD_DIM)
        q2 = _stack_heads(q_ref[:, hk * group * HEAD_DIM:(hk + 1) * group * HEAD_DIM])
        s_band = _dot_nt(q2, k_ref[pl.ds(start, band), kv])
        s_ctx = _dot_nt(q2, k_ref[pl.ds(t_lat, c_len), kv])
        p_band, p_ctx, denom = [], [], []
        for g in range(group):
            head = slice(g * tq, (g + 1) * tq)
            sb = jnp.where(valid, s_band[head, :], -jnp.inf)
            sc = s_ctx[head, :]
            sink = sink_ref[hk * group + g]
            m = jnp.maximum(jnp.maximum(jnp.max(sb, axis=-1, keepdims=True),
                                        jnp.max(sc, axis=-1, keepdims=True)), sink)
            pb = jnp.exp(sb - m)
            pc = jnp.exp(sc - m)
            denom.append(jnp.sum(pb, axis=-1, keepdims=True) + jnp.sum(pc, axis=-1, keepdims=True)
                         + jnp.exp(sink - m))
            p_band.append(pb.astype(MXU_DTYPE))
            p_ctx.append(pc.astype(MXU_DTYPE))
        o = (_dot(jnp.concatenate(p_band, axis=0), v_ref[pl.ds(start, band), kv])
             + _dot(jnp.concatenate(p_ctx, axis=0), v_ref[pl.ds(t_lat, c_len), kv]))
        o = o / jnp.concatenate(denom, axis=0)
        for g in range(group):
            cols = slice((hk * group + g) * HEAD_DIM, (hk * group + g + 1) * HEAD_DIM)
            o_ref[:, cols] = o[g * tq:(g + 1) * tq, :].astype(o_ref.dtype)


def _window_attention(qkv, sink, t_lat):
    b, s, _ = qkv.shape
    tq = ROW_TILE
    assert tq & (tq - 1) == 0 and t_lat >= tq + 2 * WINDOW
    q0 = (A_QW + 2 * A_KVW) // B_QW
    k0 = (A_QW + 2 * A_KVW + B_QW) // B_KVW
    v0 = k0 + 1
    return pl.pallas_call(
        functools.partial(_window_attn_kernel, t_lat=t_lat, tq=tq),
        grid=(b, s // tq),
        in_specs=[
            pl.BlockSpec(memory_space=pltpu.SMEM),
            pl.BlockSpec((None, tq, B_QW), lambda bi, i: (bi, i, q0)),
            pl.BlockSpec((None, s, B_KVW), lambda bi, i: (bi, 0, k0)),
            pl.BlockSpec((None, s, B_KVW), lambda bi, i: (bi, 0, v0)),
        ],
        out_specs=pl.BlockSpec((None, tq, B_QW), lambda bi, i: (bi, i, 0)),
        out_shape=jax.ShapeDtypeStruct((b, s, B_QW), MXU_DTYPE),
        compiler_params=_cparams(("parallel", "arbitrary"), 40),
        name="window_attention",
    )(sink, qkv, qkv, qkv)


def _attn_out_kernel(oa_ref, ob_ref, h_ref, mod_ref, w_ref, lng_ref, lnb_ref, out_ref, *, alpha):
    y = _dot(oa_ref[...], w_ref[0:A_QW, :]) + _dot(ob_ref[...], w_ref[A_QW:A_QW + B_QW, :])
    out_ref[...] = _residual_layer_norm(h_ref[...], y, mod_ref[2:3, :], lng_ref[...], lnb_ref[...], alpha)


def _attn_out(o_a, o_b, h, mods, layer, w_out, ln_g, ln_b, t_lat, n_rows, alpha):
    b, s, d = h.shape
    tm = ROW_TILE
    row = _mod_row(t_lat, tm)
    return pl.pallas_call(
        functools.partial(_attn_out_kernel, alpha=alpha),
        grid=(b, n_rows // tm),
        in_specs=[
            pl.BlockSpec((None, tm, A_QW), lambda bi, i: (bi, i, 0)),
            pl.BlockSpec((None, tm, B_QW), lambda bi, i: (bi, i, 0)),
            pl.BlockSpec((None, tm, d), lambda bi, i: (bi, i, 0)),
            pl.BlockSpec((None, None, 6, d), lambda bi, i: (layer, row(bi, i), 0, 0)),
            pl.BlockSpec((A_QW + B_QW, d), lambda bi, i: (0, 0)),
            pl.BlockSpec((1, d), lambda bi, i: (0, 0)),
            pl.BlockSpec((1, d), lambda bi, i: (0, 0)),
        ],
        out_specs=pl.BlockSpec((None, tm, d), lambda bi, i: (bi, i, 0)),
        out_shape=jax.ShapeDtypeStruct((b, n_rows, d), F32),
        compiler_params=_cparams(("parallel", "parallel"), 40),
        name="attn_out_norm",
    )(o_a, o_b, h, mods, w_out, ln_g.reshape(1, d), ln_b.reshape(1, d))


def _log_sigmoid(x):
    return jnp.minimum(x, 0.0) - jnp.log(1.0 + jnp.exp(-jnp.abs(x)))


def _gla_proj_kernel(h_ref, mod_ref, w_ref, wa1_ref, wa2_ref, ba_ref, p_ref, ld_ref):
    a = (h_ref[...] * (1.0 + mod_ref[1:2, :]) + mod_ref[0:1, :]).astype(MXU_DTYPE)
    for cb in range(GLA_IN // MXU_N):
        z = _dot(a, w_ref[:, cb * MXU_N:(cb + 1) * MXU_N])
        if cb < GLA_KEY_DIM // MXU_N:
            z = z * (GLA_DK ** -0.5)
        p_ref[:, cb * MXU_N:(cb + 1) * MXU_N] = z.astype(p_ref.dtype)
    low = _dot(a, wa1_ref[...]).astype(MXU_DTYPE)
    for cb in range(2 * GLA_KEY_DIM // MXU_N):
        cols = slice(cb * MXU_N, (cb + 1) * MXU_N)
        logits = _dot(low, wa2_ref[:, cols]) + ba_ref[:, cols]
        ld_ref[:, cols] = _log_sigmoid(logits) / GLA_GATE_NORM


def _gla_project(h, mods, layer, w_in, w_a1, w_a2, b_a, t_lat):
    b, s, d = h.shape
    tm = ROW_TILE
    row = _mod_row(t_lat, tm)
    return pl.pallas_call(
        _gla_proj_kernel,
        grid=(b, s // tm),
        in_specs=[
            pl.BlockSpec((None, tm, d), lambda bi, i: (bi, i, 0)),
            pl.BlockSpec((None, None, 6, d), lambda bi, i: (layer, row(bi, i), 0, 0)),
            pl.BlockSpec((d, GLA_IN), lambda bi, i: (0, 0)),
            pl.BlockSpec((d, LANES), lambda bi, i: (0, 0)),
            pl.BlockSpec((LANES, 2 * GLA_KEY_DIM), lambda bi, i: (0, 0)),
            pl.BlockSpec((1, 2 * GLA_KEY_DIM), lambda bi, i: (0, 0)),
        ],
        out_specs=[
            pl.BlockSpec((None, tm, GLA_IN), lambda bi, i: (bi, i, 0)),
            pl.BlockSpec((None, tm, 2 * GLA_KEY_DIM), lambda bi, i: (bi, i, 0)),
        ],
        out_shape=[
            jax.ShapeDtypeStruct((b, s, GLA_IN), MXU_DTYPE),
            jax.ShapeDtypeStruct((b, s, 2 * GLA_KEY_DIM), F32),
        ],
        compiler_params=_cparams(("parallel", "parallel"), 48),
        name="gla_project",
    )(h, mods, w_in, w_a1, w_a2, b_a)


def _cumsum_rows(x):
    rows, cols = x.shape
    sub = lax.broadcasted_iota(jnp.int32, x.shape, 0) & (SUBLANES - 1)
    shift = 1
    while shift < SUBLANES:
        x = x + jnp.where(sub >= shift, pltpu.roll(x, shift, 0), 0.0)
        shift *= 2
    x3 = x.reshape(rows // SUBLANES, SUBLANES, cols)
    carry = jnp.zeros((1, cols), F32)
    tiles = []
    for i in range(rows // SUBLANES):
        tile = x3[i] + carry
        tiles.append(tile)
        carry = tile[SUBLANES - 1:SUBLANES, :]
    return jnp.concatenate(tiles, axis=0)


def _block_anchor(p, n):
    rows, cols = p.shape
    half = n // 2
    if n >= SUBLANES:
        p3 = p.reshape(rows // n, n, cols)
        return jnp.broadcast_to(p3[:, half - 1:half, :], p3.shape).reshape(rows, cols)
    if n == 2:
        odd = (lax.broadcasted_iota(jnp.int32, p.shape, 0) & 1) == 1
        return jnp.where(odd, pltpu.roll(p, 1, 0), p)
    p3 = p.reshape(rows // SUBLANES, SUBLANES, cols)
    sub = lax.broadcasted_iota(jnp.int32, p3.shape, 1)
    a = None
    for b0 in range(0, SUBLANES, n):
        row = jnp.broadcast_to(p3[:, b0 + half - 1:b0 + half, :], p3.shape)
        a = row if a is None else jnp.where(sub >= b0, row, a)
    return a.reshape(rows, cols)


def _gla_chunk(q, k, v, ld, st_ref, reverse):
    chunk = q.shape[0]
    q = q.astype(F32)
    k = k.astype(F32)
    p_inc = _cumsum_rows(ld)
    p_tot = p_inc[chunk - 1:chunk, :]
    px = p_inc - ld if reverse else p_inc

    row = lax.broadcasted_iota(jnp.int32, (chunk, chunk), 0)
    col = lax.broadcasted_iota(jnp.int32, (chunk, chunk), 1)
    differ = row ^ col
    row_d = lax.broadcasted_iota(jnp.int32, q.shape, 0)
    att = jnp.broadcast_to(jnp.sum(q * k, axis=-1, keepdims=True), (chunk, chunk))
    n = 2
    while n <= chunk:
        upper = (row_d & (n - 1)) >= n // 2
        anchor = _block_anchor(p_inc, n)
        e = jnp.exp(jnp.where(upper, px - anchor, anchor - px))
        s_lvl = _dot_nt((q * e).astype(MXU_DTYPE), (k * e).astype(MXU_DTYPE))
        att = jnp.where((differ >> (n.bit_length() - 2)) == 1, s_lvl, att)
        n *= 2
    att = jnp.where(row <= col if reverse else row >= col, att, 0.0)

    if reverse:
        qd = q * jnp.exp(p_tot - px)
        kd = k * jnp.exp(px)
    else:
        qd = q * jnp.exp(px)
        kd = k * jnp.exp(p_tot - px)
    state = st_ref[...]
    o = _dot(att.astype(MXU_DTYPE), v) + _dot_nt(qd.astype(MXU_DTYPE), state.astype(MXU_DTYPE))
    st_ref[...] = state * jnp.exp(p_tot) + _dot_tn(v, kd.astype(MXU_DTYPE))
    return o


def _gla_scan_kernel(qf_ref, kf_ref, vf_ref, ldf_ref, qr_ref, kr_ref, vr_ref, ldr_ref,
                     of_ref, or_ref, st_ref):
    @pl.when(pl.program_id(1) == 0)
    def _():
        st_ref[...] = jnp.zeros(st_ref.shape, F32)

    operands = ((qf_ref, kf_ref, vf_ref, ldf_ref, of_ref), (qr_ref, kr_ref, vr_ref, ldr_ref, or_ref))
    for direction, (q_ref, k_ref, v_ref, ld_ref, o_ref) in enumerate(operands):
        for hd in range(GLA_HEADS):
            kc = slice(hd * GLA_DK, (hd + 1) * GLA_DK)
            vc = slice(hd * GLA_DV, (hd + 1) * GLA_DV)
            o = _gla_chunk(q_ref[:, kc], k_ref[:, kc], v_ref[:, vc], ld_ref[:, kc],
                           st_ref.at[direction, hd], reverse=direction == 1)
            o_ref[:, vc] = o.astype(o_ref.dtype)


def _gla_scan(proj, ld, t_lat):
    b, s, _ = proj.shape
    ch = GLA_CHUNK
    n_chunks = s // ch
    lat_chunks = t_lat // ch

    def fwd(c):
        return (c + lat_chunks) % n_chunks

    def rev(c):
        return n_chunks - 1 - c

    def specs(chunk_of, direction):
        return [
            pl.BlockSpec((None, ch, GLA_KEY_DIM), lambda bi, c: (bi, chunk_of(c), 0)),
            pl.BlockSpec((None, ch, GLA_KEY_DIM), lambda bi, c: (bi, chunk_of(c), 1)),
            pl.BlockSpec((None, ch, GLA_VAL_DIM), lambda bi, c: (bi, chunk_of(c), 1)),
            pl.BlockSpec((None, ch, GLA_KEY_DIM), lambda bi, c: (bi, chunk_of(c), direction)),
        ]

    o_shape = jax.ShapeDtypeStruct((b, s, GLA_VAL_DIM), F32)
    return pl.pallas_call(
        _gla_scan_kernel,
        grid=(b, n_chunks),
        in_specs=specs(fwd, 0) + specs(rev, 1),
        out_specs=[
            pl.BlockSpec((None, ch, GLA_VAL_DIM), lambda bi, c: (bi, fwd(c), 0)),
            pl.BlockSpec((None, ch, GLA_VAL_DIM), lambda bi, c: (bi, rev(c), 0)),
        ],
        out_shape=[o_shape, o_shape],
        scratch_shapes=[pltpu.VMEM((2, GLA_HEADS, GLA_DV, GLA_DK), F32)],
        compiler_params=_cparams(("parallel", "arbitrary"), 32),
        name="gla_scan",
    )(proj, proj, proj, ld, proj, proj, proj, ld)


def _gla_out_kernel(of_ref, ob_ref, g_ref, h_ref, mod_ref, hn_ref, w_ref, lng_ref, lnb_ref, out_ref, *, alpha):
    o = of_ref[...] + ob_ref[...]
    gate = _silu(g_ref[...].astype(F32))
    y = jnp.zeros(h_ref.shape, F32)
    for hd in range(GLA_HEADS):
        cols = slice(hd * GLA_DV, (hd + 1) * GLA_DV)
        oh = o[:, cols]
        oh = oh * lax.rsqrt(jnp.mean(oh * oh, axis=-1, keepdims=True) + EPS) * hn_ref[...]
        y = y + _dot((oh * gate[:, cols]).astype(MXU_DTYPE), w_ref[cols, :])
    out_ref[...] = _residual_layer_norm(h_ref[...], y, mod_ref[2:3, :], lng_ref[...], lnb_ref[...], alpha)


def _gla_out(o_f, o_b, proj, h, mods, layer, head_norm, w_out, ln_g, ln_b, t_lat, n_rows, alpha):
    b, s, d = h.shape
    tm = ROW_TILE
    row = _mod_row(t_lat, tm)
    g_blk = (2 * GLA_KEY_DIM + GLA_VAL_DIM) // GLA_VAL_DIM
    return pl.pallas_call(
        functools.partial(_gla_out_kernel, alpha=alpha),
        grid=(b, n_rows // tm),
        in_specs=[
            pl.BlockSpec((None, tm, GLA_VAL_DIM), lambda bi, i: (bi, i, 0)),
            pl.BlockSpec((None, tm, GLA_VAL_DIM), lambda bi, i: (bi, i, 0)),
            pl.BlockSpec((None, tm, GLA_VAL_DIM), lambda bi, i: (bi, i, g_blk)),
            pl.BlockSpec((None, tm, d), lambda bi, i: (bi, i, 0)),
            pl.BlockSpec((None, None, 6, d), lambda bi, i: (layer, row(bi, i), 0, 0)),
            pl.BlockSpec((1, GLA_DV), lambda bi, i: (0, 0)),
            pl.BlockSpec((GLA_VAL_DIM, d), lambda bi, i: (0, 0)),
            pl.BlockSpec((1, d), lambda bi, i: (0, 0)),
            pl.BlockSpec((1, d), lambda bi, i: (0, 0)),
        ],
        out_specs=pl.BlockSpec((None, tm, d), lambda bi, i: (bi, i, 0)),
        out_shape=jax.ShapeDtypeStruct((b, n_rows, d), F32),
        compiler_params=_cparams(("parallel", "parallel"), 40),
        name="gla_out_norm",
    )(o_f, o_b, proj, h, mods, head_norm.reshape(1, GLA_DV), w_out, ln_g.reshape(1, d), ln_b.reshape(1, d))


def _ffn_kernel(h_ref, hp_ref, hn_ref, mod_ref, wup_ref, cw_ref, cb_ref, wdn_ref, lng_ref, lnb_ref,
                out_ref, a_scr, u_scr, act_scr, *, t_lat, s_tot, alpha):
    tm = h_ref.shape[0]
    d_ff = wdn_ref.shape[0]
    row0 = pl.program_id(1) * tm
    at_start = (row0 == 0) | (row0 == t_lat)
    at_end = (row0 + tm == t_lat) | (row0 + tm == s_tot)
    shift = mod_ref[3:4, :]
    scale = 1.0 + mod_ref[4:5, :]
    h = h_ref[...]
    a_scr[0:HALO, :] = jnp.where(at_start, 0.0, hp_ref[...] * scale + shift).astype(a_scr.dtype)
    a_scr[HALO:HALO + tm, :] = (h * scale + shift).astype(a_scr.dtype)
    a_scr[HALO + tm:2 * HALO + tm, :] = jnp.where(at_end, 0.0, hn_ref[...] * scale + shift).astype(a_scr.dtype)
    a = a_scr[...]

    def conv(u_ref, c0):
        cols = slice(c0, c0 + FF_CHUNK)
        acc = cb_ref[:, cols] + cw_ref[0:1, cols] * u_ref[pl.ds(HALO - 1, tm), :]
        for j in range(1, CONV_W):
            acc = acc + cw_ref[j:j + 1, cols] * u_ref[pl.ds(HALO - 1 + j, tm), :]
        return acc

    def up_project(c):
        g0 = c * FF_CHUNK
        v0 = d_ff + c * FF_CHUNK
        u_scr[c % U_SLOTS, 0] = _dot(a, wup_ref[:, g0:g0 + FF_CHUNK])
        u_scr[c % U_SLOTS, 1] = _dot(a, wup_ref[:, v0:v0 + FF_CHUNK])

    def down_project(c):
        return _dot(act_scr[c % ACT_SLOTS], wdn_ref[c * FF_CHUNK:(c + 1) * FF_CHUNK, :])

    n_chunks = d_ff // FF_CHUNK
    ahead = U_SLOTS - 1
    y = jnp.zeros(h.shape, F32)
    for c in range(min(ahead, n_chunks)):
        up_project(c)
    for c in range(n_chunks):
        if c + ahead < n_chunks:
            up_project(c + ahead)
        if c > 0:
            y = y + down_project(c - 1)
        g0 = c * FF_CHUNK
        act = _silu(conv(u_scr.at[c % U_SLOTS, 0], g0)) * conv(u_scr.at[c % U_SLOTS, 1], d_ff + g0)
        act_scr[c % ACT_SLOTS] = act.astype(act_scr.dtype)
    y = y + down_project(n_chunks - 1)
    out_ref[...] = _residual_layer_norm(h, y, mod_ref[5:6, :], lng_ref[...], lnb_ref[...], alpha)


def _conv_ffn(h, mods, layer, w_up, conv_w, conv_b, w_down, ln_g, ln_b, t_lat, n_rows, alpha):
    b, s, d = h.shape
    tm = ROW_TILE
    d_ff = w_down.shape[0]
    row = _mod_row(t_lat, tm)
    hb = tm // HALO
    n_halo = s // HALO
    return pl.pallas_call(
        functools.partial(_ffn_kernel, t_lat=t_lat, s_tot=s, alpha=alpha),
        grid=(b, n_rows // tm),
        in_specs=[
            pl.BlockSpec((None, tm, d), lambda bi, i: (bi, i, 0)),
            pl.BlockSpec((None, HALO, d), lambda bi, i: (bi, jnp.maximum(i * hb - 1, 0), 0)),
            pl.BlockSpec((None, HALO, d), lambda bi, i: (bi, jnp.minimum((i + 1) * hb, n_halo - 1), 0)),
            pl.BlockSpec((None, None, 6, d), lambda bi, i: (layer, row(bi, i), 0, 0)),
            pl.BlockSpec((d, 2 * d_ff), lambda bi, i: (0, 0)),
            pl.BlockSpec((CONV_W, 2 * d_ff), lambda bi, i: (0, 0)),
            pl.BlockSpec((1, 2 * d_ff), lambda bi, i: (0, 0)),
            pl.BlockSpec((d_ff, d), lambda bi, i: (0, 0)),
            pl.BlockSpec((1, d), lambda bi, i: (0, 0)),
            pl.BlockSpec((1, d), lambda bi, i: (0, 0)),
        ],
        out_specs=pl.BlockSpec((None, tm, d), lambda bi, i: (bi, i, 0)),
        out_shape=jax.ShapeDtypeStruct((b, n_rows, d), F32),
        scratch_shapes=[
            pltpu.VMEM((tm + 2 * HALO, d), MXU_DTYPE),
            pltpu.VMEM((U_SLOTS, 2, tm + 2 * HALO, FF_CHUNK), F32),
            pltpu.VMEM((ACT_SLOTS, tm, FF_CHUNK), MXU_DTYPE),
        ],
        compiler_params=_cparams(("parallel", "parallel"), 56),
        name="conv_ffn_norm",
    )(h, h, h, mods, w_up, conv_w, conv_b.reshape(1, 2 * d_ff), w_down, ln_g.reshape(1, d), ln_b.reshape(1, d))


def _rope_tables(t_lat, c_len):
    rows = t_lat // GRID_W
    row = jnp.repeat(jnp.arange(rows, dtype=F32), GRID_W)
    col = jnp.tile(jnp.arange(GRID_W, dtype=F32), rows)
    inv_freq = jnp.power(ROPE_THETA, -jnp.arange(ROPE_AXIS_DIM // 2, dtype=F32) * 2.0 / ROPE_AXIS_DIM)
    ang = jnp.concatenate([row[:, None] * inv_freq, col[:, None] * inv_freq], axis=-1)
    ang = jnp.concatenate([ang, jnp.zeros((c_len, HEAD_DIM // 2), F32)], axis=0)
    cos, sin = jnp.cos(ang), jnp.sin(ang)
    return jnp.concatenate([cos, cos], axis=-1), jnp.concatenate([-sin, sin], axis=-1)


def kernel(x, c, ctx, c_ctx, ada_w, ada_b, ln_g, ln_b, ffn_w_up, ffn_conv_w, ffn_conv_b, ffn_w_down, attn_w_in, attn_q_norm, attn_k_norm, attn_sink, attn_w_out, gla_w_in, gla_w_a1, gla_w_a2, gla_b_a, gla_head_norm, gla_w_out):
    b, t_lat, d = x.shape
    c_len = ctx.shape[1]
    s = t_lat + c_len
    depth = ada_w.shape[0]
    assert b < MOD_CTX_ROW and t_lat % ROW_TILE == 0 and c_len % ROW_TILE == 0
    assert t_lat % GLA_CHUNK == 0 and c_len % GLA_CHUNK == 0 and s % MXU_N == 0
    alpha = (2 * depth) ** 0.25

    cond = jnp.zeros((MOD_ROWS, d), F32).at[:b].set(c).at[MOD_CTX_ROW].set(c_ctx)
    mods = _modulation(cond, ada_w, ada_b).reshape(depth, MOD_ROWS, 6, d)
    cos, sin = _rope_tables(t_lat, c_len)
    h = jnp.concatenate([x, ctx], axis=1)

    for i in range(depth):
        last = i == depth - 1
        n_rows = t_lat if last else s
        j = i // 2
        if i % 2 == 0:
            qkv = _attn_project(h, mods, i, attn_w_in[j].astype(MXU_DTYPE), attn_q_norm[j], attn_k_norm[j],
                                cos, sin, t_lat)
            o_a = _global_attention(qkv, t_lat)
            o_b = _window_attention(qkv, attn_sink[j], t_lat)
            h1 = _attn_out(o_a, o_b, h, mods, i, attn_w_out[j].astype(MXU_DTYPE), ln_g[i, 0], ln_b[i, 0],
                           t_lat, n_rows, alpha)
        else:
            w_a1 = jnp.concatenate([gla_w_a1[j, 0], gla_w_a1[j, 1]], axis=1)
            w_a1 = jnp.pad(w_a1, ((0, 0), (0, LANES - 2 * GLA_GATE_RANK))).astype(MXU_DTYPE)
            w_a2 = jnp.zeros((LANES, 2 * GLA_KEY_DIM), F32)
            w_a2 = w_a2.at[:GLA_GATE_RANK, :GLA_KEY_DIM].set(gla_w_a2[j, 0])
            w_a2 = w_a2.at[GLA_GATE_RANK:2 * GLA_GATE_RANK, GLA_KEY_DIM:].set(gla_w_a2[j, 1]).astype(MXU_DTYPE)
            proj, ld = _gla_project(h, mods, i, gla_w_in[j].astype(MXU_DTYPE), w_a1, w_a2,
                                    gla_b_a[j].reshape(1, 2 * GLA_KEY_DIM), t_lat)
            o_f, o_r = _gla_scan(proj, ld, t_lat)
            h1 = _gla_out(o_f, o_r, proj, h, mods, i, gla_head_norm[j], gla_w_out[j].astype(MXU_DTYPE),
                          ln_g[i, 0], ln_b[i, 0], t_lat, n_rows, alpha)
        h = _conv_ffn(h1, mods, i, ffn_w_up[i].astype(MXU_DTYPE), ffn_conv_w[i], ffn_conv_b[i],
                      ffn_w_down[i].astype(MXU_DTYPE), ln_g[i, 1], ln_b[i, 1], t_lat, n_rows, alpha)
    return h
```

```python
import functools

import jax
import jax.numpy as jnp
from jax import lax
from jax.experimental import pallas as pl
from jax.experimental.pallas import tpu as pltpu

GRID_W = 64
HEAD_DIM = 128
A_Q_HEADS = 4
A_KV_HEADS = 2
B_Q_HEADS = 4
B_KV_HEADS = 2
A_QW = A_Q_HEADS * HEAD_DIM
A_KVW = A_KV_HEADS * HEAD_DIM
B_QW = B_Q_HEADS * HEAD_DIM
B_KVW = B_KV_HEADS * HEAD_DIM
ATTN_IN = A_QW + 2 * A_KVW + B_QW + 2 * B_KVW
ATTN_SCALE = HEAD_DIM ** -0.5
WINDOW = 128
ROPE_THETA = 10000.0
ROPE_AXIS_DIM = HEAD_DIM // 2
GLA_HEADS = 4
GLA_DK = 128
GLA_DV = 256
GLA_KEY_DIM = GLA_HEADS * GLA_DK
GLA_VAL_DIM = GLA_HEADS * GLA_DV
GLA_IN = 2 * GLA_KEY_DIM + 2 * GLA_VAL_DIM
GLA_GATE_RANK = 16
GLA_GATE_NORM = 16.0
CONV_W = 3
EPS = 1e-6

LANES = 128
SUBLANES = 8
MXU_N = 256
VMEM_BYTES = 64 * 2 ** 20

MXU_DTYPE = jnp.bfloat16
F32 = jnp.float32

ROW_TILE = 256
HALO = 16
FF_CHUNK = 256
U_SLOTS = 3
ACT_SLOTS = 2
KV_CHUNK = 768
KV_SLOTS = 2
SOFTMAX_ROWS = 16
LOG2_E = 1.4426950408889634
GLA_CHUNK = 128
MOD_COLS = 1536


def _cparams(sem, vmem_mib):
    return pltpu.CompilerParams(dimension_semantics=sem, vmem_limit_bytes=vmem_mib * 2 ** 20)


def _dot(a, b):
    return jnp.dot(a, b, preferred_element_type=F32)


def _dot_nt(a, b):
    return lax.dot_general(a, b, (((1,), (1,)), ((), ())), preferred_element_type=F32)


def _dot_tn(a, b):
    return lax.dot_general(a, b, (((0,), (0,)), ((), ())), preferred_element_type=F32)


def _silu(x):
    return x / (1.0 + jnp.exp(-x))


def _residual_layer_norm(h, y, gate, ln_g, ln_b, alpha):
    r = alpha * h + gate * y
    mu = jnp.mean(r, axis=-1, keepdims=True)
    d = r - mu
    var = jnp.mean(d * d, axis=-1, keepdims=True)
    return d * lax.rsqrt(var + EPS) * ln_g + ln_b


def _mod_row(t_lat, tm):
    n_lat = t_lat // tm

    def row(b, i):
        return jnp.where(i < n_lat, b, MOD_CTX_ROW)
    return row


MOD_ROWS = 8
MOD_CTX_ROW = MOD_ROWS - 1


def _mod_kernel(cond_ref, w_ref, b_ref, o_ref):
    a = _silu(cond_ref[...]).astype(MXU_DTYPE)
    o_ref[...] = _dot(a, w_ref[...].astype(MXU_DTYPE)) + b_ref[...]


def _modulation(cond, ada_w, ada_b):
    depth, d, n = ada_w.shape
    return pl.pallas_call(
        _mod_kernel,
        grid=(depth, n // MOD_COLS),
        in_specs=[
            pl.BlockSpec((MOD_ROWS, d), lambda l, j: (0, 0)),
            pl.BlockSpec((None, d, MOD_COLS), lambda l, j: (l, 0, j)),
            pl.BlockSpec((None, 1, MOD_COLS), lambda l, j: (l, 0, j)),
        ],
        out_specs=pl.BlockSpec((None, MOD_ROWS, MOD_COLS), lambda l, j: (l, 0, j)),
        out_shape=jax.ShapeDtypeStruct((depth, MOD_ROWS, n), F32),
        compiler_params=_cparams(("parallel", "parallel"), 32),
        name="ada_modulation",
    )(cond, ada_w, ada_b.reshape(depth, 1, n))


def _attn_proj_kernel(h_ref, mod_ref, w_ref, qn_ref, kn_ref, cos_ref, sin_ref, o_ref):
    a = (h_ref[...] * (1.0 + mod_ref[1:2, :]) + mod_ref[0:1, :]).astype(MXU_DTYPE)
    cos = cos_ref[...]
    sin = sin_ref[...]

    def rms(z, g):
        return z * lax.rsqrt(jnp.mean(z * z, axis=-1, keepdims=True) + EPS) * g

    def rope(z):
        return z * cos + pltpu.roll(z, HEAD_DIM // 2, 1) * sin

    a_k0 = A_QW // HEAD_DIM
    a_v0 = a_k0 + A_KV_HEADS
    b_q0 = a_v0 + A_KV_HEADS
    b_k0 = b_q0 + B_Q_HEADS
    b_v0 = b_k0 + B_KV_HEADS
    heads_per_dot = MXU_N // HEAD_DIM
    for cb in range(ATTN_IN // MXU_N):
        z2 = _dot(a, w_ref[:, cb * MXU_N:(cb + 1) * MXU_N])
        for half in range(heads_per_dot):
            hb = cb * heads_per_dot + half
            z = z2[:, half * HEAD_DIM:(half + 1) * HEAD_DIM]
            if hb < a_k0:
                z = rope(rms(z, qn_ref[...])) * (ATTN_SCALE * LOG2_E)
            elif hb < a_v0:
                z = rope(rms(z, kn_ref[...]))
            elif hb < b_q0:
                pass
            elif hb < b_k0:
                z = rope(z) * ATTN_SCALE
            elif hb < b_v0:
                z = rope(z)
            o_ref[:, hb * HEAD_DIM:(hb + 1) * HEAD_DIM] = z.astype(o_ref.dtype)


def _attn_project(h, mods, layer, w_in, mixer, q_norm, k_norm, cos, sin, t_lat):
    b, s, d = h.shape
    tm = ROW_TILE
    row = _mod_row(t_lat, tm)
    return pl.pallas_call(
        _attn_proj_kernel,
        grid=(b, s // tm),
        in_specs=[
            pl.BlockSpec((None, tm, d), lambda bi, i: (bi, i, 0)),
            pl.BlockSpec((None, None, 6, d), lambda bi, i: (layer, row(bi, i), 0, 0)),
            pl.BlockSpec((None, d, ATTN_IN), lambda bi, i: (mixer, 0, 0)),
            pl.BlockSpec((1, HEAD_DIM), lambda bi, i: (0, 0)),
            pl.BlockSpec((1, HEAD_DIM), lambda bi, i: (0, 0)),
            pl.BlockSpec((tm, HEAD_DIM), lambda bi, i: (i, 0)),
            pl.BlockSpec((tm, HEAD_DIM), lambda bi, i: (i, 0)),
        ],
        out_specs=pl.BlockSpec((None, tm, ATTN_IN), lambda bi, i: (bi, i, 0)),
        out_shape=jax.ShapeDtypeStruct((b, s, ATTN_IN), MXU_DTYPE),
        compiler_params=_cparams(("parallel", "parallel"), 40),
        name="attn_project",
    )(h, mods, w_in, q_norm.reshape(1, HEAD_DIM), k_norm.reshape(1, HEAD_DIM), cos, sin)


def _stack_heads(q):
    g = q.shape[1] // HEAD_DIM
    return jnp.concatenate([q[:, i * HEAD_DIM:(i + 1) * HEAD_DIM] for i in range(g)], axis=0)


def _store_heads(o_ref, o, tq):
    for i in range(o.shape[0] // tq):
        o_ref[:, i * HEAD_DIM:(i + 1) * HEAD_DIM] = o[i * tq:(i + 1) * tq, :].astype(o_ref.dtype)


def _softmax_chunk(s_ref, m_scr, alpha_ref, p_ref):
    rows, n_keys = s_ref.shape
    for g in range(rows // SOFTMAX_ROWS):
        rg = slice(g * SOFTMAX_ROWS, (g + 1) * SOFTMAX_ROWS)
        blocks = [s_ref[rg, j * LANES:(j + 1) * LANES] for j in range(n_keys // LANES)]
        mx = blocks[0]
        for blk in blocks[1:]:
            mx = jnp.maximum(mx, blk)
        m_prev = m_scr[rg, :]
        m_new = jnp.maximum(m_prev, jnp.max(mx, axis=-1, keepdims=True))
        alpha_ref[rg, :] = jnp.exp2(m_prev - m_new)
        for j, blk in enumerate(blocks):
            p_ref[rg, j * LANES:(j + 1) * LANES] = jnp.exp2(blk - m_new).astype(p_ref.dtype)
        m_scr[rg, :] = m_new


def _global_attn_kernel(q_ref, k_ref, v_ref, o_ref, vext_scr, m_scr, acc_scr, s_scr, p_scr, alpha_scr,
                        *, t_lat, tq, kv_chunk):
    s_tot = k_ref.shape[0]
    c_len = s_tot - t_lat
    is_ctx = pl.program_id(2) * tq >= t_lat

    @pl.when(pl.program_id(2) == 0)
    def _():
        ones_col = jnp.where(lax.broadcasted_iota(jnp.int32, (MXU_N, HEAD_DIM), 1) == 0, 1.0, 0.0)

        def fill(blk, carry):
            rows = pl.ds(pl.multiple_of(blk * MXU_N, MXU_N), MXU_N)
            vext_scr[rows, 0:HEAD_DIM] = v_ref[rows, :]
            vext_scr[rows, HEAD_DIM:2 * HEAD_DIM] = ones_col.astype(vext_scr.dtype)
            return carry

        lax.fori_loop(0, s_tot // MXU_N, fill, 0)

    q2 = _stack_heads(q_ref[...])
    m_scr[...] = jnp.full(m_scr.shape, -jnp.inf, F32)
    acc_scr[...] = jnp.zeros(acc_scr.shape, F32)

    def keys(c):
        return slice(c * kv_chunk, (c + 1) * kv_chunk)

    def accumulate(p, v_ext, alpha):
        acc_scr[...] = jnp.concatenate([alpha, alpha], axis=1) * acc_scr[...] + _dot(p, v_ext)

    @pl.when(jnp.logical_not(is_ctx))
    def _():
        n_chunks = s_tot // kv_chunk
        s_scr[0] = _dot_nt(q2, k_ref[keys(0), :])
        for c in range(n_chunks):
            slot = c % KV_SLOTS
            if c + 1 < n_chunks:
                s_scr[(c + 1) % KV_SLOTS] = _dot_nt(q2, k_ref[keys(c + 1), :])
            if c > 0:
                prev = (c - 1) % KV_SLOTS
                accumulate(p_scr[prev], vext_scr[keys(c - 1), :], alpha_scr[prev])
            _softmax_chunk(s_scr.at[slot], m_scr, alpha_scr.at[slot], p_scr.at[slot])
        last = (n_chunks - 1) % KV_SLOTS
        accumulate(p_scr[last], vext_scr[keys(n_chunks - 1), :], alpha_scr[last])

    @pl.when(is_ctx)
    def _():
        s_scr[0, :, 0:c_len] = _dot_nt(q2, k_ref[t_lat:s_tot, :])
        _softmax_chunk(s_scr.at[0, :, 0:c_len], m_scr, alpha_scr.at[0], p_scr.at[0, :, 0:c_len])
        accumulate(p_scr[0, :, 0:c_len], vext_scr[t_lat:s_tot, :], alpha_scr[0])

    acc = acc_scr[...]
    _store_heads(o_ref, acc[:, 0:HEAD_DIM] / acc[:, HEAD_DIM:HEAD_DIM + 1], tq)


def _kv_chunk(s_tot):
    return max(n for n in range(MXU_N, KV_CHUNK + 1, MXU_N) if s_tot % n == 0)


def _global_attention(qkv, t_lat):
    b, s, _ = qkv.shape
    tq = ROW_TILE
    group = A_Q_HEADS // A_KV_HEADS
    k0 = A_QW // HEAD_DIM
    v0 = k0 + A_KV_HEADS
    kv_chunk = _kv_chunk(s)
    assert s - t_lat <= kv_chunk
    return pl.pallas_call(
        functools.partial(_global_attn_kernel, t_lat=t_lat, tq=tq, kv_chunk=kv_chunk),
        grid=(b, A_KV_HEADS, s // tq),
        in_specs=[
            pl.BlockSpec((None, tq, group * HEAD_DIM), lambda bi, hk, i: (bi, i, hk)),
            pl.BlockSpec((None, s, HEAD_DIM), lambda bi, hk, i: (bi, 0, k0 + hk)),
            pl.BlockSpec((None, s, HEAD_DIM), lambda bi, hk, i: (bi, 0, v0 + hk)),
        ],
        out_specs=pl.BlockSpec((None, tq, group * HEAD_DIM), lambda bi, hk, i: (bi, i, hk)),
        out_shape=jax.ShapeDtypeStruct((b, s, A_QW), MXU_DTYPE),
        scratch_shapes=[
            pltpu.VMEM((s, 2 * HEAD_DIM), MXU_DTYPE),
            pltpu.VMEM((group * tq, LANES), F32),
            pltpu.VMEM((group * tq, 2 * HEAD_DIM), F32),
            pltpu.VMEM((KV_SLOTS, group * tq, kv_chunk), F32),
            pltpu.VMEM((KV_SLOTS, group * tq, kv_chunk), MXU_DTYPE),
            pltpu.VMEM((KV_SLOTS, group * tq, LANES), F32),
        ],
        compiler_params=_cparams(("parallel", "parallel", "arbitrary"), 40),
        name="global_attention",
    )(qkv, qkv, qkv)


def _window_attn_kernel(sink_ref, q_ref, k_ref, v_ref, o_ref, *, t_lat, tq):
    s_tot = k_ref.shape[0]
    c_len = s_tot - t_lat
    band = tq + 2 * WINDOW
    q0 = pl.program_id(1) * tq
    is_ctx = q0 >= t_lat
    start = pl.multiple_of(jnp.clip(q0 - WINDOW, 0, t_lat - band), WINDOW)
    group = B_Q_HEADS // B_KV_HEADS

    delta = (lax.broadcasted_iota(jnp.int32, (tq, band), 1) + (start - q0)
             - lax.broadcasted_iota(jnp.int32, (tq, band), 0))
    reach = jnp.where(is_ctx, -1, WINDOW)
    valid = jnp.abs(delta) <= reach
    for hk in range(B_KV_HEADS):
        kv = slice(hk * HEAD_DIM, (hk + 1) * HEAD_DIM)
        q2 = _stack_heads(q_ref[:, hk * group * HEAD_DIM:(hk + 1) * group * HEAD_DIM])
        s_band = _dot_nt(q2, k_ref[pl.ds(start, band), kv])
        s_ctx = _dot_nt(q2, k_ref[pl.ds(t_lat, c_len), kv])
        p_band, p_ctx, denom = [], [], []
        for g in range(group):
            head = slice(g * tq, (g + 1) * tq)
            sb = jnp.where(valid, s_band[head, :], -jnp.inf)
            sc = s_ctx[head, :]
            sink = sink_ref[hk * group + g]
            m = jnp.maximum(jnp.maximum(jnp.max(sb, axis=-1, keepdims=True),
                                        jnp.max(sc, axis=-1, keepdims=True)), sink)
            pb = jnp.exp(sb - m)
            pc = jnp.exp(sc - m)
            denom.append(jnp.sum(pb, axis=-1, keepdims=True) + jnp.sum(pc, axis=-1, keepdims=True)
                         + jnp.exp(sink - m))
            p_band.append(pb.astype(MXU_DTYPE))
            p_ctx.append(pc.astype(MXU_DTYPE))
        o = (_dot(jnp.concatenate(p_band, axis=0), v_ref[pl.ds(start, band), kv])
             + _dot(jnp.concatenate(p_ctx, axis=0), v_ref[pl.ds(t_lat, c_len), kv]))
        o = o / jnp.concatenate(denom, axis=0)
        for g in range(group):
            cols = slice((hk * group + g) * HEAD_DIM, (hk * group + g + 1) * HEAD_DIM)
            o_ref[:, cols] = o[g * tq:(g + 1) * tq, :].astype(o_ref.dtype)


def _window_attention(qkv, sink, t_lat):
    b, s, _ = qkv.shape
    tq = ROW_TILE
    assert tq & (tq - 1) == 0 and t_lat >= tq + 2 * WINDOW
    q0 = (A_QW + 2 * A_KVW) // B_QW
    k0 = (A_QW + 2 * A_KVW + B_QW) // B_KVW
    v0 = k0 + 1
    return pl.pallas_call(
        functools.partial(_window_attn_kernel, t_lat=t_lat, tq=tq),
        grid=(b, s // tq),
        in_specs=[
            pl.BlockSpec(memory_space=pltpu.SMEM),
            pl.BlockSpec((None, tq, B_QW), lambda bi, i: (bi, i, q0)),
            pl.BlockSpec((None, s, B_KVW), lambda bi, i: (bi, 0, k0)),
            pl.BlockSpec((None, s, B_KVW), lambda bi, i: (bi, 0, v0)),
        ],
        out_specs=pl.BlockSpec((None, tq, B_QW), lambda bi, i: (bi, i, 0)),
        out_shape=jax.ShapeDtypeStruct((b, s, B_QW), MXU_DTYPE),
        compiler_params=_cparams(("parallel", "arbitrary"), 40),
        name="window_attention",
    )(sink, qkv, qkv, qkv)


def _attn_out_kernel(oa_ref, ob_ref, h_ref, mod_ref, w_ref, lng_ref, lnb_ref, out_ref, *, alpha):
    y = _dot(oa_ref[...], w_ref[0:A_QW, :]) + _dot(ob_ref[...], w_ref[A_QW:A_QW + B_QW, :])
    out_ref[...] = _residual_layer_norm(h_ref[...], y, mod_ref[2:3, :], lng_ref[...], lnb_ref[...], alpha)


def _attn_out(o_a, o_b, h, mods, layer, w_out, mixer, ln_g, ln_b, t_lat, n_rows, alpha):
    b, s, d = h.shape
    tm = ROW_TILE
    row = _mod_row(t_lat, tm)
    return pl.pallas_call(
        functools.partial(_attn_out_kernel, alpha=alpha),
        grid=(b, n_rows // tm),
        in_specs=[
            pl.BlockSpec((None, tm, A_QW), lambda bi, i: (bi, i, 0)),
            pl.BlockSpec((None, tm, B_QW), lambda bi, i: (bi, i, 0)),
            pl.BlockSpec((None, tm, d), lambda bi, i: (bi, i, 0)),
            pl.BlockSpec((None, None, 6, d), lambda bi, i: (layer, row(bi, i), 0, 0)),
            pl.BlockSpec((None, A_QW + B_QW, d), lambda bi, i: (mixer, 0, 0)),
            pl.BlockSpec((1, d), lambda bi, i: (0, 0)),
            pl.BlockSpec((1, d), lambda bi, i: (0, 0)),
        ],
        out_specs=pl.BlockSpec((None, tm, d), lambda bi, i: (bi, i, 0)),
        out_shape=jax.ShapeDtypeStruct((b, n_rows, d), F32),
        compiler_params=_cparams(("parallel", "parallel"), 40),
        name="attn_out_norm",
    )(o_a, o_b, h, mods, w_out, ln_g.reshape(1, d), ln_b.reshape(1, d))


def _log_sigmoid(x):
    return jnp.minimum(x, 0.0) - jnp.log(1.0 + jnp.exp(-jnp.abs(x)))


def _gla_proj_kernel(h_ref, mod_ref, w_ref, wa1_ref, wa2_ref, ba_ref, p_ref, ld_ref):
    a = (h_ref[...] * (1.0 + mod_ref[1:2, :]) + mod_ref[0:1, :]).astype(MXU_DTYPE)
    for cb in range(GLA_IN // MXU_N):
        z = _dot(a, w_ref[:, cb * MXU_N:(cb + 1) * MXU_N])
        if cb < GLA_KEY_DIM // MXU_N:
            z = z * (GLA_DK ** -0.5)
        p_ref[:, cb * MXU_N:(cb + 1) * MXU_N] = z.astype(p_ref.dtype)
    low = _dot(a, wa1_ref[...]).astype(MXU_DTYPE)
    for cb in range(2 * GLA_KEY_DIM // MXU_N):
        cols = slice(cb * MXU_N, (cb + 1) * MXU_N)
        logits = _dot(low, wa2_ref[:, cols]) + ba_ref[:, cols]
        ld_ref[:, cols] = _log_sigmoid(logits) * (LOG2_E / GLA_GATE_NORM)


def _gla_project(h, mods, layer, w_in, mixer, w_a1, w_a2, b_a, t_lat):
    b, s, d = h.shape
    tm = ROW_TILE
    row = _mod_row(t_lat, tm)
    return pl.pallas_call(
        _gla_proj_kernel,
        grid=(b, s // tm),
        in_specs=[
            pl.BlockSpec((None, tm, d), lambda bi, i: (bi, i, 0)),
            pl.BlockSpec((None, None, 6, d), lambda bi, i: (layer, row(bi, i), 0, 0)),
            pl.BlockSpec((None, d, GLA_IN), lambda bi, i: (mixer, 0, 0)),
            pl.BlockSpec((d, LANES), lambda bi, i: (0, 0)),
            pl.BlockSpec((LANES, 2 * GLA_KEY_DIM), lambda bi, i: (0, 0)),
            pl.BlockSpec((1, 2 * GLA_KEY_DIM), lambda bi, i: (0, 0)),
        ],
        out_specs=[
            pl.BlockSpec((None, tm, GLA_IN), lambda bi, i: (bi, i, 0)),
            pl.BlockSpec((None, tm, 2 * GLA_KEY_DIM), lambda bi, i: (bi, i, 0)),
        ],
        out_shape=[
            jax.ShapeDtypeStruct((b, s, GLA_IN), MXU_DTYPE),
            jax.ShapeDtypeStruct((b, s, 2 * GLA_KEY_DIM), F32),
        ],
        compiler_params=_cparams(("parallel", "parallel"), 48),
        name="gla_project",
    )(h, mods, w_in, w_a1, w_a2, b_a)


def _cumsum_rows(x):
    rows, cols = x.shape
    sub = lax.broadcasted_iota(jnp.int32, x.shape, 0) & (SUBLANES - 1)
    shift = 1
    while shift < SUBLANES:
        x = x + jnp.where(sub >= shift, pltpu.roll(x, shift, 0), 0.0)
        shift *= 2
    x3 = x.reshape(rows // SUBLANES, SUBLANES, cols)
    carry = jnp.zeros((1, cols), F32)
    tiles = []
    for i in range(rows // SUBLANES):
        tile = x3[i] + carry
        tiles.append(tile)
        carry = tile[SUBLANES - 1:SUBLANES, :]
    return jnp.concatenate(tiles, axis=0)


def _block_anchor(p, n):
    rows, cols = p.shape
    half = n // 2
    if n >= SUBLANES:
        p3 = p.reshape(rows // n, n, cols)
        return jnp.broadcast_to(p3[:, half - 1:half, :], p3.shape).reshape(rows, cols)
    if n == 2:
        odd = (lax.broadcasted_iota(jnp.int32, p.shape, 0) & 1) == 1
        return jnp.where(odd, pltpu.roll(p, 1, 0), p)
    p3 = p.reshape(rows // SUBLANES, SUBLANES, cols)
    sub = lax.broadcasted_iota(jnp.int32, p3.shape, 1)
    a = None
    for b0 in range(0, SUBLANES, n):
        row = jnp.broadcast_to(p3[:, b0 + half - 1:b0 + half, :], p3.shape)
        a = row if a is None else jnp.where(sub >= b0, row, a)
    return a.reshape(rows, cols)


def _gla_chunk(q, k, v, ld, sign_ref, st_ref, reverse):
    chunk = q.shape[0]
    p_inc = _cumsum_rows(ld)
    p_tot = p_inc[chunk - 1:chunk, :]
    px = p_inc - ld if reverse else p_inc
    qf = q.astype(F32)
    kf = k.astype(F32)

    def decayed(x, exponent):
        return (x * jnp.exp2(exponent)).astype(MXU_DTYPE)

    row = lax.broadcasted_iota(jnp.int32, (chunk, chunk), 0)
    col = lax.broadcasted_iota(jnp.int32, (chunk, chunk), 1)
    differ = row ^ col
    att = _dot_nt(q, k)
    n = 2
    while n <= chunk:
        level = n.bit_length() - 2
        exponent = (px - _block_anchor(p_inc, n)) * sign_ref[level]
        s_lvl = _dot_nt(decayed(qf, exponent), decayed(kf, exponent))
        att = jnp.where((differ >> level) == 1, s_lvl, att)
        n *= 2
    att = jnp.where(row <= col if reverse else row >= col, att, 0.0)

    if reverse:
        qd = decayed(qf, p_tot - px)
        kd = decayed(kf, px)
    else:
        qd = decayed(qf, px)
        kd = decayed(kf, p_tot - px)
    state = st_ref[...]
    o = _dot(att.astype(MXU_DTYPE), v) + _dot_nt(qd, state.astype(MXU_DTYPE))
    st_ref[...] = state * jnp.exp2(p_tot) + _dot_tn(v, kd)
    return o


def _gla_scan_kernel(sign_ref, qf_ref, kf_ref, vf_ref, ldf_ref, qr_ref, kr_ref, vr_ref, ldr_ref,
                     of_ref, or_ref, st_ref):
    @pl.when(pl.program_id(1) == 0)
    def _():
        st_ref[...] = jnp.zeros(st_ref.shape, F32)

    operands = ((qf_ref, kf_ref, vf_ref, ldf_ref, of_ref), (qr_ref, kr_ref, vr_ref, ldr_ref, or_ref))
    for direction, (q_ref, k_ref, v_ref, ld_ref, o_ref) in enumerate(operands):
        for hd in range(GLA_HEADS):
            kc = slice(hd * GLA_DK, (hd + 1) * GLA_DK)
            vc = slice(hd * GLA_DV, (hd + 1) * GLA_DV)
            o = _gla_chunk(q_ref[:, kc], k_ref[:, kc], v_ref[:, vc], ld_ref[:, kc], sign_ref,
                           st_ref.at[direction, hd], reverse=direction == 1)
            o_ref[:, vc] = o.astype(o_ref.dtype)


def _gla_scan(proj, ld, t_lat):
    b, s, _ = proj.shape
    ch = GLA_CHUNK
    n_chunks = s // ch
    lat_chunks = t_lat // ch

    def fwd(c):
        return (c + lat_chunks) % n_chunks

    def rev(c):
        return n_chunks - 1 - c

    def specs(chunk_of, direction):
        return [
            pl.BlockSpec((None, ch, GLA_KEY_DIM), lambda bi, c: (bi, chunk_of(c), 0)),
            pl.BlockSpec((None, ch, GLA_KEY_DIM), lambda bi, c: (bi, chunk_of(c), 1)),
            pl.BlockSpec((None, ch, GLA_VAL_DIM), lambda bi, c: (bi, chunk_of(c), 1)),
            pl.BlockSpec((None, ch, GLA_KEY_DIM), lambda bi, c: (bi, chunk_of(c), direction)),
        ]

    levels = ch.bit_length() - 1
    half_bit = 1 << jnp.arange(levels, dtype=jnp.int32)[:, None, None]
    in_upper = (jnp.arange(ch, dtype=jnp.int32)[None, :, None] & half_bit) != 0
    sign = jnp.broadcast_to(jnp.where(in_upper, 1.0, -1.0).astype(F32), (levels, ch, GLA_DK))
    o_shape = jax.ShapeDtypeStruct((b, s, GLA_VAL_DIM), F32)
    return pl.pallas_call(
        _gla_scan_kernel,
        grid=(b, n_chunks),
        in_specs=[pl.BlockSpec((levels, ch, GLA_DK), lambda bi, c: (0, 0, 0))] + specs(fwd, 0) + specs(rev, 1),
        out_specs=[
            pl.BlockSpec((None, ch, GLA_VAL_DIM), lambda bi, c: (bi, fwd(c), 0)),
            pl.BlockSpec((None, ch, GLA_VAL_DIM), lambda bi, c: (bi, rev(c), 0)),
        ],
        out_shape=[o_shape, o_shape],
        scratch_shapes=[pltpu.VMEM((2, GLA_HEADS, GLA_DV, GLA_DK), F32)],
        compiler_params=_cparams(("parallel", "arbitrary"), 32),
        name="gla_scan",
    )(sign, proj, proj, proj, ld, proj, proj, proj, ld)


def _gla_out_kernel(of_ref, ob_ref, g_ref, h_ref, mod_ref, hn_ref, w_ref, lng_ref, lnb_ref, out_ref, *, alpha):
    o = of_ref[...] + ob_ref[...]
    gate = _silu(g_ref[...].astype(F32))
    y = jnp.zeros(h_ref.shape, F32)
    for hd in range(GLA_HEADS):
        cols = slice(hd * GLA_DV, (hd + 1) * GLA_DV)
        oh = o[:, cols]
        oh = oh * lax.rsqrt(jnp.mean(oh * oh, axis=-1, keepdims=True) + EPS) * hn_ref[...]
        y = y + _dot((oh * gate[:, cols]).astype(MXU_DTYPE), w_ref[cols, :])
    out_ref[...] = _residual_layer_norm(h_ref[...], y, mod_ref[2:3, :], lng_ref[...], lnb_ref[...], alpha)


def _gla_out(o_f, o_b, proj, h, mods, layer, head_norm, w_out, mixer, ln_g, ln_b, t_lat, n_rows, alpha):
    b, s, d = h.shape
    tm = ROW_TILE
    row = _mod_row(t_lat, tm)
    g_blk = (2 * GLA_KEY_DIM + GLA_VAL_DIM) // GLA_VAL_DIM
    return pl.pallas_call(
        functools.partial(_gla_out_kernel, alpha=alpha),
        grid=(b, n_rows // tm),
        in_specs=[
            pl.BlockSpec((None, tm, GLA_VAL_DIM), lambda bi, i: (bi, i, 0)),
            pl.BlockSpec((None, tm, GLA_VAL_DIM), lambda bi, i: (bi, i, 0)),
            pl.BlockSpec((None, tm, GLA_VAL_DIM), lambda bi, i: (bi, i, g_blk)),
            pl.BlockSpec((None, tm, d), lambda bi, i: (bi, i, 0)),
            pl.BlockSpec((None, None, 6, d), lambda bi, i: (layer, row(bi, i), 0, 0)),
            pl.BlockSpec((1, GLA_DV), lambda bi, i: (0, 0)),
            pl.BlockSpec((None, GLA_VAL_DIM, d), lambda bi, i: (mixer, 0, 0)),
            pl.BlockSpec((1, d), lambda bi, i: (0, 0)),
            pl.BlockSpec((1, d), lambda bi, i: (0, 0)),
        ],
        out_specs=pl.BlockSpec((None, tm, d), lambda bi, i: (bi, i, 0)),
        out_shape=jax.ShapeDtypeStruct((b, n_rows, d), F32),
        compiler_params=_cparams(("parallel", "parallel"), 40),
        name="gla_out_norm",
    )(o_f, o_b, proj, h, mods, head_norm.reshape(1, GLA_DV), w_out, ln_g.reshape(1, d), ln_b.reshape(1, d))


def _ffn_kernel(h_ref, hp_ref, hn_ref, mod_ref, wup_ref, cw_ref, cb_ref, wdn_ref, lng_ref, lnb_ref,
                out_ref, a_scr, u_scr, act_scr, *, t_lat, s_tot, alpha):
    tm = h_ref.shape[0]
    d_ff = wdn_ref.shape[0]
    row0 = pl.program_id(1) * tm
    at_start = (row0 == 0) | (row0 == t_lat)
    at_end = (row0 + tm == t_lat) | (row0 + tm == s_tot)
    shift = mod_ref[3:4, :]
    scale = 1.0 + mod_ref[4:5, :]
    h = h_ref[...]
    a_scr[0:HALO, :] = jnp.where(at_start, 0.0, hp_ref[...] * scale + shift).astype(a_scr.dtype)
    a_scr[HALO:HALO + tm, :] = (h * scale + shift).astype(a_scr.dtype)
    a_scr[HALO + tm:2 * HALO + tm, :] = jnp.where(at_end, 0.0, hn_ref[...] * scale + shift).astype(a_scr.dtype)
    a = a_scr[...]

    def conv(u_ref, c0):
        cols = slice(c0, c0 + FF_CHUNK)
        acc = cb_ref[:, cols] + cw_ref[0:1, cols] * u_ref[pl.ds(HALO - 1, tm), :]
        for j in range(1, CONV_W):
            acc = acc + cw_ref[j:j + 1, cols] * u_ref[pl.ds(HALO - 1 + j, tm), :]
        return acc

    def up_project(c):
        g0 = c * FF_CHUNK
        v0 = d_ff + c * FF_CHUNK
        u_scr[c % U_SLOTS, 0] = _dot(a, wup_ref[:, g0:g0 + FF_CHUNK])
        u_scr[c % U_SLOTS, 1] = _dot(a, wup_ref[:, v0:v0 + FF_CHUNK])

    def down_project(c):
        return _dot(act_scr[c % ACT_SLOTS], wdn_ref[c * FF_CHUNK:(c + 1) * FF_CHUNK, :])

    n_chunks = d_ff // FF_CHUNK
    ahead = U_SLOTS - 1
    y = jnp.zeros(h.shape, F32)
    for c in range(min(ahead, n_chunks)):
        up_project(c)
    for c in range(n_chunks):
        if c + ahead < n_chunks:
            up_project(c + ahead)
        if c > 0:
            y = y + down_project(c - 1)
        g0 = c * FF_CHUNK
        act = _silu(conv(u_scr.at[c % U_SLOTS, 0], g0)) * conv(u_scr.at[c % U_SLOTS, 1], d_ff + g0)
        act_scr[c % ACT_SLOTS] = act.astype(act_scr.dtype)
    y = y + down_project(n_chunks - 1)
    out_ref[...] = _residual_layer_norm(h, y, mod_ref[5:6, :], lng_ref[...], lnb_ref[...], alpha)


def _conv_ffn(h, mods, layer, w_up, conv_w, conv_b, w_down, ln_g, ln_b, t_lat, n_rows, alpha):
    b, s, d = h.shape
    tm = ROW_TILE
    d_ff = w_down.shape[1]
    row = _mod_row(t_lat, tm)
    hb = tm // HALO
    n_halo = s // HALO
    return pl.pallas_call(
        functools.partial(_ffn_kernel, t_lat=t_lat, s_tot=s, alpha=alpha),
        grid=(b, n_rows // tm),
        in_specs=[
            pl.BlockSpec((None, tm, d), lambda bi, i: (bi, i, 0)),
            pl.BlockSpec((None, HALO, d), lambda bi, i: (bi, jnp.maximum(i * hb - 1, 0), 0)),
            pl.BlockSpec((None, HALO, d), lambda bi, i: (bi, jnp.minimum((i + 1) * hb, n_halo - 1), 0)),
            pl.BlockSpec((None, None, 6, d), lambda bi, i: (layer, row(bi, i), 0, 0)),
            pl.BlockSpec((None, d, 2 * d_ff), lambda bi, i: (layer, 0, 0)),
            pl.BlockSpec((None, CONV_W, 2 * d_ff), lambda bi, i: (layer, 0, 0)),
            pl.BlockSpec((None, 1, 2 * d_ff), lambda bi, i: (layer, 0, 0)),
            pl.BlockSpec((None, d_ff, d), lambda bi, i: (layer, 0, 0)),
            pl.BlockSpec((1, d), lambda bi, i: (0, 0)),
            pl.BlockSpec((1, d), lambda bi, i: (0, 0)),
        ],
        out_specs=pl.BlockSpec((None, tm, d), lambda bi, i: (bi, i, 0)),
        out_shape=jax.ShapeDtypeStruct((b, n_rows, d), F32),
        scratch_shapes=[
            pltpu.VMEM((tm + 2 * HALO, d), MXU_DTYPE),
            pltpu.VMEM((U_SLOTS, 2, tm + 2 * HALO, FF_CHUNK), F32),
            pltpu.VMEM((ACT_SLOTS, tm, FF_CHUNK), MXU_DTYPE),
        ],
        compiler_params=_cparams(("parallel", "parallel"), 56),
        name="conv_ffn_norm",
    )(h, h, h, mods, w_up, conv_w, conv_b.reshape(-1, 1, 2 * d_ff), w_down, ln_g.reshape(1, d), ln_b.reshape(1, d))


def _rope_tables(t_lat, c_len):
    rows = t_lat // GRID_W
    row = jnp.repeat(jnp.arange(rows, dtype=F32), GRID_W)
    col = jnp.tile(jnp.arange(GRID_W, dtype=F32), rows)
    inv_freq = jnp.power(ROPE_THETA, -jnp.arange(ROPE_AXIS_DIM // 2, dtype=F32) * 2.0 / ROPE_AXIS_DIM)
    ang = jnp.concatenate([row[:, None] * inv_freq, col[:, None] * inv_freq], axis=-1)
    ang = jnp.concatenate([ang, jnp.zeros((c_len, HEAD_DIM // 2), F32)], axis=0)
    cos, sin = jnp.cos(ang), jnp.sin(ang)
    return jnp.concatenate([cos, cos], axis=-1), jnp.concatenate([-sin, sin], axis=-1)


def kernel(x, c, ctx, c_ctx, ada_w, ada_b, ln_g, ln_b, ffn_w_up, ffn_conv_w, ffn_conv_b, ffn_w_down, attn_w_in, attn_q_norm, attn_k_norm, attn_sink, attn_w_out, gla_w_in, gla_w_a1, gla_w_a2, gla_b_a, gla_head_norm, gla_w_out):
    b, t_lat, d = x.shape
    c_len = ctx.shape[1]
    s = t_lat + c_len
    depth = ada_w.shape[0]
    assert b < MOD_CTX_ROW and t_lat % ROW_TILE == 0 and c_len % ROW_TILE == 0
    assert t_lat % GLA_CHUNK == 0 and c_len % GLA_CHUNK == 0 and s % MXU_N == 0
    alpha = (2 * depth) ** 0.25

    cond = jnp.zeros((MOD_ROWS, d), F32).at[:b].set(c).at[MOD_CTX_ROW].set(c_ctx)
    mods = _modulation(cond, ada_w, ada_b).reshape(depth, MOD_ROWS, 6, d)
    cos, sin = _rope_tables(t_lat, c_len)
    h = jnp.concatenate([x, ctx], axis=1)
    attn_w_in_c, attn_w_out_c = attn_w_in.astype(MXU_DTYPE), attn_w_out.astype(MXU_DTYPE)
    gla_w_in_c, gla_w_out_c = gla_w_in.astype(MXU_DTYPE), gla_w_out.astype(MXU_DTYPE)
    ffn_w_up_c, ffn_w_down_c = ffn_w_up.astype(MXU_DTYPE), ffn_w_down.astype(MXU_DTYPE)

    for i in range(depth):
        last = i == depth - 1
        n_rows = t_lat if last else s
        j = i // 2
        if i % 2 == 0:
            qkv = _attn_project(h, mods, i, attn_w_in_c, j, attn_q_norm[j], attn_k_norm[j],
                                cos, sin, t_lat)
            o_a = _global_attention(qkv, t_lat)
            o_b = _window_attention(qkv, attn_sink[j], t_lat)
            h1 = _attn_out(o_a, o_b, h, mods, i, attn_w_out_c, j, ln_g[i, 0], ln_b[i, 0],
                           t_lat, n_rows, alpha)
        else:
            w_a1 = jnp.concatenate([gla_w_a1[j, 0], gla_w_a1[j, 1]], axis=1)
            w_a1 = jnp.pad(w_a1, ((0, 0), (0, LANES - 2 * GLA_GATE_RANK))).astype(MXU_DTYPE)
            w_a2 = jnp.zeros((LANES, 2 * GLA_KEY_DIM), F32)
            w_a2 = w_a2.at[:GLA_GATE_RANK, :GLA_KEY_DIM].set(gla_w_a2[j, 0])
            w_a2 = w_a2.at[GLA_GATE_RANK:2 * GLA_GATE_RANK, GLA_KEY_DIM:].set(gla_w_a2[j, 1]).astype(MXU_DTYPE)
            proj, ld = _gla_project(h, mods, i, gla_w_in_c, j, w_a1, w_a2,
                                    gla_b_a[j].reshape(1, 2 * GLA_KEY_DIM), t_lat)
            o_f, o_r = _gla_scan(proj, ld, t_lat)
            h1 = _gla_out(o_f, o_r, proj, h, mods, i, gla_head_norm[j], gla_w_out_c, j,
                          ln_g[i, 0], ln_b[i, 0], t_lat, n_rows, alpha)
        h = _conv_ffn(h1, mods, i, ffn_w_up_c, ffn_conv_w, ffn_conv_b, ffn_w_down_c, ln_g[i, 1], ln_b[i, 1],
                      t_lat, n_rows, alpha)
    return h
```

```python
import functools

import jax
import jax.numpy as jnp
from jax import lax
from jax.experimental import pallas as pl
from jax.experimental.pallas import tpu as pltpu

GRID_W = 64
HEAD_DIM = 128
A_Q_HEADS = 4
A_KV_HEADS = 2
B_Q_HEADS = 4
B_KV_HEADS = 2
A_QW = A_Q_HEADS * HEAD_DIM
A_KVW = A_KV_HEADS * HEAD_DIM
B_QW = B_Q_HEADS * HEAD_DIM
B_KVW = B_KV_HEADS * HEAD_DIM
ATTN_IN = A_QW + 2 * A_KVW + B_QW + 2 * B_KVW
ATTN_SCALE = HEAD_DIM ** -0.5
WINDOW = 128
ROPE_THETA = 10000.0
ROPE_AXIS_DIM = HEAD_DIM // 2
GLA_HEADS = 4
GLA_DK = 128
GLA_DV = 256
GLA_KEY_DIM = GLA_HEADS * GLA_DK
GLA_VAL_DIM = GLA_HEADS * GLA_DV
GLA_IN = 2 * GLA_KEY_DIM + 2 * GLA_VAL_DIM
GLA_GATE_RANK = 16
GLA_GATE_NORM = 16.0
CONV_W = 3
EPS = 1e-6

LANES = 128
SUBLANES = 8
MXU_N = 256
VMEM_BYTES = 64 * 2 ** 20

MXU_DTYPE = jnp.bfloat16
F32 = jnp.float32

ROW_TILE = 256
HALO = 16
FF_CHUNK = 256
U_SLOTS = 3
ACT_SLOTS = 2
KV_CHUNK = 768
KV_SLOTS = 2
SOFTMAX_ROWS = 16
LOG2_E = 1.4426950408889634
GLA_CHUNK = 128
MOD_COLS = 1536


def _cparams(sem, vmem_mib):
    return pltpu.CompilerParams(dimension_semantics=sem, vmem_limit_bytes=vmem_mib * 2 ** 20)


def _dot(a, b):
    return jnp.dot(a, b, preferred_element_type=F32)


def _dot_nt(a, b):
    return lax.dot_general(a, b, (((1,), (1,)), ((), ())), preferred_element_type=F32)


def _dot_tn(a, b):
    return lax.dot_general(a, b, (((0,), (0,)), ((), ())), preferred_element_type=F32)


def _silu(x):
    return x / (1.0 + jnp.exp(-x))


def _residual_layer_norm(h, y, gate, ln_g, ln_b, alpha):
    r = alpha * h + gate * y
    mu = jnp.mean(r, axis=-1, keepdims=True)
    d = r - mu
    var = jnp.mean(d * d, axis=-1, keepdims=True)
    return d * lax.rsqrt(var + EPS) * ln_g + ln_b


def _mod_row(t_lat, tm):
    n_lat = t_lat // tm

    def row(b, i):
        return jnp.where(i < n_lat, b, MOD_CTX_ROW)
    return row


MOD_ROWS = 8
MOD_CTX_ROW = MOD_ROWS - 1


def _mod_kernel(cond_ref, w_ref, b_ref, o_ref):
    a = _silu(cond_ref[...]).astype(MXU_DTYPE)
    o_ref[...] = _dot(a, w_ref[...].astype(MXU_DTYPE)) + b_ref[...]


def _modulation(cond, ada_w, ada_b):
    depth, d, n = ada_w.shape
    return pl.pallas_call(
        _mod_kernel,
        grid=(depth, n // MOD_COLS),
        in_specs=[
            pl.BlockSpec((MOD_ROWS, d), lambda l, j: (0, 0)),
            pl.BlockSpec((None, d, MOD_COLS), lambda l, j: (l, 0, j)),
            pl.BlockSpec((None, 1, MOD_COLS), lambda l, j: (l, 0, j)),
        ],
        out_specs=pl.BlockSpec((None, MOD_ROWS, MOD_COLS), lambda l, j: (l, 0, j)),
        out_shape=jax.ShapeDtypeStruct((depth, MOD_ROWS, n), F32),
        compiler_params=_cparams(("parallel", "parallel"), 32),
        name="ada_modulation",
    )(cond, ada_w, ada_b.reshape(depth, 1, n))


def _attn_proj_kernel(h_ref, mod_ref, w_ref, qn_ref, kn_ref, cos_ref, sin_ref, o_ref):
    a = (h_ref[...] * (1.0 + mod_ref[1:2, :]) + mod_ref[0:1, :]).astype(MXU_DTYPE)
    cos = cos_ref[...]
    sin = sin_ref[...]

    def rms(z, g):
        return z * lax.rsqrt(jnp.mean(z * z, axis=-1, keepdims=True) + EPS) * g

    def rope(z):
        return z * cos + pltpu.roll(z, HEAD_DIM // 2, 1) * sin

    a_k0 = A_QW // HEAD_DIM
    a_v0 = a_k0 + A_KV_HEADS
    b_q0 = a_v0 + A_KV_HEADS
    b_k0 = b_q0 + B_Q_HEADS
    b_v0 = b_k0 + B_KV_HEADS
    heads_per_dot = MXU_N // HEAD_DIM
    for cb in range(ATTN_IN // MXU_N):
        z2 = _dot(a, w_ref[:, cb * MXU_N:(cb + 1) * MXU_N])
        for half in range(heads_per_dot):
            hb = cb * heads_per_dot + half
            z = z2[:, half * HEAD_DIM:(half + 1) * HEAD_DIM]
            if hb < a_k0:
                z = rope(rms(z, qn_ref[...])) * (ATTN_SCALE * LOG2_E)
            elif hb < a_v0:
                z = rope(rms(z, kn_ref[...]))
            elif hb < b_q0:
                pass
            elif hb < b_k0:
                z = rope(z) * ATTN_SCALE
            elif hb < b_v0:
                z = rope(z)
            o_ref[:, hb * HEAD_DIM:(hb + 1) * HEAD_DIM] = z.astype(o_ref.dtype)


def _attn_project(h, mods, layer, w_in, mixer, q_norm, k_norm, cos, sin, t_lat):
    b, s, d = h.shape
    tm = ROW_TILE
    row = _mod_row(t_lat, tm)
    return pl.pallas_call(
        _attn_proj_kernel,
        grid=(b, s // tm),
        in_specs=[
            pl.BlockSpec((None, tm, d), lambda bi, i: (bi, i, 0)),
            pl.BlockSpec((None, None, 6, d), lambda bi, i: (layer, row(bi, i), 0, 0)),
            pl.BlockSpec((None, d, ATTN_IN), lambda bi, i: (mixer, 0, 0)),
            pl.BlockSpec((1, HEAD_DIM), lambda bi, i: (0, 0)),
            pl.BlockSpec((1, HEAD_DIM), lambda bi, i: (0, 0)),
            pl.BlockSpec((tm, HEAD_DIM), lambda bi, i: (i, 0)),
            pl.BlockSpec((tm, HEAD_DIM), lambda bi, i: (i, 0)),
        ],
        out_specs=pl.BlockSpec((None, tm, ATTN_IN), lambda bi, i: (bi, i, 0)),
        out_shape=jax.ShapeDtypeStruct((b, s, ATTN_IN), MXU_DTYPE),
        compiler_params=_cparams(("parallel", "parallel"), 40),
        name="attn_project",
    )(h, mods, w_in, q_norm.reshape(1, HEAD_DIM), k_norm.reshape(1, HEAD_DIM), cos, sin)


def _stack_heads(q):
    g = q.shape[1] // HEAD_DIM
    return jnp.concatenate([q[:, i * HEAD_DIM:(i + 1) * HEAD_DIM] for i in range(g)], axis=0)


def _store_heads(o_ref, o, tq):
    for i in range(o.shape[0] // tq):
        o_ref[:, i * HEAD_DIM:(i + 1) * HEAD_DIM] = o[i * tq:(i + 1) * tq, :].astype(o_ref.dtype)


def _softmax_chunk(s_ref, m_scr, alpha_ref, p_ref):
    rows, n_keys = s_ref.shape
    for g in range(rows // SOFTMAX_ROWS):
        rg = slice(g * SOFTMAX_ROWS, (g + 1) * SOFTMAX_ROWS)
        blocks = [s_ref[rg, j * LANES:(j + 1) * LANES] for j in range(n_keys // LANES)]
        mx = blocks[0]
        for blk in blocks[1:]:
            mx = jnp.maximum(mx, blk)
        m_prev = m_scr[rg, :]
        m_new = jnp.maximum(m_prev, jnp.max(mx, axis=-1, keepdims=True))
        alpha_ref[rg, :] = jnp.exp2(m_prev - m_new)
        for j, blk in enumerate(blocks):
            p_ref[rg, j * LANES:(j + 1) * LANES] = jnp.exp2(blk - m_new).astype(p_ref.dtype)
        m_scr[rg, :] = m_new


def _global_attn_kernel(q_ref, k_ref, v_ref, o_ref, vext_scr, m_scr, acc_scr, s_scr, p_scr, alpha_scr,
                        *, t_lat, tq, kv_chunk):
    s_tot = k_ref.shape[0]
    c_len = s_tot - t_lat
    is_ctx = pl.program_id(2) * tq >= t_lat

    @pl.when(pl.program_id(2) == 0)
    def _():
        ones_col = jnp.where(lax.broadcasted_iota(jnp.int32, (MXU_N, HEAD_DIM), 1) == 0, 1.0, 0.0)

        def fill(blk, carry):
            rows = pl.ds(pl.multiple_of(blk * MXU_N, MXU_N), MXU_N)
            vext_scr[rows, 0:HEAD_DIM] = v_ref[rows, :]
            vext_scr[rows, HEAD_DIM:2 * HEAD_DIM] = ones_col.astype(vext_scr.dtype)
            return carry

        lax.fori_loop(0, s_tot // MXU_N, fill, 0)

    q2 = _stack_heads(q_ref[...])
    m_scr[...] = jnp.full(m_scr.shape, -jnp.inf, F32)
    acc_scr[...] = jnp.zeros(acc_scr.shape, F32)

    def keys(c):
        return slice(c * kv_chunk, (c + 1) * kv_chunk)

    def accumulate(p, v_ext, alpha):
        acc_scr[...] = jnp.concatenate([alpha, alpha], axis=1) * acc_scr[...] + _dot(p, v_ext)

    @pl.when(jnp.logical_not(is_ctx))
    def _():
        n_chunks = s_tot // kv_chunk
        s_scr[0] = _dot_nt(q2, k_ref[keys(0), :])
        for c in range(n_chunks):
            slot = c % KV_SLOTS
            if c + 1 < n_chunks:
                s_scr[(c + 1) % KV_SLOTS] = _dot_nt(q2, k_ref[keys(c + 1), :])
            if c > 0:
                prev = (c - 1) % KV_SLOTS
                accumulate(p_scr[prev], vext_scr[keys(c - 1), :], alpha_scr[prev])
            _softmax_chunk(s_scr.at[slot], m_scr, alpha_scr.at[slot], p_scr.at[slot])
        last = (n_chunks - 1) % KV_SLOTS
        accumulate(p_scr[last], vext_scr[keys(n_chunks - 1), :], alpha_scr[last])

    @pl.when(is_ctx)
    def _():
        s_scr[0, :, 0:c_len] = _dot_nt(q2, k_ref[t_lat:s_tot, :])
        _softmax_chunk(s_scr.at[0, :, 0:c_len], m_scr, alpha_scr.at[0], p_scr.at[0, :, 0:c_len])
        accumulate(p_scr[0, :, 0:c_len], vext_scr[t_lat:s_tot, :], alpha_scr[0])

    acc = acc_scr[...]
    _store_heads(o_ref, acc[:, 0:HEAD_DIM] / acc[:, HEAD_DIM:HEAD_DIM + 1], tq)


def _kv_chunk(s_tot):
    return max(n for n in range(MXU_N, KV_CHUNK + 1, MXU_N) if s_tot % n == 0)


def _global_attention(qkv, t_lat):
    b, s, _ = qkv.shape
    tq = ROW_TILE
    group = A_Q_HEADS // A_KV_HEADS
    k0 = A_QW // HEAD_DIM
    v0 = k0 + A_KV_HEADS
    kv_chunk = _kv_chunk(s)
    assert s - t_lat <= kv_chunk
    return pl.pallas_call(
        functools.partial(_global_attn_kernel, t_lat=t_lat, tq=tq, kv_chunk=kv_chunk),
        grid=(b, A_KV_HEADS, s // tq),
        in_specs=[
            pl.BlockSpec((None, tq, group * HEAD_DIM), lambda bi, hk, i: (bi, i, hk)),
            pl.BlockSpec((None, s, HEAD_DIM), lambda bi, hk, i: (bi, 0, k0 + hk)),
            pl.BlockSpec((None, s, HEAD_DIM), lambda bi, hk, i: (bi, 0, v0 + hk)),
        ],
        out_specs=pl.BlockSpec((None, tq, group * HEAD_DIM), lambda bi, hk, i: (bi, i, hk)),
        out_shape=jax.ShapeDtypeStruct((b, s, A_QW), MXU_DTYPE),
        scratch_shapes=[
            pltpu.VMEM((s, 2 * HEAD_DIM), MXU_DTYPE),
            pltpu.VMEM((group * tq, LANES), F32),
            pltpu.VMEM((group * tq, 2 * HEAD_DIM), F32),
            pltpu.VMEM((KV_SLOTS, group * tq, kv_chunk), F32),
            pltpu.VMEM((KV_SLOTS, group * tq, kv_chunk), MXU_DTYPE),
            pltpu.VMEM((KV_SLOTS, group * tq, LANES), F32),
        ],
        compiler_params=_cparams(("parallel", "parallel", "arbitrary"), 40),
        name="global_attention",
    )(qkv, qkv, qkv)


def _window_attn_kernel(sink_ref, q_ref, k_ref, v_ref, o_ref, *, t_lat, tq):
    s_tot = k_ref.shape[0]
    c_len = s_tot - t_lat
    band = tq + 2 * WINDOW
    q0 = pl.program_id(1) * tq
    is_ctx = q0 >= t_lat
    start = pl.multiple_of(jnp.clip(q0 - WINDOW, 0, t_lat - band), WINDOW)
    group = B_Q_HEADS // B_KV_HEADS

    delta = (lax.broadcasted_iota(jnp.int32, (tq, band), 1) + (start - q0)
             - lax.broadcasted_iota(jnp.int32, (tq, band), 0))
    reach = jnp.where(is_ctx, -1, WINDOW)
    valid = jnp.abs(delta) <= reach
    for hk in range(B_KV_HEADS):
        kv = slice(hk * HEAD_DIM, (hk + 1) * HEAD_DIM)
        q2 = _stack_heads(q_ref[:, hk * group * HEAD_DIM:(hk + 1) * group * HEAD_DIM])
        s_band = _dot_nt(q2, k_ref[pl.ds(start, band), kv])
        s_ctx = _dot_nt(q2, k_ref[pl.ds(t_lat, c_len), kv])
        p_band, p_ctx, denom = [], [], []
        for g in range(group):
            head = slice(g * tq, (g + 1) * tq)
            sb = jnp.where(valid, s_band[head, :], -jnp.inf)
            sc = s_ctx[head, :]
            sink = sink_ref[hk * group + g]
            m = jnp.maximum(jnp.maximum(jnp.max(sb, axis=-1, keepdims=True),
                                        jnp.max(sc, axis=-1, keepdims=True)), sink)
            pb = jnp.exp(sb - m)
            pc = jnp.exp(sc - m)
            denom.append(jnp.sum(pb, axis=-1, keepdims=True) + jnp.sum(pc, axis=-1, keepdims=True)
                         + jnp.exp(sink - m))
            p_band.append(pb.astype(MXU_DTYPE))
            p_ctx.append(pc.astype(MXU_DTYPE))
        o = (_dot(jnp.concatenate(p_band, axis=0), v_ref[pl.ds(start, band), kv])
             + _dot(jnp.concatenate(p_ctx, axis=0), v_ref[pl.ds(t_lat, c_len), kv]))
        o = o / jnp.concatenate(denom, axis=0)
        for g in range(group):
            cols = slice((hk * group + g) * HEAD_DIM, (hk * group + g + 1) * HEAD_DIM)
            o_ref[:, cols] = o[g * tq:(g + 1) * tq, :].astype(o_ref.dtype)


def _window_attention(qkv, sink, t_lat):
    b, s, _ = qkv.shape
    tq = ROW_TILE
    assert tq & (tq - 1) == 0 and t_lat >= tq + 2 * WINDOW
    q0 = (A_QW + 2 * A_KVW) // B_QW
    k0 = (A_QW + 2 * A_KVW + B_QW) // B_KVW
    v0 = k0 + 1
    return pl.pallas_call(
        functools.partial(_window_attn_kernel, t_lat=t_lat, tq=tq),
        grid=(b, s // tq),
        in_specs=[
            pl.BlockSpec(memory_space=pltpu.SMEM),
            pl.BlockSpec((None, tq, B_QW), lambda bi, i: (bi, i, q0)),
            pl.BlockSpec((None, s, B_KVW), lambda bi, i: (bi, 0, k0)),
            pl.BlockSpec((None, s, B_KVW), lambda bi, i: (bi, 0, v0)),
        ],
        out_specs=pl.BlockSpec((None, tq, B_QW), lambda bi, i: (bi, i, 0)),
        out_shape=jax.ShapeDtypeStruct((b, s, B_QW), MXU_DTYPE),
        compiler_params=_cparams(("parallel", "arbitrary"), 40),
        name="window_attention",
    )(sink, qkv, qkv, qkv)


def _log_sigmoid(x):
    return jnp.minimum(x, 0.0) - jnp.log(1.0 + jnp.exp(-jnp.abs(x)))


def _gla_proj_kernel(h_ref, mod_ref, w_ref, wa1_ref, wa2_ref, ba_ref, p_ref, ld_ref):
    a = (h_ref[...] * (1.0 + mod_ref[1:2, :]) + mod_ref[0:1, :]).astype(MXU_DTYPE)
    for cb in range(GLA_IN // MXU_N):
        z = _dot(a, w_ref[:, cb * MXU_N:(cb + 1) * MXU_N])
        if cb < GLA_KEY_DIM // MXU_N:
            z = z * (GLA_DK ** -0.5)
        p_ref[:, cb * MXU_N:(cb + 1) * MXU_N] = z.astype(p_ref.dtype)
    low = _dot(a, wa1_ref[...]).astype(MXU_DTYPE)
    for cb in range(2 * GLA_KEY_DIM // MXU_N):
        cols = slice(cb * MXU_N, (cb + 1) * MXU_N)
        logits = _dot(low, wa2_ref[:, cols]) + ba_ref[:, cols]
        ld_ref[:, cols] = _log_sigmoid(logits) * (LOG2_E / GLA_GATE_NORM)


def _gla_project(h, mods, layer, w_in, mixer, w_a1, w_a2, b_a, t_lat):
    b, s, d = h.shape
    tm = ROW_TILE
    row = _mod_row(t_lat, tm)
    return pl.pallas_call(
        _gla_proj_kernel,
        grid=(b, s // tm),
        in_specs=[
            pl.BlockSpec((None, tm, d), lambda bi, i: (bi, i, 0)),
            pl.BlockSpec((None, None, 6, d), lambda bi, i: (layer, row(bi, i), 0, 0)),
            pl.BlockSpec((None, d, GLA_IN), lambda bi, i: (mixer, 0, 0)),
            pl.BlockSpec((d, LANES), lambda bi, i: (0, 0)),
            pl.BlockSpec((LANES, 2 * GLA_KEY_DIM), lambda bi, i: (0, 0)),
            pl.BlockSpec((1, 2 * GLA_KEY_DIM), lambda bi, i: (0, 0)),
        ],
        out_specs=[
            pl.BlockSpec((None, tm, GLA_IN), lambda bi, i: (bi, i, 0)),
            pl.BlockSpec((None, tm, 2 * GLA_KEY_DIM), lambda bi, i: (bi, i, 0)),
        ],
        out_shape=[
            jax.ShapeDtypeStruct((b, s, GLA_IN), MXU_DTYPE),
            jax.ShapeDtypeStruct((b, s, 2 * GLA_KEY_DIM), F32),
        ],
        compiler_params=_cparams(("parallel", "parallel"), 48),
        name="gla_project",
    )(h, mods, w_in, w_a1, w_a2, b_a)


def _cumsum_rows(x):
    rows, cols = x.shape
    sub = lax.broadcasted_iota(jnp.int32, x.shape, 0) & (SUBLANES - 1)
    shift = 1
    while shift < SUBLANES:
        x = x + jnp.where(sub >= shift, pltpu.roll(x, shift, 0), 0.0)
        shift *= 2
    x3 = x.reshape(rows // SUBLANES, SUBLANES, cols)
    carry = jnp.zeros((1, cols), F32)
    tiles = []
    for i in range(rows // SUBLANES):
        tile = x3[i] + carry
        tiles.append(tile)
        carry = tile[SUBLANES - 1:SUBLANES, :]
    return jnp.concatenate(tiles, axis=0)


def _block_anchor(p, n):
    rows, cols = p.shape
    half = n // 2
    if n >= SUBLANES:
        p3 = p.reshape(rows // n, n, cols)
        return jnp.broadcast_to(p3[:, half - 1:half, :], p3.shape).reshape(rows, cols)
    if n == 2:
        odd = (lax.broadcasted_iota(jnp.int32, p.shape, 0) & 1) == 1
        return jnp.where(odd, pltpu.roll(p, 1, 0), p)
    p3 = p.reshape(rows // SUBLANES, SUBLANES, cols)
    sub = lax.broadcasted_iota(jnp.int32, p3.shape, 1)
    a = None
    for b0 in range(0, SUBLANES, n):
        row = jnp.broadcast_to(p3[:, b0 + half - 1:b0 + half, :], p3.shape)
        a = row if a is None else jnp.where(sub >= b0, row, a)
    return a.reshape(rows, cols)


def _gla_chunk(q, k, v, ld, sign_ref, st_ref, reverse):
    chunk = q.shape[0]
    p_inc = _cumsum_rows(ld)
    p_tot = p_inc[chunk - 1:chunk, :]
    px = p_inc - ld if reverse else p_inc
    qf = q.astype(F32)
    kf = k.astype(F32)

    def decayed(x, exponent):
        return (x * jnp.exp2(exponent)).astype(MXU_DTYPE)

    row = lax.broadcasted_iota(jnp.int32, (chunk, chunk), 0)
    col = lax.broadcasted_iota(jnp.int32, (chunk, chunk), 1)
    differ = row ^ col
    att = jnp.broadcast_to(jnp.sum(qf * kf, axis=-1, keepdims=True), (chunk, chunk))
    n = 2
    while n <= chunk:
        level = n.bit_length() - 2
        exponent = (px - _block_anchor(p_inc, n)) * sign_ref[level]
        s_lvl = _dot_nt(decayed(qf, exponent), decayed(kf, exponent))
        att = jnp.where((differ >> level) == 1, s_lvl, att)
        n *= 2
    att = jnp.where(row <= col if reverse else row >= col, att, 0.0)

    if reverse:
        qd = decayed(qf, p_tot - px)
        kd = decayed(kf, px)
    else:
        qd = decayed(qf, px)
        kd = decayed(kf, p_tot - px)
    state = st_ref[...]
    o = _dot(att.astype(MXU_DTYPE), v) + _dot_nt(qd, state.astype(MXU_DTYPE))
    st_ref[...] = state * jnp.exp2(p_tot) + _dot_tn(v, kd)
    return o


def _gla_scan_kernel(sign_ref, qf_ref, kf_ref, vf_ref, ldf_ref, qr_ref, kr_ref, vr_ref, ldr_ref,
                     of_ref, or_ref, st_ref):
    @pl.when(pl.program_id(1) == 0)
    def _():
        st_ref[...] = jnp.zeros(st_ref.shape, F32)

    operands = ((qf_ref, kf_ref, vf_ref, ldf_ref, of_ref), (qr_ref, kr_ref, vr_ref, ldr_ref, or_ref))
    for direction, (q_ref, k_ref, v_ref, ld_ref, o_ref) in enumerate(operands):
        for hd in range(GLA_HEADS):
            kc = slice(hd * GLA_DK, (hd + 1) * GLA_DK)
            vc = slice(hd * GLA_DV, (hd + 1) * GLA_DV)
            o = _gla_chunk(q_ref[:, kc], k_ref[:, kc], v_ref[:, vc], ld_ref[:, kc], sign_ref,
                           st_ref.at[direction, hd], reverse=direction == 1)
            o_ref[:, vc] = o.astype(o_ref.dtype)


def _gla_scan(proj, ld, t_lat):
    b, s, _ = proj.shape
    ch = GLA_CHUNK
    n_chunks = s // ch
    lat_chunks = t_lat // ch

    def fwd(c):
        return (c + lat_chunks) % n_chunks

    def rev(c):
        return n_chunks - 1 - c

    def specs(chunk_of, direction):
        return [
            pl.BlockSpec((None, ch, GLA_KEY_DIM), lambda bi, c: (bi, chunk_of(c), 0)),
            pl.BlockSpec((None, ch, GLA_KEY_DIM), lambda bi, c: (bi, chunk_of(c), 1)),
            pl.BlockSpec((None, ch, GLA_VAL_DIM), lambda bi, c: (bi, chunk_of(c), 1)),
            pl.BlockSpec((None, ch, GLA_KEY_DIM), lambda bi, c: (bi, chunk_of(c), direction)),
        ]

    levels = ch.bit_length() - 1
    half_bit = 1 << jnp.arange(levels, dtype=jnp.int32)[:, None, None]
    in_upper = (jnp.arange(ch, dtype=jnp.int32)[None, :, None] & half_bit) != 0
    sign = jnp.broadcast_to(jnp.where(in_upper, 1.0, -1.0).astype(F32), (levels, ch, GLA_DK))
    o_shape = jax.ShapeDtypeStruct((b, s, GLA_VAL_DIM), F32)
    return pl.pallas_call(
        _gla_scan_kernel,
        grid=(b, n_chunks),
        in_specs=[pl.BlockSpec((levels, ch, GLA_DK), lambda bi, c: (0, 0, 0))] + specs(fwd, 0) + specs(rev, 1),
        out_specs=[
            pl.BlockSpec((None, ch, GLA_VAL_DIM), lambda bi, c: (bi, fwd(c), 0)),
            pl.BlockSpec((None, ch, GLA_VAL_DIM), lambda bi, c: (bi, rev(c), 0)),
        ],
        out_shape=[o_shape, o_shape],
        scratch_shapes=[pltpu.VMEM((2, GLA_HEADS, GLA_DV, GLA_DK), F32)],
        compiler_params=_cparams(("parallel", "arbitrary"), 32),
        name="gla_scan",
    )(sign, proj, proj, proj, ld, proj, proj, proj, ld)


def _with_halo(prev_ref, ref, next_ref):
    return jnp.concatenate([prev_ref[...], ref[...], next_ref[...]], axis=0)


def _mixer_out_norm(x, h_refs, mod_ref, wout_ref, lng_ref, lnb_ref, h1_scr, alpha):
    y = _dot(x, wout_ref[...])
    h1_scr[...] = _residual_layer_norm(_with_halo(*h_refs), y, mod_ref[2:3, :], lng_ref[...], lnb_ref[...], alpha)


def _attn_ffn_kernel(oap_ref, oa_ref, oan_ref, obp_ref, ob_ref, obn_ref, hp_ref, h_ref, hn_ref, mod_ref,
                     wout_ref, ln1g_ref, ln1b_ref, wup_ref, cw_ref, cb_ref, wdn_ref, ln2g_ref, ln2b_ref,
                     out_ref, h1_scr, a_scr, u_scr, act_scr, *, t_lat, s_tot, alpha):
    x = jnp.concatenate([_with_halo(oap_ref, oa_ref, oan_ref), _with_halo(obp_ref, ob_ref, obn_ref)], axis=1)
    _mixer_out_norm(x, (hp_ref, h_ref, hn_ref), mod_ref, wout_ref, ln1g_ref, ln1b_ref, h1_scr, alpha)
    _conv_ffn_norm(h1_scr, mod_ref, wup_ref, cw_ref, cb_ref, wdn_ref, ln2g_ref, ln2b_ref, out_ref,
                   a_scr, u_scr, act_scr, t_lat=t_lat, s_tot=s_tot, alpha=alpha)


def _gla_ffn_kernel(ofp_ref, of_ref, ofn_ref, orp_ref, or_ref, orn_ref, gp_ref, g_ref, gn_ref,
                    hp_ref, h_ref, hn_ref, mod_ref, hnorm_ref, wout_ref, ln1g_ref, ln1b_ref,
                    wup_ref, cw_ref, cb_ref, wdn_ref, ln2g_ref, ln2b_ref,
                    out_ref, h1_scr, a_scr, u_scr, act_scr, *, t_lat, s_tot, alpha):
    o = _with_halo(ofp_ref, of_ref, ofn_ref) + _with_halo(orp_ref, or_ref, orn_ref)
    gate = _silu(_with_halo(gp_ref, g_ref, gn_ref).astype(F32))
    heads = []
    for hd in range(GLA_HEADS):
        cols = slice(hd * GLA_DV, (hd + 1) * GLA_DV)
        oh = o[:, cols]
        oh = oh * lax.rsqrt(jnp.mean(oh * oh, axis=-1, keepdims=True) + EPS) * hnorm_ref[...]
        heads.append((oh * gate[:, cols]).astype(MXU_DTYPE))
    x = jnp.concatenate(heads, axis=1)
    _mixer_out_norm(x, (hp_ref, h_ref, hn_ref), mod_ref, wout_ref, ln1g_ref, ln1b_ref, h1_scr, alpha)
    _conv_ffn_norm(h1_scr, mod_ref, wup_ref, cw_ref, cb_ref, wdn_ref, ln2g_ref, ln2b_ref, out_ref,
                   a_scr, u_scr, act_scr, t_lat=t_lat, s_tot=s_tot, alpha=alpha)


def _conv_ffn_norm(h1_scr, mod_ref, wup_ref, cw_ref, cb_ref, wdn_ref, lng_ref, lnb_ref,
                   out_ref, a_scr, u_scr, act_scr, *, t_lat, s_tot, alpha):
    tm = out_ref.shape[0]
    d_ff = wdn_ref.shape[0]
    row0 = pl.program_id(1) * tm
    at_start = (row0 == 0) | (row0 == t_lat)
    at_end = (row0 + tm == t_lat) | (row0 + tm == s_tot)
    shift = mod_ref[3:4, :]
    scale = 1.0 + mod_ref[4:5, :]
    a_all = h1_scr[...] * scale + shift
    a_scr[0:HALO, :] = jnp.where(at_start, 0.0, a_all[0:HALO, :]).astype(a_scr.dtype)
    a_scr[HALO:HALO + tm, :] = a_all[HALO:HALO + tm, :].astype(a_scr.dtype)
    a_scr[HALO + tm:2 * HALO + tm, :] = jnp.where(at_end, 0.0, a_all[HALO + tm:2 * HALO + tm, :]).astype(a_scr.dtype)
    a = a_scr[...]

    def conv(u_ref, c0):
        cols = slice(c0, c0 + FF_CHUNK)
        acc = cb_ref[:, cols] + cw_ref[0:1, cols] * u_ref[pl.ds(HALO - 1, tm), :]
        for j in range(1, CONV_W):
            acc = acc + cw_ref[j:j + 1, cols] * u_ref[pl.ds(HALO - 1 + j, tm), :]
        return acc

    def up_project(c):
        g0 = c * FF_CHUNK
        v0 = d_ff + c * FF_CHUNK
        u_scr[c % U_SLOTS, 0] = _dot(a, wup_ref[:, g0:g0 + FF_CHUNK])
        u_scr[c % U_SLOTS, 1] = _dot(a, wup_ref[:, v0:v0 + FF_CHUNK])

    def down_project(c):
        return _dot(act_scr[c % ACT_SLOTS], wdn_ref[c * FF_CHUNK:(c + 1) * FF_CHUNK, :])

    n_chunks = d_ff // FF_CHUNK
    ahead = U_SLOTS - 1
    y = jnp.zeros(out_ref.shape, F32)
    for c in range(min(ahead, n_chunks)):
        up_project(c)
    for c in range(n_chunks):
        if c + ahead < n_chunks:
            up_project(c + ahead)
        if c > 0:
            y = y + down_project(c - 1)
        g0 = c * FF_CHUNK
        act = _silu(conv(u_scr.at[c % U_SLOTS, 0], g0)) * conv(u_scr.at[c % U_SLOTS, 1], d_ff + g0)
        act_scr[c % ACT_SLOTS] = act.astype(act_scr.dtype)
    y = y + down_project(n_chunks - 1)
    h1 = h1_scr[HALO:HALO + tm, :]
    out_ref[...] = _residual_layer_norm(h1, y, mod_ref[5:6, :], lng_ref[...], lnb_ref[...], alpha)


def _mixer_ffn(kernel_fn, name, mixer_ins, h, mods, layer, extra_params, w_out, mixer, ln_g, ln_b,
               w_up, conv_w, conv_b, w_down, t_lat, n_rows, alpha):
    b, s, d = h.shape
    tm = ROW_TILE
    d_ff = w_down.shape[1]
    row = _mod_row(t_lat, tm)
    hb = tm // HALO
    n_halo = s // HALO

    def halo_specs(cols, cb):
        return [
            pl.BlockSpec((None, HALO, cols), lambda bi, i: (bi, jnp.maximum(i * hb - 1, 0), cb)),
            pl.BlockSpec((None, tm, cols), lambda bi, i: (bi, i, cb)),
            pl.BlockSpec((None, HALO, cols), lambda bi, i: (bi, jnp.minimum((i + 1) * hb, n_halo - 1), cb)),
        ]

    row_ins = list(mixer_ins) + [(h, d, 0)]
    in_specs, operands = [], []
    for arr, cols, cb in row_ins:
        in_specs += halo_specs(cols, cb)
        operands += [arr, arr, arr]
    in_specs.append(pl.BlockSpec((None, None, 6, d), lambda bi, i: (layer, row(bi, i), 0, 0)))
    operands.append(mods)
    for p in extra_params:
        in_specs.append(pl.BlockSpec(p.shape, lambda bi, i: (0, 0)))
        operands.append(p)
    in_specs += [
        pl.BlockSpec((None,) + w_out.shape[1:], lambda bi, i: (mixer, 0, 0)),
        pl.BlockSpec((None, 1, d), lambda bi, i: (layer, 0, 0)),
        pl.BlockSpec((None, 1, d), lambda bi, i: (layer, 0, 0)),
        pl.BlockSpec((None, d, 2 * d_ff), lambda bi, i: (layer, 0, 0)),
        pl.BlockSpec((None, CONV_W, 2 * d_ff), lambda bi, i: (layer, 0, 0)),
        pl.BlockSpec((None, 1, 2 * d_ff), lambda bi, i: (layer, 0, 0)),
        pl.BlockSpec((None, d_ff, d), lambda bi, i: (layer, 0, 0)),
        pl.BlockSpec((None, 1, d), lambda bi, i: (layer, 0, 0)),
        pl.BlockSpec((None, 1, d), lambda bi, i: (layer, 0, 0)),
    ]
    operands += [w_out, ln_g[:, 0:1, :], ln_b[:, 0:1, :], w_up, conv_w, conv_b.reshape(-1, 1, 2 * d_ff), w_down,
                 ln_g[:, 1:2, :], ln_b[:, 1:2, :]]
    return pl.pallas_call(
        functools.partial(kernel_fn, t_lat=t_lat, s_tot=s, alpha=alpha),
        grid=(b, n_rows // tm),
        in_specs=in_specs,
        out_specs=pl.BlockSpec((None, tm, d), lambda bi, i: (bi, i, 0)),
        out_shape=jax.ShapeDtypeStruct((b, n_rows, d), F32),
        scratch_shapes=[
            pltpu.VMEM((tm + 2 * HALO, d), F32),
            pltpu.VMEM((tm + 2 * HALO, d), MXU_DTYPE),
            pltpu.VMEM((U_SLOTS, 2, tm + 2 * HALO, FF_CHUNK), F32),
            pltpu.VMEM((ACT_SLOTS, tm, FF_CHUNK), MXU_DTYPE),
        ],
        compiler_params=_cparams(("parallel", "parallel"), 60),
        name=name,
    )(*operands)


def _rope_tables(t_lat, c_len):
    rows = t_lat // GRID_W
    row = jnp.repeat(jnp.arange(rows, dtype=F32), GRID_W)
    col = jnp.tile(jnp.arange(GRID_W, dtype=F32), rows)
    inv_freq = jnp.power(ROPE_THETA, -jnp.arange(ROPE_AXIS_DIM // 2, dtype=F32) * 2.0 / ROPE_AXIS_DIM)
    ang = jnp.concatenate([row[:, None] * inv_freq, col[:, None] * inv_freq], axis=-1)
    ang = jnp.concatenate([ang, jnp.zeros((c_len, HEAD_DIM // 2), F32)], axis=0)
    cos, sin = jnp.cos(ang), jnp.sin(ang)
    return jnp.concatenate([cos, cos], axis=-1), jnp.concatenate([-sin, sin], axis=-1)


def kernel(x, c, ctx, c_ctx, ada_w, ada_b, ln_g, ln_b, ffn_w_up, ffn_conv_w, ffn_conv_b, ffn_w_down, attn_w_in, attn_q_norm, attn_k_norm, attn_sink, attn_w_out, gla_w_in, gla_w_a1, gla_w_a2, gla_b_a, gla_head_norm, gla_w_out):
    b, t_lat, d = x.shape
    c_len = ctx.shape[1]
    s = t_lat + c_len
    depth = ada_w.shape[0]
    assert b < MOD_CTX_ROW and t_lat % ROW_TILE == 0 and c_len % ROW_TILE == 0
    assert t_lat % GLA_CHUNK == 0 and c_len % GLA_CHUNK == 0 and s % MXU_N == 0
    alpha = (2 * depth) ** 0.25

    cond = jnp.zeros((MOD_ROWS, d), F32).at[:b].set(c).at[MOD_CTX_ROW].set(c_ctx)
    mods = _modulation(cond, ada_w, ada_b).reshape(depth, MOD_ROWS, 6, d)
    cos, sin = _rope_tables(t_lat, c_len)
    h = jnp.concatenate([x, ctx], axis=1)
    attn_w_in_c, attn_w_out_c = attn_w_in.astype(MXU_DTYPE), attn_w_out.astype(MXU_DTYPE)
    gla_w_in_c, gla_w_out_c = gla_w_in.astype(MXU_DTYPE), gla_w_out.astype(MXU_DTYPE)
    ffn_w_up_c, ffn_w_down_c = ffn_w_up.astype(MXU_DTYPE), ffn_w_down.astype(MXU_DTYPE)

    for i in range(depth):
        last = i == depth - 1
        n_rows = t_lat if last else s
        j = i // 2
        if i % 2 == 0:
            qkv = _attn_project(h, mods, i, attn_w_in_c, j, attn_q_norm[j], attn_k_norm[j],
                                cos, sin, t_lat)
            o_a = _global_attention(qkv, t_lat)
            o_b = _window_attention(qkv, attn_sink[j], t_lat)
            h = _mixer_ffn(_attn_ffn_kernel, "attn_out_ffn", [(o_a, A_QW, 0), (o_b, B_QW, 0)], h, mods, i, [],
                           attn_w_out_c, j, ln_g, ln_b, ffn_w_up_c, ffn_conv_w, ffn_conv_b, ffn_w_down_c,
                           t_lat, n_rows, alpha)
        else:
            w_a1 = jnp.concatenate([gla_w_a1[j, 0], gla_w_a1[j, 1]], axis=1)
            w_a1 = jnp.pad(w_a1, ((0, 0), (0, LANES - 2 * GLA_GATE_RANK))).astype(MXU_DTYPE)
            w_a2 = jnp.zeros((LANES, 2 * GLA_KEY_DIM), F32)
            w_a2 = w_a2.at[:GLA_GATE_RANK, :GLA_KEY_DIM].set(gla_w_a2[j, 0])
            w_a2 = w_a2.at[GLA_GATE_RANK:2 * GLA_GATE_RANK, GLA_KEY_DIM:].set(gla_w_a2[j, 1]).astype(MXU_DTYPE)
            proj, ld = _gla_project(h, mods, i, gla_w_in_c, j, w_a1, w_a2,
                                    gla_b_a[j].reshape(1, 2 * GLA_KEY_DIM), t_lat)
            o_f, o_r = _gla_scan(proj, ld, t_lat)
            g_blk = (2 * GLA_KEY_DIM + GLA_VAL_DIM) // GLA_VAL_DIM
            h = _mixer_ffn(_gla_ffn_kernel, "gla_out_ffn",
                           [(o_f, GLA_VAL_DIM, 0), (o_r, GLA_VAL_DIM, 0), (proj, GLA_VAL_DIM, g_blk)], h, mods, i,
                           [gla_head_norm[j].reshape(1, GLA_DV)], gla_w_out_c, j, ln_g, ln_b,
                           ffn_w_up_c, ffn_conv_w, ffn_conv_b, ffn_w_down_c, t_lat, n_rows, alpha)
    return h
```

```python
import functools

import jax
import jax.numpy as jnp
from jax import lax
from jax.experimental import pallas as pl
from jax.experimental.pallas import tpu as pltpu

GRID_W = 64
HEAD_DIM = 128
A_Q_HEADS = 4
A_KV_HEADS = 2
B_Q_HEADS = 4
B_KV_HEADS = 2
A_QW = A_Q_HEADS * HEAD_DIM
A_KVW = A_KV_HEADS * HEAD_DIM
B_QW = B_Q_HEADS * HEAD_DIM
B_KVW = B_KV_HEADS * HEAD_DIM
ATTN_IN = A_QW + 2 * A_KVW + B_QW + 2 * B_KVW
ATTN_SCALE = HEAD_DIM ** -0.5
WINDOW = 128
ROPE_THETA = 10000.0
ROPE_AXIS_DIM = HEAD_DIM // 2
GLA_HEADS = 4
GLA_DK = 128
GLA_DV = 256
GLA_KEY_DIM = GLA_HEADS * GLA_DK
GLA_VAL_DIM = GLA_HEADS * GLA_DV
GLA_IN = 2 * GLA_KEY_DIM + 2 * GLA_VAL_DIM
GLA_GATE_RANK = 16
GLA_GATE_NORM = 16.0
CONV_W = 3
EPS = 1e-6

LANES = 128
SUBLANES = 8
MXU_N = 256
VMEM_BYTES = 64 * 2 ** 20

MXU_DTYPE = jnp.bfloat16
F32 = jnp.float32

ROW_TILE = 256
PROJ_TILE = 768
HALO = 16
FF_CHUNK = 256
U_SLOTS = 3
ACT_SLOTS = 2
KV_CHUNK = 768
KV_SLOTS = 2
SOFTMAX_ROWS = 16
LOG2_E = 1.4426950408889634
GLA_CHUNK = 128
MOD_COLS = 1536


def _cparams(sem, vmem_mib):
    return pltpu.CompilerParams(dimension_semantics=sem, vmem_limit_bytes=vmem_mib * 2 ** 20)


def _dot(a, b):
    return jnp.dot(a, b, preferred_element_type=F32)


def _dot_nt(a, b):
    return lax.dot_general(a, b, (((1,), (1,)), ((), ())), preferred_element_type=F32)


def _dot_tn(a, b):
    return lax.dot_general(a, b, (((0,), (0,)), ((), ())), preferred_element_type=F32)


def _silu(x):
    return x / (1.0 + jnp.exp(-x))


def _residual_layer_norm(h, y, gate, ln_g, ln_b, alpha):
    r = alpha * h + gate * y
    mu = jnp.mean(r, axis=-1, keepdims=True)
    d = r - mu
    var = jnp.mean(d * d, axis=-1, keepdims=True)
    return d * lax.rsqrt(var + EPS) * ln_g + ln_b


def _modulated_rows(h_ref, modl_ref, modc_ref, t_lat):
    tm = h_ref.shape[0]
    rows = pl.program_id(1) * tm + lax.broadcasted_iota(jnp.int32, (tm, 1), 0)
    is_ctx = rows >= t_lat
    scale = 1.0 + jnp.where(is_ctx, modc_ref[1:2, :], modl_ref[1:2, :])
    shift = jnp.where(is_ctx, modc_ref[0:1, :], modl_ref[0:1, :])
    return (h_ref[...] * scale + shift).astype(MXU_DTYPE)


def _proj_tile(s_tot):
    return max(n for n in range(ROW_TILE, PROJ_TILE + 1, ROW_TILE) if s_tot % n == 0)


def _mod_row(t_lat, tm):
    n_lat = t_lat // tm

    def row(b, i):
        return jnp.where(i < n_lat, b, MOD_CTX_ROW)
    return row


MOD_ROWS = 8
MOD_CTX_ROW = MOD_ROWS - 1


def _mod_kernel(cond_ref, w_ref, b_ref, o_ref):
    a = _silu(cond_ref[...]).astype(MXU_DTYPE)
    o_ref[...] = _dot(a, w_ref[...].astype(MXU_DTYPE)) + b_ref[...]


def _modulation(cond, ada_w, ada_b):
    depth, d, n = ada_w.shape
    return pl.pallas_call(
        _mod_kernel,
        grid=(depth, n // MOD_COLS),
        in_specs=[
            pl.BlockSpec((MOD_ROWS, d), lambda l, j: (0, 0)),
            pl.BlockSpec((None, d, MOD_COLS), lambda l, j: (l, 0, j)),
            pl.BlockSpec((None, 1, MOD_COLS), lambda l, j: (l, 0, j)),
        ],
        out_specs=pl.BlockSpec((None, MOD_ROWS, MOD_COLS), lambda l, j: (l, 0, j)),
        out_shape=jax.ShapeDtypeStruct((depth, MOD_ROWS, n), F32),
        compiler_params=_cparams(("parallel", "parallel"), 32),
        name="ada_modulation",
    )(cond, ada_w, ada_b.reshape(depth, 1, n))


def _attn_proj_kernel(h_ref, modl_ref, modc_ref, w_ref, qn_ref, kn_ref, cos_ref, sin_ref, o_ref, *, t_lat):
    a = _modulated_rows(h_ref, modl_ref, modc_ref, t_lat)
    cos = cos_ref[...]
    sin = sin_ref[...]

    def rms(z, g):
        return z * lax.rsqrt(jnp.mean(z * z, axis=-1, keepdims=True) + EPS) * g

    def rope(z):
        return z * cos + pltpu.roll(z, HEAD_DIM // 2, 1) * sin

    a_k0 = A_QW // HEAD_DIM
    a_v0 = a_k0 + A_KV_HEADS
    b_q0 = a_v0 + A_KV_HEADS
    b_k0 = b_q0 + B_Q_HEADS
    b_v0 = b_k0 + B_KV_HEADS
    heads_per_dot = MXU_N // HEAD_DIM
    for cb in range(ATTN_IN // MXU_N):
        z2 = _dot(a, w_ref[:, cb * MXU_N:(cb + 1) * MXU_N])
        for half in range(heads_per_dot):
            hb = cb * heads_per_dot + half
            z = z2[:, half * HEAD_DIM:(half + 1) * HEAD_DIM]
            if hb < a_k0:
                z = rope(rms(z, qn_ref[...])) * (ATTN_SCALE * LOG2_E)
            elif hb < a_v0:
                z = rope(rms(z, kn_ref[...]))
            elif hb < b_q0:
                pass
            elif hb < b_k0:
                z = rope(z) * ATTN_SCALE
            elif hb < b_v0:
                z = rope(z)
            o_ref[:, hb * HEAD_DIM:(hb + 1) * HEAD_DIM] = z.astype(o_ref.dtype)


def _attn_project(h, mods, layer, w_in, mixer, q_norm, k_norm, cos, sin, t_lat):
    b, s, d = h.shape
    tm = ROW_TILE
    return pl.pallas_call(
        functools.partial(_attn_proj_kernel, t_lat=t_lat),
        grid=(b, s // tm),
        in_specs=[
            pl.BlockSpec((None, tm, d), lambda bi, i: (bi, i, 0)),
            pl.BlockSpec((None, None, 6, d), lambda bi, i: (layer, bi, 0, 0)),
            pl.BlockSpec((None, None, 6, d), lambda bi, i: (layer, MOD_CTX_ROW, 0, 0)),
            pl.BlockSpec((None, d, ATTN_IN), lambda bi, i: (mixer, 0, 0)),
            pl.BlockSpec((1, HEAD_DIM), lambda bi, i: (0, 0)),
            pl.BlockSpec((1, HEAD_DIM), lambda bi, i: (0, 0)),
            pl.BlockSpec((tm, HEAD_DIM), lambda bi, i: (i, 0)),
            pl.BlockSpec((tm, HEAD_DIM), lambda bi, i: (i, 0)),
        ],
        out_specs=pl.BlockSpec((None, tm, ATTN_IN), lambda bi, i: (bi, i, 0)),
        out_shape=jax.ShapeDtypeStruct((b, s, ATTN_IN), MXU_DTYPE),
        compiler_params=_cparams(("parallel", "parallel"), 40),
        name="attn_project",
    )(h, mods, mods, w_in, q_norm.reshape(1, HEAD_DIM), k_norm.reshape(1, HEAD_DIM), cos, sin)


def _stack_heads(q):
    g = q.shape[1] // HEAD_DIM
    return jnp.concatenate([q[:, i * HEAD_DIM:(i + 1) * HEAD_DIM] for i in range(g)], axis=0)


def _store_heads(o_ref, o, tq):
    for i in range(o.shape[0] // tq):
        o_ref[:, i * HEAD_DIM:(i + 1) * HEAD_DIM] = o[i * tq:(i + 1) * tq, :].astype(o_ref.dtype)


def _softmax_chunk(s_ref, m_scr, alpha_ref, p_ref):
    rows, n_keys = s_ref.shape
    for g in range(rows // SOFTMAX_ROWS):
        rg = slice(g * SOFTMAX_ROWS, (g + 1) * SOFTMAX_ROWS)
        blocks = [s_ref[rg, j * LANES:(j + 1) * LANES] for j in range(n_keys // LANES)]
        mx = blocks[0]
        for blk in blocks[1:]:
            mx = jnp.maximum(mx, blk)
        m_prev = m_scr[rg, :]
        m_new = jnp.maximum(m_prev, jnp.max(mx, axis=-1, keepdims=True))
        alpha_ref[rg, :] = jnp.exp2(m_prev - m_new)
        for j, blk in enumerate(blocks):
            p_ref[rg, j * LANES:(j + 1) * LANES] = jnp.exp2(blk - m_new).astype(p_ref.dtype)
        m_scr[rg, :] = m_new


def _global_attn_kernel(q_ref, k_ref, v_ref, o_ref, vext_scr, m_scr, acc_scr, s_scr, p_scr, alpha_scr,
                        *, t_lat, tq, kv_chunk):
    s_tot = k_ref.shape[0]
    c_len = s_tot - t_lat
    is_ctx = pl.program_id(2) * tq >= t_lat

    @pl.when(pl.program_id(2) == 0)
    def _():
        ones_col = jnp.where(lax.broadcasted_iota(jnp.int32, (MXU_N, HEAD_DIM), 1) == 0, 1.0, 0.0)

        def fill(blk, carry):
            rows = pl.ds(pl.multiple_of(blk * MXU_N, MXU_N), MXU_N)
            vext_scr[rows, 0:HEAD_DIM] = v_ref[rows, :]
            vext_scr[rows, HEAD_DIM:2 * HEAD_DIM] = ones_col.astype(vext_scr.dtype)
            return carry

        lax.fori_loop(0, s_tot // MXU_N, fill, 0)

    q2 = _stack_heads(q_ref[...])
    m_scr[...] = jnp.full(m_scr.shape, -jnp.inf, F32)
    acc_scr[...] = jnp.zeros(acc_scr.shape, F32)

    def keys(c):
        return slice(c * kv_chunk, (c + 1) * kv_chunk)

    def accumulate(p, v_ext, alpha):
        acc_scr[...] = jnp.concatenate([alpha, alpha], axis=1) * acc_scr[...] + _dot(p, v_ext)

    @pl.when(jnp.logical_not(is_ctx))
    def _():
        n_chunks = s_tot // kv_chunk
        s_scr[0] = _dot_nt(q2, k_ref[keys(0), :])
        for c in range(n_chunks):
            slot = c % KV_SLOTS
            if c + 1 < n_chunks:
                s_scr[(c + 1) % KV_SLOTS] = _dot_nt(q2, k_ref[keys(c + 1), :])
            if c > 0:
                prev = (c - 1) % KV_SLOTS
                accumulate(p_scr[prev], vext_scr[keys(c - 1), :], alpha_scr[prev])
            _softmax_chunk(s_scr.at[slot], m_scr, alpha_scr.at[slot], p_scr.at[slot])
        last = (n_chunks - 1) % KV_SLOTS
        accumulate(p_scr[last], vext_scr[keys(n_chunks - 1), :], alpha_scr[last])

    @pl.when(is_ctx)
    def _():
        s_scr[0, :, 0:c_len] = _dot_nt(q2, k_ref[t_lat:s_tot, :])
        _softmax_chunk(s_scr.at[0, :, 0:c_len], m_scr, alpha_scr.at[0], p_scr.at[0, :, 0:c_len])
        accumulate(p_scr[0, :, 0:c_len], vext_scr[t_lat:s_tot, :], alpha_scr[0])

    acc = acc_scr[...]
    _store_heads(o_ref, acc[:, 0:HEAD_DIM] / acc[:, HEAD_DIM:HEAD_DIM + 1], tq)


def _kv_chunk(s_tot):
    return max(n for n in range(MXU_N, KV_CHUNK + 1, MXU_N) if s_tot % n == 0)


def _global_attention(qkv, t_lat):
    b, s, _ = qkv.shape
    tq = ROW_TILE
    group = A_Q_HEADS // A_KV_HEADS
    k0 = A_QW // HEAD_DIM
    v0 = k0 + A_KV_HEADS
    kv_chunk = _kv_chunk(s)
    assert s - t_lat <= kv_chunk
    return pl.pallas_call(
        functools.partial(_global_attn_kernel, t_lat=t_lat, tq=tq, kv_chunk=kv_chunk),
        grid=(b, A_KV_HEADS, s // tq),
        in_specs=[
            pl.BlockSpec((None, tq, group * HEAD_DIM), lambda bi, hk, i: (bi, i, hk)),
            pl.BlockSpec((None, s, HEAD_DIM), lambda bi, hk, i: (bi, 0, k0 + hk)),
            pl.BlockSpec((None, s, HEAD_DIM), lambda bi, hk, i: (bi, 0, v0 + hk)),
        ],
        out_specs=pl.BlockSpec((None, tq, group * HEAD_DIM), lambda bi, hk, i: (bi, i, hk)),
        out_shape=jax.ShapeDtypeStruct((b, s, A_QW), MXU_DTYPE),
        scratch_shapes=[
            pltpu.VMEM((s, 2 * HEAD_DIM), MXU_DTYPE),
            pltpu.VMEM((group * tq, LANES), F32),
            pltpu.VMEM((group * tq, 2 * HEAD_DIM), F32),
            pltpu.VMEM((KV_SLOTS, group * tq, kv_chunk), F32),
            pltpu.VMEM((KV_SLOTS, group * tq, kv_chunk), MXU_DTYPE),
            pltpu.VMEM((KV_SLOTS, group * tq, LANES), F32),
        ],
        compiler_params=_cparams(("parallel", "parallel", "arbitrary"), 40),
        name="global_attention",
    )(qkv, qkv, qkv)


def _window_attn_kernel(sink_ref, q_ref, k_ref, v_ref, o_ref, *, t_lat, tq):
    s_tot = k_ref.shape[0]
    c_len = s_tot - t_lat
    band = tq + 2 * WINDOW
    q0 = pl.program_id(1) * tq
    is_ctx = q0 >= t_lat
    start = pl.multiple_of(jnp.clip(q0 - WINDOW, 0, t_lat - band), WINDOW)
    group = B_Q_HEADS // B_KV_HEADS

    delta = (lax.broadcasted_iota(jnp.int32, (tq, band), 1) + (start - q0)
             - lax.broadcasted_iota(jnp.int32, (tq, band), 0))
    reach = jnp.where(is_ctx, -1, WINDOW)
    valid = jnp.abs(delta) <= reach
    for hk in range(B_KV_HEADS):
        kv = slice(hk * HEAD_DIM, (hk + 1) * HEAD_DIM)
        q2 = _stack_heads(q_ref[:, hk * group * HEAD_DIM:(hk + 1) * group * HEAD_DIM])
        s_band = _dot_nt(q2, k_ref[pl.ds(start, band), kv])
        s_ctx = _dot_nt(q2, k_ref[pl.ds(t_lat, c_len), kv])
        p_band, p_ctx, denom = [], [], []
        for g in range(group):
            head = slice(g * tq, (g + 1) * tq)
            sb = jnp.where(valid, s_band[head, :], -jnp.inf)
            sc = s_ctx[head, :]
            sink = sink_ref[hk * group + g]
            m = jnp.maximum(jnp.maximum(jnp.max(sb, axis=-1, keepdims=True),
                                        jnp.max(sc, axis=-1, keepdims=True)), sink)
            pb = jnp.exp(sb - m)
            pc = jnp.exp(sc - m)
            denom.append(jnp.sum(pb, axis=-1, keepdims=True) + jnp.sum(pc, axis=-1, keepdims=True)
                         + jnp.exp(sink - m))
            p_band.append(pb.astype(MXU_DTYPE))
            p_ctx.append(pc.astype(MXU_DTYPE))
        o = (_dot(jnp.concatenate(p_band, axis=0), v_ref[pl.ds(start, band), kv])
             + _dot(jnp.concatenate(p_ctx, axis=0), v_ref[pl.ds(t_lat, c_len), kv]))
        o = o / jnp.concatenate(denom, axis=0)
        for g in range(group):
            cols = slice((hk * group + g) * HEAD_DIM, (hk * group + g + 1) * HEAD_DIM)
            o_ref[:, cols] = o[g * tq:(g + 1) * tq, :].astype(o_ref.dtype)


def _window_attention(qkv, sink, t_lat):
    b, s, _ = qkv.shape
    tq = ROW_TILE
    assert tq & (tq - 1) == 0 and t_lat >= tq + 2 * WINDOW
    q0 = (A_QW + 2 * A_KVW) // B_QW
    k0 = (A_QW + 2 * A_KVW + B_QW) // B_KVW
    v0 = k0 + 1
    return pl.pallas_call(
        functools.partial(_window_attn_kernel, t_lat=t_lat, tq=tq),
        grid=(b, s // tq),
        in_specs=[
            pl.BlockSpec(memory_space=pltpu.SMEM),
            pl.BlockSpec((None, tq, B_QW), lambda bi, i: (bi, i, q0)),
            pl.BlockSpec((None, s, B_KVW), lambda bi, i: (bi, 0, k0)),
            pl.BlockSpec((None, s, B_KVW), lambda bi, i: (bi, 0, v0)),
        ],
        out_specs=pl.BlockSpec((None, tq, B_QW), lambda bi, i: (bi, i, 0)),
        out_shape=jax.ShapeDtypeStruct((b, s, B_QW), MXU_DTYPE),
        compiler_params=_cparams(("parallel", "arbitrary"), 40),
        name="window_attention",
    )(sink, qkv, qkv, qkv)


def _log_sigmoid(x):
    return jnp.minimum(x, 0.0) - jnp.log(1.0 + jnp.exp(-jnp.abs(x)))


def _gla_proj_kernel(h_ref, modl_ref, modc_ref, w_ref, wa1_ref, wa2_ref, ba_ref, p_ref, ld_ref, *, t_lat):
    a = _modulated_rows(h_ref, modl_ref, modc_ref, t_lat)
    for cb in range(GLA_IN // MXU_N):
        z = _dot(a, w_ref[:, cb * MXU_N:(cb + 1) * MXU_N])
        if cb < GLA_KEY_DIM // MXU_N:
            z = z * (GLA_DK ** -0.5)
        p_ref[:, cb * MXU_N:(cb + 1) * MXU_N] = z.astype(p_ref.dtype)
    low = _dot(a, wa1_ref[...]).astype(MXU_DTYPE)
    for cb in range(2 * GLA_KEY_DIM // MXU_N):
        cols = slice(cb * MXU_N, (cb + 1) * MXU_N)
        logits = _dot(low, wa2_ref[:, cols]) + ba_ref[:, cols]
        ld_ref[:, cols] = _log_sigmoid(logits) * (LOG2_E / GLA_GATE_NORM)


def _gla_project(h, mods, layer, w_in, mixer, w_a1, w_a2, b_a, t_lat):
    b, s, d = h.shape
    tm = _proj_tile(s)
    return pl.pallas_call(
        functools.partial(_gla_proj_kernel, t_lat=t_lat),
        grid=(b, s // tm),
        in_specs=[
            pl.BlockSpec((None, tm, d), lambda bi, i: (bi, i, 0)),
            pl.BlockSpec((None, None, 6, d), lambda bi, i: (layer, bi, 0, 0)),
            pl.BlockSpec((None, None, 6, d), lambda bi, i: (layer, MOD_CTX_ROW, 0, 0)),
            pl.BlockSpec((None, d, GLA_IN), lambda bi, i: (mixer, 0, 0)),
            pl.BlockSpec((d, LANES), lambda bi, i: (0, 0)),
            pl.BlockSpec((LANES, 2 * GLA_KEY_DIM), lambda bi, i: (0, 0)),
            pl.BlockSpec((1, 2 * GLA_KEY_DIM), lambda bi, i: (0, 0)),
        ],
        out_specs=[
            pl.BlockSpec((None, tm, GLA_IN), lambda bi, i: (bi, i, 0)),
            pl.BlockSpec((None, tm, 2 * GLA_KEY_DIM), lambda bi, i: (bi, i, 0)),
        ],
        out_shape=[
            jax.ShapeDtypeStruct((b, s, GLA_IN), MXU_DTYPE),
            jax.ShapeDtypeStruct((b, s, 2 * GLA_KEY_DIM), F32),
        ],
        compiler_params=_cparams(("parallel", "parallel"), 48),
        name="gla_project",
    )(h, mods, mods, w_in, w_a1, w_a2, b_a)


def _cumsum_rows(x):
    rows, cols = x.shape
    sub = lax.broadcasted_iota(jnp.int32, x.shape, 0) & (SUBLANES - 1)
    shift = 1
    while shift < SUBLANES:
        x = x + jnp.where(sub >= shift, pltpu.roll(x, shift, 0), 0.0)
        shift *= 2
    x3 = x.reshape(rows // SUBLANES, SUBLANES, cols)
    carry = jnp.zeros((1, cols), F32)
    tiles = []
    for i in range(rows // SUBLANES):
        tile = x3[i] + carry
        tiles.append(tile)
        carry = tile[SUBLANES - 1:SUBLANES, :]
    return jnp.concatenate(tiles, axis=0)


def _block_anchor(p, n):
    rows, cols = p.shape
    half = n // 2
    if n >= SUBLANES:
        p3 = p.reshape(rows // n, n, cols)
        return jnp.broadcast_to(p3[:, half - 1:half, :], p3.shape).reshape(rows, cols)
    if n == 2:
        odd = (lax.broadcasted_iota(jnp.int32, p.shape, 0) & 1) == 1
        return jnp.where(odd, pltpu.roll(p, 1, 0), p)
    p3 = p.reshape(rows // SUBLANES, SUBLANES, cols)
    sub = lax.broadcasted_iota(jnp.int32, p3.shape, 1)
    a = None
    for b0 in range(0, SUBLANES, n):
        row = jnp.broadcast_to(p3[:, b0 + half - 1:b0 + half, :], p3.shape)
        a = row if a is None else jnp.where(sub >= b0, row, a)
    return a.reshape(rows, cols)


def _gla_chunk(q, k, v, ld, sign_ref, st_ref, reverse):
    chunk = q.shape[0]
    p_inc = _cumsum_rows(ld)
    p_tot = p_inc[chunk - 1:chunk, :]
    px = p_inc - ld if reverse else p_inc
    qf = q.astype(F32)
    kf = k.astype(F32)

    def decayed(x, exponent):
        return (x * jnp.exp2(exponent)).astype(MXU_DTYPE)

    row = lax.broadcasted_iota(jnp.int32, (chunk, chunk), 0)
    col = lax.broadcasted_iota(jnp.int32, (chunk, chunk), 1)
    differ = row ^ col
    att = jnp.broadcast_to(jnp.sum(qf * kf, axis=-1, keepdims=True), (chunk, chunk))
    n = 2
    while n <= chunk:
        level = n.bit_length() - 2
        exponent = (px - _block_anchor(p_inc, n)) * sign_ref[level]
        s_lvl = _dot_nt(decayed(qf, exponent), decayed(kf, exponent))
        att = jnp.where((differ >> level) == 1, s_lvl, att)
        n *= 2
    att = jnp.where(row <= col if reverse else row >= col, att, 0.0)

    if reverse:
        qd = decayed(qf, p_tot - px)
        kd = decayed(kf, px)
    else:
        qd = decayed(qf, px)
        kd = decayed(kf, p_tot - px)
    state = st_ref[...]
    o = _dot(att.astype(MXU_DTYPE), v) + _dot_nt(qd, state.astype(MXU_DTYPE))
    st_ref[...] = state * jnp.exp2(p_tot) + _dot_tn(v, kd)
    return o


def _gla_scan_kernel(sign_ref, qf_ref, kf_ref, vf_ref, ldf_ref, qr_ref, kr_ref, vr_ref, ldr_ref,
                     of_ref, or_ref, st_ref):
    @pl.when(pl.program_id(1) == 0)
    def _():
        st_ref[...] = jnp.zeros(st_ref.shape, F32)

    operands = ((qf_ref, kf_ref, vf_ref, ldf_ref, of_ref), (qr_ref, kr_ref, vr_ref, ldr_ref, or_ref))
    for direction, (q_ref, k_ref, v_ref, ld_ref, o_ref) in enumerate(operands):
        for hd in range(GLA_HEADS):
            kc = slice(hd * GLA_DK, (hd + 1) * GLA_DK)
            vc = slice(hd * GLA_DV, (hd + 1) * GLA_DV)
            o = _gla_chunk(q_ref[:, kc], k_ref[:, kc], v_ref[:, vc], ld_ref[:, kc], sign_ref,
                           st_ref.at[direction, hd], reverse=direction == 1)
            o_ref[:, vc] = o.astype(o_ref.dtype)


def _gla_scan(proj, ld, t_lat):
    b, s, _ = proj.shape
    ch = GLA_CHUNK
    n_chunks = s // ch
    lat_chunks = t_lat // ch

    def fwd(c):
        return (c + lat_chunks) % n_chunks

    def rev(c):
        return n_chunks - 1 - c

    def specs(chunk_of, direction):
        return [
            pl.BlockSpec((None, ch, GLA_KEY_DIM), lambda bi, c: (bi, chunk_of(c), 0)),
            pl.BlockSpec((None, ch, GLA_KEY_DIM), lambda bi, c: (bi, chunk_of(c), 1)),
            pl.BlockSpec((None, ch, GLA_VAL_DIM), lambda bi, c: (bi, chunk_of(c), 1)),
            pl.BlockSpec((None, ch, GLA_KEY_DIM), lambda bi, c: (bi, chunk_of(c), direction)),
        ]

    levels = ch.bit_length() - 1
    half_bit = 1 << jnp.arange(levels, dtype=jnp.int32)[:, None, None]
    in_upper = (jnp.arange(ch, dtype=jnp.int32)[None, :, None] & half_bit) != 0
    sign = jnp.broadcast_to(jnp.where(in_upper, 1.0, -1.0).astype(F32), (levels, ch, GLA_DK))
    o_shape = jax.ShapeDtypeStruct((b, s, GLA_VAL_DIM), F32)
    return pl.pallas_call(
        _gla_scan_kernel,
        grid=(b, n_chunks),
        in_specs=[pl.BlockSpec((levels, ch, GLA_DK), lambda bi, c: (0, 0, 0))] + specs(fwd, 0) + specs(rev, 1),
        out_specs=[
            pl.BlockSpec((None, ch, GLA_VAL_DIM), lambda bi, c: (bi, fwd(c), 0)),
            pl.BlockSpec((None, ch, GLA_VAL_DIM), lambda bi, c: (bi, rev(c), 0)),
        ],
        out_shape=[o_shape, o_shape],
        scratch_shapes=[pltpu.VMEM((2, GLA_HEADS, GLA_DV, GLA_DK), F32)],
        compiler_params=_cparams(("parallel", "arbitrary"), 32),
        name="gla_scan",
    )(sign, proj, proj, proj, ld, proj, proj, proj, ld)


def _with_halo(ref, next_ref):
    return jnp.concatenate([ref[...], next_ref[...]], axis=0)


def _mixer_out_norm(x, h_refs, mod_ref, wout_ref, lng_ref, lnb_ref, h1_scr, alpha):
    tm = h_refs[0].shape[0]

    @pl.when(pl.program_id(1) == 0)
    def _():
        h1_scr[0:HALO, :] = jnp.zeros((HALO, h1_scr.shape[1]), F32)

    @pl.when(pl.program_id(1) > 0)
    def _():
        h1_scr[0:HALO, :] = h1_scr[tm:tm + HALO, :]

    y = _dot(x, wout_ref[...])
    h1_scr[HALO:, :] = _residual_layer_norm(_with_halo(*h_refs), y, mod_ref[2:3, :], lng_ref[...], lnb_ref[...],
                                            alpha)


def _attn_ffn_kernel(oa_ref, oan_ref, ob_ref, obn_ref, h_ref, hn_ref, mod_ref,
                     wout_ref, ln1g_ref, ln1b_ref, wup_ref, cw_ref, cb_ref, wdn_ref, ln2g_ref, ln2b_ref,
                     out_ref, h1_scr, a_scr, u_scr, act_scr, *, t_lat, s_tot, alpha):
    x = jnp.concatenate([_with_halo(oa_ref, oan_ref), _with_halo(ob_ref, obn_ref)], axis=1)
    _mixer_out_norm(x, (h_ref, hn_ref), mod_ref, wout_ref, ln1g_ref, ln1b_ref, h1_scr, alpha)
    _conv_ffn_norm(h1_scr, mod_ref, wup_ref, cw_ref, cb_ref, wdn_ref, ln2g_ref, ln2b_ref, out_ref,
                   a_scr, u_scr, act_scr, t_lat=t_lat, s_tot=s_tot, alpha=alpha)


def _gla_ffn_kernel(of_ref, ofn_ref, or_ref, orn_ref, g_ref, gn_ref,
                    h_ref, hn_ref, mod_ref, hnorm_ref, wout_ref, ln1g_ref, ln1b_ref,
                    wup_ref, cw_ref, cb_ref, wdn_ref, ln2g_ref, ln2b_ref,
                    out_ref, h1_scr, a_scr, u_scr, act_scr, *, t_lat, s_tot, alpha):
    o = _with_halo(of_ref, ofn_ref) + _with_halo(or_ref, orn_ref)
    gate = _silu(_with_halo(g_ref, gn_ref).astype(F32))
    heads = []
    for hd in range(GLA_HEADS):
        cols = slice(hd * GLA_DV, (hd + 1) * GLA_DV)
        oh = o[:, cols]
        oh = oh * lax.rsqrt(jnp.mean(oh * oh, axis=-1, keepdims=True) + EPS) * hnorm_ref[...]
        heads.append((oh * gate[:, cols]).astype(MXU_DTYPE))
    x = jnp.concatenate(heads, axis=1)
    _mixer_out_norm(x, (h_ref, hn_ref), mod_ref, wout_ref, ln1g_ref, ln1b_ref, h1_scr, alpha)
    _conv_ffn_norm(h1_scr, mod_ref, wup_ref, cw_ref, cb_ref, wdn_ref, ln2g_ref, ln2b_ref, out_ref,
                   a_scr, u_scr, act_scr, t_lat=t_lat, s_tot=s_tot, alpha=alpha)


def _conv_ffn_norm(h1_scr, mod_ref, wup_ref, cw_ref, cb_ref, wdn_ref, lng_ref, lnb_ref,
                   out_ref, a_scr, u_scr, act_scr, *, t_lat, s_tot, alpha):
    tm = out_ref.shape[0]
    d_ff = wdn_ref.shape[0]
    row0 = pl.program_id(1) * tm
    at_start = (row0 == 0) | (row0 == t_lat)
    at_end = (row0 + tm == t_lat) | (row0 + tm == s_tot)
    shift = mod_ref[3:4, :]
    scale = 1.0 + mod_ref[4:5, :]
    a_all = h1_scr[...] * scale + shift
    a_scr[0:HALO, :] = jnp.where(at_start, 0.0, a_all[0:HALO, :]).astype(a_scr.dtype)
    a_scr[HALO:HALO + tm, :] = a_all[HALO:HALO + tm, :].astype(a_scr.dtype)
    a_scr[HALO + tm:2 * HALO + tm, :] = jnp.where(at_end, 0.0, a_all[HALO + tm:2 * HALO + tm, :]).astype(a_scr.dtype)
    a = a_scr[...]

    def conv(u_ref, c0):
        cols = slice(c0, c0 + FF_CHUNK)
        acc = cb_ref[:, cols] + cw_ref[0:1, cols] * u_ref[pl.ds(HALO - 1, tm), :]
        for j in range(1, CONV_W):
            acc = acc + cw_ref[j:j + 1, cols] * u_ref[pl.ds(HALO - 1 + j, tm), :]
        return acc

    def up_project(c):
        g0 = c * FF_CHUNK
        v0 = d_ff + c * FF_CHUNK
        u_scr[c % U_SLOTS, 0] = _dot(a, wup_ref[:, g0:g0 + FF_CHUNK])
        u_scr[c % U_SLOTS, 1] = _dot(a, wup_ref[:, v0:v0 + FF_CHUNK])

    def down_project(c):
        return _dot(act_scr[c % ACT_SLOTS], wdn_ref[c * FF_CHUNK:(c + 1) * FF_CHUNK, :])

    n_chunks = d_ff // FF_CHUNK
    ahead = U_SLOTS - 1
    y = jnp.zeros(out_ref.shape, F32)
    for c in range(min(ahead, n_chunks)):
        up_project(c)
    for c in range(n_chunks):
        if c + ahead < n_chunks:
            up_project(c + ahead)
        if c > 0:
            y = y + down_project(c - 1)
        g0 = c * FF_CHUNK
        act = _silu(conv(u_scr.at[c % U_SLOTS, 0], g0)) * conv(u_scr.at[c % U_SLOTS, 1], d_ff + g0)
        act_scr[c % ACT_SLOTS] = act.astype(act_scr.dtype)
    y = y + down_project(n_chunks - 1)
    h1 = h1_scr[HALO:HALO + tm, :]
    out_ref[...] = _residual_layer_norm(h1, y, mod_ref[5:6, :], lng_ref[...], lnb_ref[...], alpha)


def _mixer_ffn(kernel_fn, name, mixer_ins, h, mods, layer, extra_params, w_out, mixer, ln_g, ln_b,
               w_up, conv_w, conv_b, w_down, t_lat, n_rows, alpha):
    b, s, d = h.shape
    tm = ROW_TILE
    d_ff = w_down.shape[1]
    row = _mod_row(t_lat, tm)
    hb = tm // HALO
    n_halo = s // HALO

    def halo_specs(cols, cb):
        return [
            pl.BlockSpec((None, tm, cols), lambda bi, i: (bi, i, cb)),
            pl.BlockSpec((None, HALO, cols), lambda bi, i: (bi, jnp.minimum((i + 1) * hb, n_halo - 1), cb)),
        ]

    row_ins = list(mixer_ins) + [(h, d, 0)]
    in_specs, operands = [], []
    for arr, cols, cb in row_ins:
        in_specs += halo_specs(cols, cb)
        operands += [arr, arr]
    in_specs.append(pl.BlockSpec((None, None, 6, d), lambda bi, i: (layer, row(bi, i), 0, 0)))
    operands.append(mods)
    for p in extra_params:
        in_specs.append(pl.BlockSpec(p.shape, lambda bi, i: (0, 0)))
        operands.append(p)
    in_specs += [
        pl.BlockSpec((None,) + w_out.shape[1:], lambda bi, i: (mixer, 0, 0)),
        pl.BlockSpec((None, 1, d), lambda bi, i: (layer, 0, 0)),
        pl.BlockSpec((None, 1, d), lambda bi, i: (layer, 0, 0)),
        pl.BlockSpec((None, d, 2 * d_ff), lambda bi, i: (layer, 0, 0)),
        pl.BlockSpec((None, CONV_W, 2 * d_ff), lambda bi, i: (layer, 0, 0)),
        pl.BlockSpec((None, 1, 2 * d_ff), lambda bi, i: (layer, 0, 0)),
        pl.BlockSpec((None, d_ff, d), lambda bi, i: (layer, 0, 0)),
        pl.BlockSpec((None, 1, d), lambda bi, i: (layer, 0, 0)),
        pl.BlockSpec((None, 1, d), lambda bi, i: (layer, 0, 0)),
    ]
    operands += [w_out, ln_g[:, 0:1, :], ln_b[:, 0:1, :], w_up, conv_w, conv_b.reshape(-1, 1, 2 * d_ff), w_down,
                 ln_g[:, 1:2, :], ln_b[:, 1:2, :]]
    return pl.pallas_call(
        functools.partial(kernel_fn, t_lat=t_lat, s_tot=s, alpha=alpha),
        grid=(b, n_rows // tm),
        in_specs=in_specs,
        out_specs=pl.BlockSpec((None, tm, d), lambda bi, i: (bi, i, 0)),
        out_shape=jax.ShapeDtypeStruct((b, n_rows, d), F32),
        scratch_shapes=[
            pltpu.VMEM((tm + 2 * HALO, d), F32),
            pltpu.VMEM((tm + 2 * HALO, d), MXU_DTYPE),
            pltpu.VMEM((U_SLOTS, 2, tm + 2 * HALO, FF_CHUNK), F32),
            pltpu.VMEM((ACT_SLOTS, tm, FF_CHUNK), MXU_DTYPE),
        ],
        compiler_params=_cparams(("parallel", "arbitrary"), 60),
        name=name,
    )(*operands)


def _rope_tables(t_lat, c_len):
    rows = t_lat // GRID_W
    row = jnp.repeat(jnp.arange(rows, dtype=F32), GRID_W)
    col = jnp.tile(jnp.arange(GRID_W, dtype=F32), rows)
    inv_freq = jnp.power(ROPE_THETA, -jnp.arange(ROPE_AXIS_DIM // 2, dtype=F32) * 2.0 / ROPE_AXIS_DIM)
    ang = jnp.concatenate([row[:, None] * inv_freq, col[:, None] * inv_freq], axis=-1)
    ang = jnp.concatenate([ang, jnp.zeros((c_len, HEAD_DIM // 2), F32)], axis=0)
    cos, sin = jnp.cos(ang), jnp.sin(ang)
    return jnp.concatenate([cos, cos], axis=-1), jnp.concatenate([-sin, sin], axis=-1)


def kernel(x, c, ctx, c_ctx, ada_w, ada_b, ln_g, ln_b, ffn_w_up, ffn_conv_w, ffn_conv_b, ffn_w_down, attn_w_in, attn_q_norm, attn_k_norm, attn_sink, attn_w_out, gla_w_in, gla_w_a1, gla_w_a2, gla_b_a, gla_head_norm, gla_w_out):
    b, t_lat, d = x.shape
    c_len = ctx.shape[1]
    s = t_lat + c_len
    depth = ada_w.shape[0]
    assert b < MOD_CTX_ROW and t_lat % ROW_TILE == 0 and c_len % ROW_TILE == 0
    assert t_lat % GLA_CHUNK == 0 and c_len % GLA_CHUNK == 0 and s % MXU_N == 0
    alpha = (2 * depth) ** 0.25

    cond = jnp.zeros((MOD_ROWS, d), F32).at[:b].set(c).at[MOD_CTX_ROW].set(c_ctx)
    mods = _modulation(cond, ada_w, ada_b).reshape(depth, MOD_ROWS, 6, d)
    cos, sin = _rope_tables(t_lat, c_len)
    h = jnp.concatenate([x, ctx], axis=1)
    attn_w_in_c, attn_w_out_c = attn_w_in.astype(MXU_DTYPE), attn_w_out.astype(MXU_DTYPE)
    gla_w_in_c, gla_w_out_c = gla_w_in.astype(MXU_DTYPE), gla_w_out.astype(MXU_DTYPE)
    ffn_w_up_c, ffn_w_down_c = ffn_w_up.astype(MXU_DTYPE), ffn_w_down.astype(MXU_DTYPE)

    for i in range(depth):
        last = i == depth - 1
        n_rows = t_lat if last else s
        j = i // 2
        if i % 2 == 0:
            qkv = _attn_project(h, mods, i, attn_w_in_c, j, attn_q_norm[j], attn_k_norm[j],
                                cos, sin, t_lat)
            o_a = _global_attention(qkv, t_lat)
            o_b = _window_attention(qkv, attn_sink[j], t_lat)
            h = _mixer_ffn(_attn_ffn_kernel, "attn_out_ffn", [(o_a, A_QW, 0), (o_b, B_QW, 0)], h, mods, i, [],
                           attn_w_out_c, j, ln_g, ln_b, ffn_w_up_c, ffn_conv_w, ffn_conv_b, ffn_w_down_c,
                           t_lat, n_rows, alpha)
        else:
            w_a1 = jnp.concatenate([gla_w_a1[j, 0], gla_w_a1[j, 1]], axis=1)
            w_a1 = jnp.pad(w_a1, ((0, 0), (0, LANES - 2 * GLA_GATE_RANK))).astype(MXU_DTYPE)
            w_a2 = jnp.zeros((LANES, 2 * GLA_KEY_DIM), F32)
            w_a2 = w_a2.at[:GLA_GATE_RANK, :GLA_KEY_DIM].set(gla_w_a2[j, 0])
            w_a2 = w_a2.at[GLA_GATE_RANK:2 * GLA_GATE_RANK, GLA_KEY_DIM:].set(gla_w_a2[j, 1]).astype(MXU_DTYPE)
            proj, ld = _gla_project(h, mods, i, gla_w_in_c, j, w_a1, w_a2,
                                    gla_b_a[j].reshape(1, 2 * GLA_KEY_DIM), t_lat)
            o_f, o_r = _gla_scan(proj, ld, t_lat)
            g_blk = (2 * GLA_KEY_DIM + GLA_VAL_DIM) // GLA_VAL_DIM
            h = _mixer_ffn(_gla_ffn_kernel, "gla_out_ffn",
                           [(o_f, GLA_VAL_DIM, 0), (o_r, GLA_VAL_DIM, 0), (proj, GLA_VAL_DIM, g_blk)], h, mods, i,
                           [gla_head_norm[j].reshape(1, GLA_DV)], gla_w_out_c, j, ln_g, ln_b,
                           ffn_w_up_c, ffn_conv_w, ffn_conv_b, ffn_w_down_c, t_lat, n_rows, alpha)
    return h
```

```python
import functools

import jax
import jax.numpy as jnp
from jax import lax
from jax.experimental import pallas as pl
from jax.experimental.pallas import tpu as pltpu

GRID_W = 64
HEAD_DIM = 128
A_Q_HEADS = 4
A_KV_HEADS = 2
B_Q_HEADS = 4
B_KV_HEADS = 2
A_QW = A_Q_HEADS * HEAD_DIM
A_KVW = A_KV_HEADS * HEAD_DIM
B_QW = B_Q_HEADS * HEAD_DIM
B_KVW = B_KV_HEADS * HEAD_DIM
ATTN_IN = A_QW + 2 * A_KVW + B_QW + 2 * B_KVW
ATTN_SCALE = HEAD_DIM ** -0.5
WINDOW = 128
ROPE_THETA = 10000.0
ROPE_AXIS_DIM = HEAD_DIM // 2
GLA_HEADS = 4
GLA_DK = 128
GLA_DV = 256
GLA_KEY_DIM = GLA_HEADS * GLA_DK
GLA_VAL_DIM = GLA_HEADS * GLA_DV
GLA_IN = 2 * GLA_KEY_DIM + 2 * GLA_VAL_DIM
GLA_GATE_RANK = 16
GLA_GATE_NORM = 16.0
CONV_W = 3
EPS = 1e-6

LANES = 128
SUBLANES = 8
MXU_N = 256
VMEM_BYTES = 64 * 2 ** 20

MXU_DTYPE = jnp.bfloat16
F32 = jnp.float32

ROW_TILE = 256
PROJ_TILE = 768
HALO = 16
FF_CHUNK = 256
U_SLOTS = 3
ACT_SLOTS = 2
KV_CHUNK = 2816
KV_EDGE = 768
KV_SLOTS = 2
SOFTMAX_ROWS = 16
LOG2_E = 1.4426950408889634
GLA_CHUNK = 128
MOD_COLS = 1536


def _cparams(sem, vmem_mib):
    return pltpu.CompilerParams(dimension_semantics=sem, vmem_limit_bytes=vmem_mib * 2 ** 20)


def _dot(a, b):
    return jnp.dot(a, b, preferred_element_type=F32)


def _dot_nt(a, b):
    return lax.dot_general(a, b, (((1,), (1,)), ((), ())), preferred_element_type=F32)


def _dot_tn(a, b):
    return lax.dot_general(a, b, (((0,), (0,)), ((), ())), preferred_element_type=F32)


def _silu(x):
    return x / (1.0 + jnp.exp(-x))


def _residual_layer_norm(h, y, gate, ln_g, ln_b, alpha):
    r = alpha * h + gate * y
    mu = jnp.mean(r, axis=-1, keepdims=True)
    d = r - mu
    var = jnp.mean(d * d, axis=-1, keepdims=True)
    return d * lax.rsqrt(var + EPS) * ln_g + ln_b


def _modulated_rows(h_ref, modl_ref, modc_ref, t_lat):
    tm = h_ref.shape[0]
    rows = pl.program_id(1) * tm + lax.broadcasted_iota(jnp.int32, (tm, 1), 0)
    is_ctx = rows >= t_lat
    scale = 1.0 + jnp.where(is_ctx, modc_ref[1:2, :], modl_ref[1:2, :])
    shift = jnp.where(is_ctx, modc_ref[0:1, :], modl_ref[0:1, :])
    return (h_ref[...] * scale + shift).astype(MXU_DTYPE)


def _proj_tile(s_tot):
    return max(n for n in range(ROW_TILE, PROJ_TILE + 1, ROW_TILE) if s_tot % n == 0)


def _mod_row(t_lat, tm):
    n_lat = t_lat // tm

    def row(b, i):
        return jnp.where(i < n_lat, b, MOD_CTX_ROW)
    return row


MOD_ROWS = 8
MOD_CTX_ROW = MOD_ROWS - 1


def _mod_kernel(cond_ref, w_ref, b_ref, o_ref):
    a = _silu(cond_ref[...]).astype(MXU_DTYPE)
    o_ref[...] = _dot(a, w_ref[...].astype(MXU_DTYPE)) + b_ref[...]


def _modulation(cond, ada_w, ada_b):
    depth, d, n = ada_w.shape
    return pl.pallas_call(
        _mod_kernel,
        grid=(depth, n // MOD_COLS),
        in_specs=[
            pl.BlockSpec((MOD_ROWS, d), lambda l, j: (0, 0)),
            pl.BlockSpec((None, d, MOD_COLS), lambda l, j: (l, 0, j)),
            pl.BlockSpec((None, 1, MOD_COLS), lambda l, j: (l, 0, j)),
        ],
        out_specs=pl.BlockSpec((None, MOD_ROWS, MOD_COLS), lambda l, j: (l, 0, j)),
        out_shape=jax.ShapeDtypeStruct((depth, MOD_ROWS, n), F32),
        compiler_params=_cparams(("parallel", "parallel"), 32),
        name="ada_modulation",
    )(cond, ada_w, ada_b.reshape(depth, 1, n))


def _attn_proj_kernel(h_ref, modl_ref, modc_ref, w_ref, qn_ref, kn_ref, cos_ref, sin_ref, o_ref, *, t_lat):
    a = _modulated_rows(h_ref, modl_ref, modc_ref, t_lat)
    cos = cos_ref[...]
    sin = sin_ref[...]

    def rms(z, g):
        return z * lax.rsqrt(jnp.mean(z * z, axis=-1, keepdims=True) + EPS) * g

    def rope(z):
        return z * cos + pltpu.roll(z, HEAD_DIM // 2, 1) * sin

    a_k0 = A_QW // HEAD_DIM
    a_v0 = a_k0 + A_KV_HEADS
    b_q0 = a_v0 + A_KV_HEADS
    b_k0 = b_q0 + B_Q_HEADS
    b_v0 = b_k0 + B_KV_HEADS
    heads_per_dot = MXU_N // HEAD_DIM
    for cb in range(ATTN_IN // MXU_N):
        z2 = _dot(a, w_ref[:, cb * MXU_N:(cb + 1) * MXU_N])
        for half in range(heads_per_dot):
            hb = cb * heads_per_dot + half
            z = z2[:, half * HEAD_DIM:(half + 1) * HEAD_DIM]
            if hb < a_k0:
                z = rope(rms(z, qn_ref[...])) * (ATTN_SCALE * LOG2_E)
            elif hb < a_v0:
                z = rope(rms(z, kn_ref[...]))
            elif hb < b_q0:
                pass
            elif hb < b_k0:
                z = rope(z) * ATTN_SCALE
            elif hb < b_v0:
                z = rope(z)
            o_ref[:, hb * HEAD_DIM:(hb + 1) * HEAD_DIM] = z.astype(o_ref.dtype)


def _attn_project(h, mods, layer, w_in, mixer, q_norm, k_norm, cos, sin, t_lat):
    b, s, d = h.shape
    tm = ROW_TILE
    return pl.pallas_call(
        functools.partial(_attn_proj_kernel, t_lat=t_lat),
        grid=(b, s // tm),
        in_specs=[
            pl.BlockSpec((None, tm, d), lambda bi, i: (bi, i, 0)),
            pl.BlockSpec((None, None, 6, d), lambda bi, i: (layer, bi, 0, 0)),
            pl.BlockSpec((None, None, 6, d), lambda bi, i: (layer, MOD_CTX_ROW, 0, 0)),
            pl.BlockSpec((None, d, ATTN_IN), lambda bi, i: (mixer, 0, 0)),
            pl.BlockSpec((1, HEAD_DIM), lambda bi, i: (0, 0)),
            pl.BlockSpec((1, HEAD_DIM), lambda bi, i: (0, 0)),
            pl.BlockSpec((tm, HEAD_DIM), lambda bi, i: (i, 0)),
            pl.BlockSpec((tm, HEAD_DIM), lambda bi, i: (i, 0)),
        ],
        out_specs=pl.BlockSpec((None, tm, ATTN_IN), lambda bi, i: (bi, i, 0)),
        out_shape=jax.ShapeDtypeStruct((b, s, ATTN_IN), MXU_DTYPE),
        compiler_params=_cparams(("parallel", "parallel"), 40),
        name="attn_project",
    )(h, mods, mods, w_in, q_norm.reshape(1, HEAD_DIM), k_norm.reshape(1, HEAD_DIM), cos, sin)


def _stack_heads(q):
    g = q.shape[1] // HEAD_DIM
    return jnp.concatenate([q[:, i * HEAD_DIM:(i + 1) * HEAD_DIM] for i in range(g)], axis=0)


def _store_heads(o_ref, o, tq):
    for i in range(o.shape[0] // tq):
        o_ref[:, i * HEAD_DIM:(i + 1) * HEAD_DIM] = o[i * tq:(i + 1) * tq, :].astype(o_ref.dtype)


def _softmax_chunk(s_ref, m_scr, alpha_ref, p_ref):
    rows, n_keys = s_ref.shape
    for g in range(rows // SOFTMAX_ROWS):
        rg = slice(g * SOFTMAX_ROWS, (g + 1) * SOFTMAX_ROWS)
        blocks = [s_ref[rg, j * LANES:(j + 1) * LANES] for j in range(n_keys // LANES)]
        mx = blocks[0]
        for blk in blocks[1:]:
            mx = jnp.maximum(mx, blk)
        m_prev = m_scr[rg, :]
        m_new = jnp.maximum(m_prev, jnp.max(mx, axis=-1, keepdims=True))
        alpha_ref[rg, :] = jnp.exp2(m_prev - m_new)
        for j, blk in enumerate(blocks):
            p_ref[rg, j * LANES:(j + 1) * LANES] = jnp.exp2(blk - m_new).astype(p_ref.dtype)
        m_scr[rg, :] = m_new


def _global_attn_kernel(q_ref, k_ref, v_ref, o_ref, vext_scr, m_scr, acc_scr, s_scr, p_scr, alpha_scr,
                        *, t_lat, tq, kv_chunks):
    s_tot = k_ref.shape[0]
    c_len = s_tot - t_lat
    is_ctx = pl.program_id(2) * tq >= t_lat

    @pl.when(pl.program_id(2) == 0)
    def _():
        ones_col = jnp.where(lax.broadcasted_iota(jnp.int32, (MXU_N, HEAD_DIM), 1) == 0, 1.0, 0.0)

        def fill(blk, carry):
            rows = pl.ds(pl.multiple_of(blk * MXU_N, MXU_N), MXU_N)
            vext_scr[rows, 0:HEAD_DIM] = v_ref[rows, :]
            vext_scr[rows, HEAD_DIM:2 * HEAD_DIM] = ones_col.astype(vext_scr.dtype)
            return carry

        lax.fori_loop(0, s_tot // MXU_N, fill, 0)

    q2 = _stack_heads(q_ref[...])
    m_scr[...] = jnp.full(m_scr.shape, -jnp.inf, F32)
    acc_scr[...] = jnp.zeros(acc_scr.shape, F32)

    def keys(c):
        return slice(kv_chunks[c][0], kv_chunks[c][0] + kv_chunks[c][1])

    def width(c):
        return slice(0, kv_chunks[c][1])

    def accumulate(p, v_ext, alpha):
        acc_scr[...] = jnp.concatenate([alpha, alpha], axis=1) * acc_scr[...] + _dot(p, v_ext)

    @pl.when(jnp.logical_not(is_ctx))
    def _():
        n_chunks = len(kv_chunks)
        s_scr[0, :, width(0)] = _dot_nt(q2, k_ref[keys(0), :])
        for c in range(n_chunks):
            slot = c % KV_SLOTS
            if c + 1 < n_chunks:
                s_scr[(c + 1) % KV_SLOTS, :, width(c + 1)] = _dot_nt(q2, k_ref[keys(c + 1), :])
            if c > 0:
                prev = (c - 1) % KV_SLOTS
                accumulate(p_scr[prev, :, width(c - 1)], vext_scr[keys(c - 1), :], alpha_scr[prev])
            _softmax_chunk(s_scr.at[slot, :, width(c)], m_scr, alpha_scr.at[slot], p_scr.at[slot, :, width(c)])
        last = (n_chunks - 1) % KV_SLOTS
        accumulate(p_scr[last, :, width(n_chunks - 1)], vext_scr[keys(n_chunks - 1), :], alpha_scr[last])

    @pl.when(is_ctx)
    def _():
        s_scr[0, :, 0:c_len] = _dot_nt(q2, k_ref[t_lat:s_tot, :])
        _softmax_chunk(s_scr.at[0, :, 0:c_len], m_scr, alpha_scr.at[0], p_scr.at[0, :, 0:c_len])
        accumulate(p_scr[0, :, 0:c_len], vext_scr[t_lat:s_tot, :], alpha_scr[0])

    acc = acc_scr[...]
    _store_heads(o_ref, acc[:, 0:HEAD_DIM] / acc[:, HEAD_DIM:HEAD_DIM + 1], tq)


def _kv_chunks(s_tot):
    units = s_tot // MXU_N
    edge = KV_EDGE // MXU_N
    if units <= 2 * edge:
        sizes = [1] * units
    else:
        n_mid = -(-(units - 2 * edge) // (KV_CHUNK // MXU_N))
        base, extra = divmod(units - 2 * edge, n_mid)
        sizes = [edge] + [base + (i < extra) for i in range(n_mid)] + [edge]
    chunks, start = [], 0
    for n in sizes:
        chunks.append((start * MXU_N, n * MXU_N))
        start += n
    return tuple(chunks)


def _global_attention(qkv, t_lat):
    b, s, _ = qkv.shape
    tq = ROW_TILE
    group = A_Q_HEADS // A_KV_HEADS
    k0 = A_QW // HEAD_DIM
    v0 = k0 + A_KV_HEADS
    kv_chunks = _kv_chunks(s)
    kv_chunk = max(size for _, size in kv_chunks)
    assert s - t_lat <= kv_chunk
    return pl.pallas_call(
        functools.partial(_global_attn_kernel, t_lat=t_lat, tq=tq, kv_chunks=kv_chunks),
        grid=(b, A_KV_HEADS, s // tq),
        in_specs=[
            pl.BlockSpec((None, tq, group * HEAD_DIM), lambda bi, hk, i: (bi, i, hk)),
            pl.BlockSpec((None, s, HEAD_DIM), lambda bi, hk, i: (bi, 0, k0 + hk)),
            pl.BlockSpec((None, s, HEAD_DIM), lambda bi, hk, i: (bi, 0, v0 + hk)),
        ],
        out_specs=pl.BlockSpec((None, tq, group * HEAD_DIM), lambda bi, hk, i: (bi, i, hk)),
        out_shape=jax.ShapeDtypeStruct((b, s, A_QW), MXU_DTYPE),
        scratch_shapes=[
            pltpu.VMEM((s, 2 * HEAD_DIM), MXU_DTYPE),
            pltpu.VMEM((group * tq, LANES), F32),
            pltpu.VMEM((group * tq, 2 * HEAD_DIM), F32),
            pltpu.VMEM((KV_SLOTS, group * tq, kv_chunk), F32),
            pltpu.VMEM((KV_SLOTS, group * tq, kv_chunk), MXU_DTYPE),
            pltpu.VMEM((KV_SLOTS, group * tq, LANES), F32),
        ],
        compiler_params=_cparams(("parallel", "parallel", "arbitrary"), 40),
        name="global_attention",
    )(qkv, qkv, qkv)


def _window_attn_kernel(sink_ref, q_ref, k_ref, v_ref, o_ref, *, t_lat, tq):
    s_tot = k_ref.shape[0]
    c_len = s_tot - t_lat
    band = tq + 2 * WINDOW
    q0 = pl.program_id(1) * tq
    is_ctx = q0 >= t_lat
    start = pl.multiple_of(jnp.clip(q0 - WINDOW, 0, t_lat - band), WINDOW)
    group = B_Q_HEADS // B_KV_HEADS

    delta = (lax.broadcasted_iota(jnp.int32, (tq, band), 1) + (start - q0)
             - lax.broadcasted_iota(jnp.int32, (tq, band), 0))
    reach = jnp.where(is_ctx, -1, WINDOW)
    valid = jnp.abs(delta) <= reach
    for hk in range(B_KV_HEADS):
        kv = slice(hk * HEAD_DIM, (hk + 1) * HEAD_DIM)
        q2 = _stack_heads(q_ref[:, hk * group * HEAD_DIM:(hk + 1) * group * HEAD_DIM])
        s_band = _dot_nt(q2, k_ref[pl.ds(start, band), kv])
        s_ctx = _dot_nt(q2, k_ref[pl.ds(t_lat, c_len), kv])
        p_band, p_ctx, denom = [], [], []
        for g in range(group):
            head = slice(g * tq, (g + 1) * tq)
            sb = jnp.where(valid, s_band[head, :], -jnp.inf)
            sc = s_ctx[head, :]
            sink = sink_ref[hk * group + g]
            m = jnp.maximum(jnp.maximum(jnp.max(sb, axis=-1, keepdims=True),
                                        jnp.max(sc, axis=-1, keepdims=True)), sink)
            pb = jnp.exp(sb - m)
            pc = jnp.exp(sc - m)
            denom.append(jnp.sum(pb, axis=-1, keepdims=True) + jnp.sum(pc, axis=-1, keepdims=True)
                         + jnp.exp(sink - m))
            p_band.append(pb.astype(MXU_DTYPE))
            p_ctx.append(pc.astype(MXU_DTYPE))
        o = (_dot(jnp.concatenate(p_band, axis=0), v_ref[pl.ds(start, band), kv])
             + _dot(jnp.concatenate(p_ctx, axis=0), v_ref[pl.ds(t_lat, c_len), kv]))
        o = o / jnp.concatenate(denom, axis=0)
        for g in range(group):
            cols = slice((hk * group + g) * HEAD_DIM, (hk * group + g + 1) * HEAD_DIM)
            o_ref[:, cols] = o[g * tq:(g + 1) * tq, :].astype(o_ref.dtype)


def _window_attention(qkv, sink, t_lat):
    b, s, _ = qkv.shape
    tq = ROW_TILE
    assert tq & (tq - 1) == 0 and t_lat >= tq + 2 * WINDOW
    q0 = (A_QW + 2 * A_KVW) // B_QW
    k0 = (A_QW + 2 * A_KVW + B_QW) // B_KVW
    v0 = k0 + 1
    return pl.pallas_call(
        functools.partial(_window_attn_kernel, t_lat=t_lat, tq=tq),
        grid=(b, s // tq),
        in_specs=[
            pl.BlockSpec(memory_space=pltpu.SMEM),
            pl.BlockSpec((None, tq, B_QW), lambda bi, i: (bi, i, q0)),
            pl.BlockSpec((None, s, B_KVW), lambda bi, i: (bi, 0, k0)),
            pl.BlockSpec((None, s, B_KVW), lambda bi, i: (bi, 0, v0)),
        ],
        out_specs=pl.BlockSpec((None, tq, B_QW), lambda bi, i: (bi, i, 0)),
        out_shape=jax.ShapeDtypeStruct((b, s, B_QW), MXU_DTYPE),
        compiler_params=_cparams(("parallel", "arbitrary"), 40),
        name="window_attention",
    )(sink, qkv, qkv, qkv)


def _log_sigmoid(x):
    return jnp.minimum(x, 0.0) - jnp.log(1.0 + jnp.exp(-jnp.abs(x)))


def _gla_proj_kernel(h_ref, modl_ref, modc_ref, w_ref, wa1_ref, wa2_ref, ba_ref, p_ref, ld_ref, *, t_lat):
    a = _modulated_rows(h_ref, modl_ref, modc_ref, t_lat)
    for cb in range(GLA_IN // MXU_N):
        z = _dot(a, w_ref[:, cb * MXU_N:(cb + 1) * MXU_N])
        if cb < GLA_KEY_DIM // MXU_N:
            z = z * (GLA_DK ** -0.5)
        p_ref[:, cb * MXU_N:(cb + 1) * MXU_N] = z.astype(p_ref.dtype)
    low = _dot(a, wa1_ref[...]).astype(MXU_DTYPE)
    for cb in range(2 * GLA_KEY_DIM // MXU_N):
        cols = slice(cb * MXU_N, (cb + 1) * MXU_N)
        logits = _dot(low, wa2_ref[:, cols]) + ba_ref[:, cols]
        ld_ref[:, cols] = _log_sigmoid(logits) * (LOG2_E / GLA_GATE_NORM)


def _gla_project(h, mods, layer, w_in, mixer, w_a1, w_a2, b_a, t_lat):
    b, s, d = h.shape
    tm = _proj_tile(s)
    return pl.pallas_call(
        functools.partial(_gla_proj_kernel, t_lat=t_lat),
        grid=(b, s // tm),
        in_specs=[
            pl.BlockSpec((None, tm, d), lambda bi, i: (bi, i, 0)),
            pl.BlockSpec((None, None, 6, d), lambda bi, i: (layer, bi, 0, 0)),
            pl.BlockSpec((None, None, 6, d), lambda bi, i: (layer, MOD_CTX_ROW, 0, 0)),
            pl.BlockSpec((None, d, GLA_IN), lambda bi, i: (mixer, 0, 0)),
            pl.BlockSpec((d, LANES), lambda bi, i: (0, 0)),
            pl.BlockSpec((LANES, 2 * GLA_KEY_DIM), lambda bi, i: (0, 0)),
            pl.BlockSpec((1, 2 * GLA_KEY_DIM), lambda bi, i: (0, 0)),
        ],
        out_specs=[
            pl.BlockSpec((None, tm, GLA_IN), lambda bi, i: (bi, i, 0)),
            pl.BlockSpec((None, tm, 2 * GLA_KEY_DIM), lambda bi, i: (bi, i, 0)),
        ],
        out_shape=[
            jax.ShapeDtypeStruct((b, s, GLA_IN), MXU_DTYPE),
            jax.ShapeDtypeStruct((b, s, 2 * GLA_KEY_DIM), F32),
        ],
        compiler_params=_cparams(("parallel", "parallel"), 48),
        name="gla_project",
    )(h, mods, mods, w_in, w_a1, w_a2, b_a)


def _cumsum_rows(x):
    rows, cols = x.shape
    sub = lax.broadcasted_iota(jnp.int32, x.shape, 0) & (SUBLANES - 1)
    shift = 1
    while shift < SUBLANES:
        x = x + jnp.where(sub >= shift, pltpu.roll(x, shift, 0), 0.0)
        shift *= 2
    x3 = x.reshape(rows // SUBLANES, SUBLANES, cols)
    carry = jnp.zeros((1, cols), F32)
    tiles = []
    for i in range(rows // SUBLANES):
        tile = x3[i] + carry
        tiles.append(tile)
        carry = tile[SUBLANES - 1:SUBLANES, :]
    return jnp.concatenate(tiles, axis=0)


def _block_anchor(p, n):
    rows, cols = p.shape
    half = n // 2
    if n >= SUBLANES:
        p3 = p.reshape(rows // n, n, cols)
        return jnp.broadcast_to(p3[:, half - 1:half, :], p3.shape).reshape(rows, cols)
    if n == 2:
        odd = (lax.broadcasted_iota(jnp.int32, p.shape, 0) & 1) == 1
        return jnp.where(odd, pltpu.roll(p, 1, 0), p)
    p3 = p.reshape(rows // SUBLANES, SUBLANES, cols)
    sub = lax.broadcasted_iota(jnp.int32, p3.shape, 1)
    a = None
    for b0 in range(0, SUBLANES, n):
        row = jnp.broadcast_to(p3[:, b0 + half - 1:b0 + half, :], p3.shape)
        a = row if a is None else jnp.where(sub >= b0, row, a)
    return a.reshape(rows, cols)


def _gla_chunk(q, k, v, ld, sign_ref, st_ref, reverse):
    chunk = q.shape[0]
    p_inc = _cumsum_rows(ld)
    p_tot = p_inc[chunk - 1:chunk, :]
    px = p_inc - ld if reverse else p_inc
    qf = q.astype(F32)
    kf = k.astype(F32)

    def decayed(x, exponent):
        return (x * jnp.exp2(exponent)).astype(MXU_DTYPE)

    row = lax.broadcasted_iota(jnp.int32, (chunk, chunk), 0)
    col = lax.broadcasted_iota(jnp.int32, (chunk, chunk), 1)
    differ = row ^ col
    att = jnp.broadcast_to(jnp.sum(qf * kf, axis=-1, keepdims=True), (chunk, chunk))
    n = 2
    while n <= chunk:
        level = n.bit_length() - 2
        exponent = (px - _block_anchor(p_inc, n)) * sign_ref[level]
        s_lvl = _dot_nt(decayed(qf, exponent), decayed(kf, exponent))
        att = jnp.where((differ >> level) == 1, s_lvl, att)
        n *= 2
    att = jnp.where(row <= col if reverse else row >= col, att, 0.0)

    if reverse:
        qd = decayed(qf, p_tot - px)
        kd = decayed(kf, px)
    else:
        qd = decayed(qf, px)
        kd = decayed(kf, p_tot - px)
    state = st_ref[...]
    o = _dot(att.astype(MXU_DTYPE), v) + _dot_nt(qd, state.astype(MXU_DTYPE))
    st_ref[...] = state * jnp.exp2(p_tot) + _dot_tn(v, kd)
    return o


def _gla_scan_kernel(sign_ref, qf_ref, kf_ref, vf_ref, ldf_ref, qr_ref, kr_ref, vr_ref, ldr_ref,
                     of_ref, or_ref, st_ref):
    @pl.when(pl.program_id(1) == 0)
    def _():
        st_ref[...] = jnp.zeros(st_ref.shape, F32)

    operands = ((qf_ref, kf_ref, vf_ref, ldf_ref, of_ref), (qr_ref, kr_ref, vr_ref, ldr_ref, or_ref))
    for direction, (q_ref, k_ref, v_ref, ld_ref, o_ref) in enumerate(operands):
        for hd in range(GLA_HEADS):
            kc = slice(hd * GLA_DK, (hd + 1) * GLA_DK)
            vc = slice(hd * GLA_DV, (hd + 1) * GLA_DV)
            o = _gla_chunk(q_ref[:, kc], k_ref[:, kc], v_ref[:, vc], ld_ref[:, kc], sign_ref,
                           st_ref.at[direction, hd], reverse=direction == 1)
            o_ref[:, vc] = o.astype(o_ref.dtype)


def _gla_scan(proj, ld, t_lat):
    b, s, _ = proj.shape
    ch = GLA_CHUNK
    n_chunks = s // ch
    lat_chunks = t_lat // ch

    def fwd(c):
        return (c + lat_chunks) % n_chunks

    def rev(c):
        return n_chunks - 1 - c

    def specs(chunk_of, direction):
        return [
            pl.BlockSpec((None, ch, GLA_KEY_DIM), lambda bi, c: (bi, chunk_of(c), 0)),
            pl.BlockSpec((None, ch, GLA_KEY_DIM), lambda bi, c: (bi, chunk_of(c), 1)),
            pl.BlockSpec((None, ch, GLA_VAL_DIM), lambda bi, c: (bi, chunk_of(c), 1)),
            pl.BlockSpec((None, ch, GLA_KEY_DIM), lambda bi, c: (bi, chunk_of(c), direction)),
        ]

    levels = ch.bit_length() - 1
    half_bit = 1 << jnp.arange(levels, dtype=jnp.int32)[:, None, None]
    in_upper = (jnp.arange(ch, dtype=jnp.int32)[None, :, None] & half_bit) != 0
    sign = jnp.broadcast_to(jnp.where(in_upper, 1.0, -1.0).astype(F32), (levels, ch, GLA_DK))
    o_shape = jax.ShapeDtypeStruct((b, s, GLA_VAL_DIM), F32)
    return pl.pallas_call(
        _gla_scan_kernel,
        grid=(b, n_chunks),
        in_specs=[pl.BlockSpec((levels, ch, GLA_DK), lambda bi, c: (0, 0, 0))] + specs(fwd, 0) + specs(rev, 1),
        out_specs=[
            pl.BlockSpec((None, ch, GLA_VAL_DIM), lambda bi, c: (bi, fwd(c), 0)),
            pl.BlockSpec((None, ch, GLA_VAL_DIM), lambda bi, c: (bi, rev(c), 0)),
        ],
        out_shape=[o_shape, o_shape],
        scratch_shapes=[pltpu.VMEM((2, GLA_HEADS, GLA_DV, GLA_DK), F32)],
        compiler_params=_cparams(("parallel", "arbitrary"), 32),
        name="gla_scan",
    )(sign, proj, proj, proj, ld, proj, proj, proj, ld)


def _with_halo(ref, next_ref):
    return jnp.concatenate([ref[...], next_ref[...]], axis=0)


def _mixer_out_norm(x, h_refs, mod_ref, wout_ref, lng_ref, lnb_ref, h1_scr, alpha):
    tm = h_refs[0].shape[0]

    @pl.when(pl.program_id(1) == 0)
    def _():
        h1_scr[0:HALO, :] = jnp.zeros((HALO, h1_scr.shape[1]), F32)

    @pl.when(pl.program_id(1) > 0)
    def _():
        h1_scr[0:HALO, :] = h1_scr[tm:tm + HALO, :]

    y = _dot(x, wout_ref[...])
    h1_scr[HALO:, :] = _residual_layer_norm(_with_halo(*h_refs), y, mod_ref[2:3, :], lng_ref[...], lnb_ref[...],
                                            alpha)


def _attn_ffn_kernel(oa_ref, oan_ref, ob_ref, obn_ref, h_ref, hn_ref, mod_ref,
                     wout_ref, ln1g_ref, ln1b_ref, wup_ref, cw_ref, cb_ref, wdn_ref, ln2g_ref, ln2b_ref,
                     out_ref, h1_scr, a_scr, u_scr, act_scr, *, t_lat, s_tot, alpha):
    x = jnp.concatenate([_with_halo(oa_ref, oan_ref), _with_halo(ob_ref, obn_ref)], axis=1)
    _mixer_out_norm(x, (h_ref, hn_ref), mod_ref, wout_ref, ln1g_ref, ln1b_ref, h1_scr, alpha)
    _conv_ffn_norm(h1_scr, mod_ref, wup_ref, cw_ref, cb_ref, wdn_ref, ln2g_ref, ln2b_ref, out_ref,
                   a_scr, u_scr, act_scr, t_lat=t_lat, s_tot=s_tot, alpha=alpha)


def _gla_ffn_kernel(of_ref, ofn_ref, or_ref, orn_ref, g_ref, gn_ref,
                    h_ref, hn_ref, mod_ref, hnorm_ref, wout_ref, ln1g_ref, ln1b_ref,
                    wup_ref, cw_ref, cb_ref, wdn_ref, ln2g_ref, ln2b_ref,
                    out_ref, h1_scr, a_scr, u_scr, act_scr, *, t_lat, s_tot, alpha):
    o = _with_halo(of_ref, ofn_ref) + _with_halo(or_ref, orn_ref)
    gate = _silu(_with_halo(g_ref, gn_ref).astype(F32))
    heads = []
    for hd in range(GLA_HEADS):
        cols = slice(hd * GLA_DV, (hd + 1) * GLA_DV)
        oh = o[:, cols]
        oh = oh * lax.rsqrt(jnp.mean(oh * oh, axis=-1, keepdims=True) + EPS) * hnorm_ref[...]
        heads.append((oh * gate[:, cols]).astype(MXU_DTYPE))
    x = jnp.concatenate(heads, axis=1)
    _mixer_out_norm(x, (h_ref, hn_ref), mod_ref, wout_ref, ln1g_ref, ln1b_ref, h1_scr, alpha)
    _conv_ffn_norm(h1_scr, mod_ref, wup_ref, cw_ref, cb_ref, wdn_ref, ln2g_ref, ln2b_ref, out_ref,
                   a_scr, u_scr, act_scr, t_lat=t_lat, s_tot=s_tot, alpha=alpha)


def _conv_ffn_norm(h1_scr, mod_ref, wup_ref, cw_ref, cb_ref, wdn_ref, lng_ref, lnb_ref,
                   out_ref, a_scr, u_scr, act_scr, *, t_lat, s_tot, alpha):
    tm = out_ref.shape[0]
    d_ff = wdn_ref.shape[0]
    row0 = pl.program_id(1) * tm
    at_start = (row0 == 0) | (row0 == t_lat)
    at_end = (row0 + tm == t_lat) | (row0 + tm == s_tot)
    shift = mod_ref[3:4, :]
    scale = 1.0 + mod_ref[4:5, :]
    a_all = h1_scr[...] * scale + shift
    a_scr[0:HALO, :] = jnp.where(at_start, 0.0, a_all[0:HALO, :]).astype(a_scr.dtype)
    a_scr[HALO:HALO + tm, :] = a_all[HALO:HALO + tm, :].astype(a_scr.dtype)
    a_scr[HALO + tm:2 * HALO + tm, :] = jnp.where(at_end, 0.0, a_all[HALO + tm:2 * HALO + tm, :]).astype(a_scr.dtype)
    a = a_scr[...]

    def conv(u_ref, c0):
        cols = slice(c0, c0 + FF_CHUNK)
        acc = cb_ref[:, cols] + cw_ref[0:1, cols] * u_ref[pl.ds(HALO - 1, tm), :]
        for j in range(1, CONV_W):
            acc = acc + cw_ref[j:j + 1, cols] * u_ref[pl.ds(HALO - 1 + j, tm), :]
        return acc

    def up_project(c):
        g0 = c * FF_CHUNK
        v0 = d_ff + c * FF_CHUNK
        u_scr[c % U_SLOTS, 0] = _dot(a, wup_ref[:, g0:g0 + FF_CHUNK])
        u_scr[c % U_SLOTS, 1] = _dot(a, wup_ref[:, v0:v0 + FF_CHUNK])

    def down_project(c):
        return _dot(act_scr[c % ACT_SLOTS], wdn_ref[c * FF_CHUNK:(c + 1) * FF_CHUNK, :])

    n_chunks = d_ff // FF_CHUNK
    ahead = U_SLOTS - 1
    y = jnp.zeros(out_ref.shape, F32)
    for c in range(min(ahead, n_chunks)):
        up_project(c)
    for c in range(n_chunks):
        if c + ahead < n_chunks:
            up_project(c + ahead)
        if c > 0:
            y = y + down_project(c - 1)
        g0 = c * FF_CHUNK
        act = _silu(conv(u_scr.at[c % U_SLOTS, 0], g0)) * conv(u_scr.at[c % U_SLOTS, 1], d_ff + g0)
        act_scr[c % ACT_SLOTS] = act.astype(act_scr.dtype)
    y = y + down_project(n_chunks - 1)
    h1 = h1_scr[HALO:HALO + tm, :]
    out_ref[...] = _residual_layer_norm(h1, y, mod_ref[5:6, :], lng_ref[...], lnb_ref[...], alpha)


def _mixer_ffn(kernel_fn, name, mixer_ins, h, mods, layer, extra_params, w_out, mixer, ln_g, ln_b,
               w_up, conv_w, conv_b, w_down, t_lat, n_rows, alpha):
    b, s, d = h.shape
    tm = ROW_TILE
    d_ff = w_down.shape[1]
    row = _mod_row(t_lat, tm)
    hb = tm // HALO
    n_halo = s // HALO

    def halo_specs(cols, cb):
        return [
            pl.BlockSpec((None, tm, cols), lambda bi, i: (bi, i, cb)),
            pl.BlockSpec((None, HALO, cols), lambda bi, i: (bi, jnp.minimum((i + 1) * hb, n_halo - 1), cb)),
        ]

    row_ins = list(mixer_ins) + [(h, d, 0)]
    in_specs, operands = [], []
    for arr, cols, cb in row_ins:
        in_specs += halo_specs(cols, cb)
        operands += [arr, arr]
    in_specs.append(pl.BlockSpec((None, None, 6, d), lambda bi, i: (layer, row(bi, i), 0, 0)))
    operands.append(mods)
    for p in extra_params:
        in_specs.append(pl.BlockSpec(p.shape, lambda bi, i: (0, 0)))
        operands.append(p)
    in_specs += [
        pl.BlockSpec((None,) + w_out.shape[1:], lambda bi, i: (mixer, 0, 0)),
        pl.BlockSpec((None, 1, d), lambda bi, i: (layer, 0, 0)),
        pl.BlockSpec((None, 1, d), lambda bi, i: (layer, 0, 0)),
        pl.BlockSpec((None, d, 2 * d_ff), lambda bi, i: (layer, 0, 0)),
        pl.BlockSpec((None, CONV_W, 2 * d_ff), lambda bi, i: (layer, 0, 0)),
        pl.BlockSpec((None, 1, 2 * d_ff), lambda bi, i: (layer, 0, 0)),
        pl.BlockSpec((None, d_ff, d), lambda bi, i: (layer, 0, 0)),
        pl.BlockSpec((None, 1, d), lambda bi, i: (layer, 0, 0)),
        pl.BlockSpec((None, 1, d), lambda bi, i: (layer, 0, 0)),
    ]
    operands += [w_out, ln_g[:, 0:1, :], ln_b[:, 0:1, :], w_up, conv_w, conv_b.reshape(-1, 1, 2 * d_ff), w_down,
                 ln_g[:, 1:2, :], ln_b[:, 1:2, :]]
    return pl.pallas_call(
        functools.partial(kernel_fn, t_lat=t_lat, s_tot=s, alpha=alpha),
        grid=(b, n_rows // tm),
        in_specs=in_specs,
        out_specs=pl.BlockSpec((None, tm, d), lambda bi, i: (bi, i, 0)),
        out_shape=jax.ShapeDtypeStruct((b, n_rows, d), F32),
        scratch_shapes=[
            pltpu.VMEM((tm + 2 * HALO, d), F32),
            pltpu.VMEM((tm + 2 * HALO, d), MXU_DTYPE),
            pltpu.VMEM((U_SLOTS, 2, tm + 2 * HALO, FF_CHUNK), F32),
            pltpu.VMEM((ACT_SLOTS, tm, FF_CHUNK), MXU_DTYPE),
        ],
        compiler_params=_cparams(("parallel", "arbitrary"), 60),
        name=name,
    )(*operands)


def _rope_tables(t_lat, c_len):
    rows = t_lat // GRID_W
    row = jnp.repeat(jnp.arange(rows, dtype=F32), GRID_W)
    col = jnp.tile(jnp.arange(GRID_W, dtype=F32), rows)
    inv_freq = jnp.power(ROPE_THETA, -jnp.arange(ROPE_AXIS_DIM // 2, dtype=F32) * 2.0 / ROPE_AXIS_DIM)
    ang = jnp.concatenate([row[:, None] * inv_freq, col[:, None] * inv_freq], axis=-1)
    ang = jnp.concatenate([ang, jnp.zeros((c_len, HEAD_DIM // 2), F32)], axis=0)
    cos, sin = jnp.cos(ang), jnp.sin(ang)
    return jnp.concatenate([cos, cos], axis=-1), jnp.concatenate([-sin, sin], axis=-1)


def kernel(x, c, ctx, c_ctx, ada_w, ada_b, ln_g, ln_b, ffn_w_up, ffn_conv_w, ffn_conv_b, ffn_w_down, attn_w_in, attn_q_norm, attn_k_norm, attn_sink, attn_w_out, gla_w_in, gla_w_a1, gla_w_a2, gla_b_a, gla_head_norm, gla_w_out):
    b, t_lat, d = x.shape
    c_len = ctx.shape[1]
    s = t_lat + c_len
    depth = ada_w.shape[0]
    assert b < MOD_CTX_ROW and t_lat % ROW_TILE == 0 and c_len % ROW_TILE == 0
    assert t_lat % GLA_CHUNK == 0 and c_len % GLA_CHUNK == 0 and s % MXU_N == 0
    alpha = (2 * depth) ** 0.25

    cond = jnp.zeros((MOD_ROWS, d), F32).at[:b].set(c).at[MOD_CTX_ROW].set(c_ctx)
    mods = _modulation(cond, ada_w, ada_b).reshape(depth, MOD_ROWS, 6, d)
    cos, sin = _rope_tables(t_lat, c_len)
    h = jnp.concatenate([x, ctx], axis=1)
    attn_w_in_c, attn_w_out_c = attn_w_in.astype(MXU_DTYPE), attn_w_out.astype(MXU_DTYPE)
    gla_w_in_c, gla_w_out_c = gla_w_in.astype(MXU_DTYPE), gla_w_out.astype(MXU_DTYPE)
    ffn_w_up_c, ffn_w_down_c = ffn_w_up.astype(MXU_DTYPE), ffn_w_down.astype(MXU_DTYPE)

    for i in range(depth):
        last = i == depth - 1
        n_rows = t_lat if last else s
        j = i // 2
        if i % 2 == 0:
            qkv = _attn_project(h, mods, i, attn_w_in_c, j, attn_q_norm[j], attn_k_norm[j],
                                cos, sin, t_lat)
            o_a = _global_attention(qkv, t_lat)
            o_b = _window_attention(qkv, attn_sink[j], t_lat)
            h = _mixer_ffn(_attn_ffn_kernel, "attn_out_ffn", [(o_a, A_QW, 0), (o_b, B_QW, 0)], h, mods, i, [],
                           attn_w_out_c, j, ln_g, ln_b, ffn_w_up_c, ffn_conv_w, ffn_conv_b, ffn_w_down_c,
                           t_lat, n_rows, alpha)
        else:
            w_a1 = jnp.concatenate([gla_w_a1[j, 0], gla_w_a1[j, 1]], axis=1)
            w_a1 = jnp.pad(w_a1, ((0, 0), (0, LANES - 2 * GLA_GATE_RANK))).astype(MXU_DTYPE)
            w_a2 = jnp.zeros((LANES, 2 * GLA_KEY_DIM), F32)
            w_a2 = w_a2.at[:GLA_GATE_RANK, :GLA_KEY_DIM].set(gla_w_a2[j, 0])
            w_a2 = w_a2.at[GLA_GATE_RANK:2 * GLA_GATE_RANK, GLA_KEY_DIM:].set(gla_w_a2[j, 1]).astype(MXU_DTYPE)
            proj, ld = _gla_project(h, mods, i, gla_w_in_c, j, w_a1, w_a2,
                                    gla_b_a[j].reshape(1, 2 * GLA_KEY_DIM), t_lat)
            o_f, o_r = _gla_scan(proj, ld, t_lat)
            g_blk = (2 * GLA_KEY_DIM + GLA_VAL_DIM) // GLA_VAL_DIM
            h = _mixer_ffn(_gla_ffn_kernel, "gla_out_ffn",
                           [(o_f, GLA_VAL_DIM, 0), (o_r, GLA_VAL_DIM, 0), (proj, GLA_VAL_DIM, g_blk)], h, mods, i,
                           [gla_head_norm[j].reshape(1, GLA_DV)], gla_w_out_c, j, ln_g, ln_b,
                           ffn_w_up_c, ffn_conv_w, ffn_conv_b, ffn_w_down_c, t_lat, n_rows, alpha)
    return h
```

```python
import functools

import jax
import jax.numpy as jnp
from jax import lax
from jax.experimental import pallas as pl
from jax.experimental.pallas import tpu as pltpu

GRID_W = 64
HEAD_DIM = 128
A_Q_HEADS = 4
A_KV_HEADS = 2
B_Q_HEADS = 4
B_KV_HEADS = 2
A_QW = A_Q_HEADS * HEAD_DIM
A_KVW = A_KV_HEADS * HEAD_DIM
B_QW = B_Q_HEADS * HEAD_DIM
B_KVW = B_KV_HEADS * HEAD_DIM
ATTN_IN = A_QW + 2 * A_KVW + B_QW + 2 * B_KVW
ATTN_SCALE = HEAD_DIM ** -0.5
WINDOW = 128
ROPE_THETA = 10000.0
ROPE_AXIS_DIM = HEAD_DIM // 2
GLA_HEADS = 4
GLA_DK = 128
GLA_DV = 256
GLA_KEY_DIM = GLA_HEADS * GLA_DK
GLA_VAL_DIM = GLA_HEADS * GLA_DV
GLA_IN = 2 * GLA_KEY_DIM + 2 * GLA_VAL_DIM
GLA_GATE_RANK = 16
GLA_GATE_NORM = 16.0
CONV_W = 3
EPS = 1e-6

LANES = 128
SUBLANES = 8
MXU_N = 256
VMEM_BYTES = 64 * 2 ** 20

MXU_DTYPE = jnp.bfloat16
F32 = jnp.float32

ROW_TILE = 256
PROJ_TILE = 768
HALO = 16
FF_CHUNK = 256
U_SLOTS = 3
ACT_SLOTS = 2
KV_CHUNK = 2816
KV_EDGE = 512
KV_SLOTS = 2
SOFTMAX_ROWS = 16
LOG2_E = 1.4426950408889634
GLA_CHUNK = 128
MOD_COLS = 1536


def _cparams(sem, vmem_mib):
    return pltpu.CompilerParams(dimension_semantics=sem, vmem_limit_bytes=vmem_mib * 2 ** 20)


def _dot(a, b):
    return jnp.dot(a, b, preferred_element_type=F32)


def _dot_nt(a, b):
    return lax.dot_general(a, b, (((1,), (1,)), ((), ())), preferred_element_type=F32)


def _dot_tn(a, b):
    return lax.dot_general(a, b, (((0,), (0,)), ((), ())), preferred_element_type=F32)


def _silu(x):
    return x / (1.0 + jnp.exp(-x))


def _residual_layer_norm(h, y, gate, ln_g, ln_b, alpha):
    r = alpha * h + gate * y
    mu = jnp.mean(r, axis=-1, keepdims=True)
    d = r - mu
    var = jnp.mean(d * d, axis=-1, keepdims=True)
    return d * lax.rsqrt(var + EPS) * ln_g + ln_b


def _modulated_rows(h_ref, modl_ref, modc_ref, t_lat):
    tm = h_ref.shape[0]
    rows = pl.program_id(1) * tm + lax.broadcasted_iota(jnp.int32, (tm, 1), 0)
    is_ctx = rows >= t_lat
    scale = 1.0 + jnp.where(is_ctx, modc_ref[1:2, :], modl_ref[1:2, :])
    shift = jnp.where(is_ctx, modc_ref[0:1, :], modl_ref[0:1, :])
    return (h_ref[...] * scale + shift).astype(MXU_DTYPE)


def _proj_tile(s_tot):
    return max(n for n in range(ROW_TILE, PROJ_TILE + 1, ROW_TILE) if s_tot % n == 0)


def _mod_row(t_lat, tm):
    n_lat = t_lat // tm

    def row(b, i):
        return jnp.where(i < n_lat, b, MOD_CTX_ROW)
    return row


MOD_ROWS = 8
MOD_CTX_ROW = MOD_ROWS - 1


def _mod_kernel(cond_ref, w_ref, b_ref, o_ref):
    a = _silu(cond_ref[...]).astype(MXU_DTYPE)
    o_ref[...] = _dot(a, w_ref[...].astype(MXU_DTYPE)) + b_ref[...]


def _modulation(cond, ada_w, ada_b):
    depth, d, n = ada_w.shape
    return pl.pallas_call(
        _mod_kernel,
        grid=(depth, n // MOD_COLS),
        in_specs=[
            pl.BlockSpec((MOD_ROWS, d), lambda l, j: (0, 0)),
            pl.BlockSpec((None, d, MOD_COLS), lambda l, j: (l, 0, j)),
            pl.BlockSpec((None, 1, MOD_COLS), lambda l, j: (l, 0, j)),
        ],
        out_specs=pl.BlockSpec((None, MOD_ROWS, MOD_COLS), lambda l, j: (l, 0, j)),
        out_shape=jax.ShapeDtypeStruct((depth, MOD_ROWS, n), F32),
        compiler_params=_cparams(("parallel", "parallel"), 32),
        name="ada_modulation",
    )(cond, ada_w, ada_b.reshape(depth, 1, n))


def _attn_proj_kernel(h_ref, modl_ref, modc_ref, w_ref, qn_ref, kn_ref, cos_ref, sin_ref, o_ref, *, t_lat):
    a = _modulated_rows(h_ref, modl_ref, modc_ref, t_lat)
    cos = cos_ref[...]
    sin = sin_ref[...]

    def rms(z, g):
        return z * lax.rsqrt(jnp.mean(z * z, axis=-1, keepdims=True) + EPS) * g

    def rope(z):
        return z * cos + pltpu.roll(z, HEAD_DIM // 2, 1) * sin

    a_k0 = A_QW // HEAD_DIM
    a_v0 = a_k0 + A_KV_HEADS
    b_q0 = a_v0 + A_KV_HEADS
    b_k0 = b_q0 + B_Q_HEADS
    b_v0 = b_k0 + B_KV_HEADS
    heads_per_dot = MXU_N // HEAD_DIM
    for cb in range(ATTN_IN // MXU_N):
        z2 = _dot(a, w_ref[:, cb * MXU_N:(cb + 1) * MXU_N])
        for half in range(heads_per_dot):
            hb = cb * heads_per_dot + half
            z = z2[:, half * HEAD_DIM:(half + 1) * HEAD_DIM]
            if hb < a_k0:
                z = rope(rms(z, qn_ref[...])) * (ATTN_SCALE * LOG2_E)
            elif hb < a_v0:
                z = rope(rms(z, kn_ref[...]))
            elif hb < b_q0:
                pass
            elif hb < b_k0:
                z = rope(z) * ATTN_SCALE
            elif hb < b_v0:
                z = rope(z)
            o_ref[:, hb * HEAD_DIM:(hb + 1) * HEAD_DIM] = z.astype(o_ref.dtype)


def _attn_project(h, mods, layer, w_in, mixer, q_norm, k_norm, cos, sin, t_lat):
    b, s, d = h.shape
    tm = ROW_TILE
    return pl.pallas_call(
        functools.partial(_attn_proj_kernel, t_lat=t_lat),
        grid=(b, s // tm),
        in_specs=[
            pl.BlockSpec((None, tm, d), lambda bi, i: (bi, i, 0)),
            pl.BlockSpec((None, None, 6, d), lambda bi, i: (layer, bi, 0, 0)),
            pl.BlockSpec((None, None, 6, d), lambda bi, i: (layer, MOD_CTX_ROW, 0, 0)),
            pl.BlockSpec((None, d, ATTN_IN), lambda bi, i: (mixer, 0, 0)),
            pl.BlockSpec((1, HEAD_DIM), lambda bi, i: (0, 0)),
            pl.BlockSpec((1, HEAD_DIM), lambda bi, i: (0, 0)),
            pl.BlockSpec((tm, HEAD_DIM), lambda bi, i: (i, 0)),
            pl.BlockSpec((tm, HEAD_DIM), lambda bi, i: (i, 0)),
        ],
        out_specs=pl.BlockSpec((None, tm, ATTN_IN), lambda bi, i: (bi, i, 0)),
        out_shape=jax.ShapeDtypeStruct((b, s, ATTN_IN), MXU_DTYPE),
        compiler_params=_cparams(("parallel", "parallel"), 40),
        name="attn_project",
    )(h, mods, mods, w_in, q_norm.reshape(1, HEAD_DIM), k_norm.reshape(1, HEAD_DIM), cos, sin)


def _stack_heads(q):
    g = q.shape[1] // HEAD_DIM
    return jnp.concatenate([q[:, i * HEAD_DIM:(i + 1) * HEAD_DIM] for i in range(g)], axis=0)


def _store_heads(o_ref, o, tq):
    for i in range(o.shape[0] // tq):
        o_ref[:, i * HEAD_DIM:(i + 1) * HEAD_DIM] = o[i * tq:(i + 1) * tq, :].astype(o_ref.dtype)


def _softmax_chunk(s_ref, m_scr, alpha_ref, p_ref):
    rows, n_keys = s_ref.shape
    for g in range(rows // SOFTMAX_ROWS):
        rg = slice(g * SOFTMAX_ROWS, (g + 1) * SOFTMAX_ROWS)
        blocks = [s_ref[rg, j * LANES:(j + 1) * LANES] for j in range(n_keys // LANES)]
        mx = blocks[0]
        for blk in blocks[1:]:
            mx = jnp.maximum(mx, blk)
        m_prev = m_scr[rg, :]
        m_new = jnp.maximum(m_prev, jnp.max(mx, axis=-1, keepdims=True))
        alpha_ref[rg, :] = jnp.exp2(m_prev - m_new)
        for j, blk in enumerate(blocks):
            p_ref[rg, j * LANES:(j + 1) * LANES] = jnp.exp2(blk - m_new).astype(p_ref.dtype)
        m_scr[rg, :] = m_new


def _global_attn_kernel(q_ref, k_ref, v_ref, o_ref, vext_scr, m_scr, acc_scr, s_scr, p_scr, alpha_scr,
                        *, t_lat, tq, kv_chunks):
    s_tot = k_ref.shape[0]
    c_len = s_tot - t_lat
    is_ctx = pl.program_id(2) * tq >= t_lat

    @pl.when(pl.program_id(2) == 0)
    def _():
        ones_col = jnp.where(lax.broadcasted_iota(jnp.int32, (MXU_N, HEAD_DIM), 1) == 0, 1.0, 0.0)

        def fill(blk, carry):
            rows = pl.ds(pl.multiple_of(blk * MXU_N, MXU_N), MXU_N)
            vext_scr[rows, 0:HEAD_DIM] = v_ref[rows, :]
            vext_scr[rows, HEAD_DIM:2 * HEAD_DIM] = ones_col.astype(vext_scr.dtype)
            return carry

        lax.fori_loop(0, s_tot // MXU_N, fill, 0)

    q2 = _stack_heads(q_ref[...])
    m_scr[...] = jnp.full(m_scr.shape, -jnp.inf, F32)
    acc_scr[...] = jnp.zeros(acc_scr.shape, F32)

    def keys(c):
        return slice(kv_chunks[c][0], kv_chunks[c][0] + kv_chunks[c][1])

    def width(c):
        return slice(0, kv_chunks[c][1])

    def accumulate(p, v_ext, alpha):
        acc_scr[...] = jnp.concatenate([alpha, alpha], axis=1) * acc_scr[...] + _dot(p, v_ext)

    @pl.when(jnp.logical_not(is_ctx))
    def _():
        n_chunks = len(kv_chunks)
        s_scr[0, :, width(0)] = _dot_nt(q2, k_ref[keys(0), :])
        for c in range(n_chunks):
            slot = c % KV_SLOTS
            if c + 1 < n_chunks:
                s_scr[(c + 1) % KV_SLOTS, :, width(c + 1)] = _dot_nt(q2, k_ref[keys(c + 1), :])
            if c > 0:
                prev = (c - 1) % KV_SLOTS
                accumulate(p_scr[prev, :, width(c - 1)], vext_scr[keys(c - 1), :], alpha_scr[prev])
            _softmax_chunk(s_scr.at[slot, :, width(c)], m_scr, alpha_scr.at[slot], p_scr.at[slot, :, width(c)])
        last = (n_chunks - 1) % KV_SLOTS
        accumulate(p_scr[last, :, width(n_chunks - 1)], vext_scr[keys(n_chunks - 1), :], alpha_scr[last])

    @pl.when(is_ctx)
    def _():
        s_scr[0, :, 0:c_len] = _dot_nt(q2, k_ref[t_lat:s_tot, :])
        _softmax_chunk(s_scr.at[0, :, 0:c_len], m_scr, alpha_scr.at[0], p_scr.at[0, :, 0:c_len])
        accumulate(p_scr[0, :, 0:c_len], vext_scr[t_lat:s_tot, :], alpha_scr[0])

    acc = acc_scr[...]
    _store_heads(o_ref, acc[:, 0:HEAD_DIM] / acc[:, HEAD_DIM:HEAD_DIM + 1], tq)


def _kv_chunks(s_tot):
    units = s_tot // MXU_N
    edge = KV_EDGE // MXU_N
    if units <= 2 * edge:
        sizes = [1] * units
    else:
        n_mid = -(-(units - 2 * edge) // (KV_CHUNK // MXU_N))
        base, extra = divmod(units - 2 * edge, n_mid)
        sizes = [edge] + [base + (i < extra) for i in range(n_mid)] + [edge]
    chunks, start = [], 0
    for n in sizes:
        chunks.append((start * MXU_N, n * MXU_N))
        start += n
    return tuple(chunks)


def _global_attention(qkv, t_lat):
    b, s, _ = qkv.shape
    tq = ROW_TILE
    group = A_Q_HEADS // A_KV_HEADS
    k0 = A_QW // HEAD_DIM
    v0 = k0 + A_KV_HEADS
    kv_chunks = _kv_chunks(s)
    kv_chunk = max(size for _, size in kv_chunks)
    assert s - t_lat <= kv_chunk
    return pl.pallas_call(
        functools.partial(_global_attn_kernel, t_lat=t_lat, tq=tq, kv_chunks=kv_chunks),
        grid=(b, A_KV_HEADS, s // tq),
        in_specs=[
            pl.BlockSpec((None, tq, group * HEAD_DIM), lambda bi, hk, i: (bi, i, hk)),
            pl.BlockSpec((None, s, HEAD_DIM), lambda bi, hk, i: (bi, 0, k0 + hk)),
            pl.BlockSpec((None, s, HEAD_DIM), lambda bi, hk, i: (bi, 0, v0 + hk)),
        ],
        out_specs=pl.BlockSpec((None, tq, group * HEAD_DIM), lambda bi, hk, i: (bi, i, hk)),
        out_shape=jax.ShapeDtypeStruct((b, s, A_QW), MXU_DTYPE),
        scratch_shapes=[
            pltpu.VMEM((s, 2 * HEAD_DIM), MXU_DTYPE),
            pltpu.VMEM((group * tq, LANES), F32),
            pltpu.VMEM((group * tq, 2 * HEAD_DIM), F32),
            pltpu.VMEM((KV_SLOTS, group * tq, kv_chunk), F32),
            pltpu.VMEM((KV_SLOTS, group * tq, kv_chunk), MXU_DTYPE),
            pltpu.VMEM((KV_SLOTS, group * tq, LANES), F32),
        ],
        compiler_params=_cparams(("parallel", "parallel", "arbitrary"), 40),
        name="global_attention",
    )(qkv, qkv, qkv)


def _window_attn_kernel(sink_ref, q_ref, k_ref, v_ref, o_ref, *, t_lat, tq):
    s_tot = k_ref.shape[0]
    c_len = s_tot - t_lat
    band = tq + 2 * WINDOW
    q0 = pl.program_id(1) * tq
    is_ctx = q0 >= t_lat
    start = pl.multiple_of(jnp.clip(q0 - WINDOW, 0, t_lat - band), WINDOW)
    group = B_Q_HEADS // B_KV_HEADS

    delta = (lax.broadcasted_iota(jnp.int32, (tq, band), 1) + (start - q0)
             - lax.broadcasted_iota(jnp.int32, (tq, band), 0))
    reach = jnp.where(is_ctx, -1, WINDOW)
    valid = jnp.abs(delta) <= reach
    for hk in range(B_KV_HEADS):
        kv = slice(hk * HEAD_DIM, (hk + 1) * HEAD_DIM)
        q2 = _stack_heads(q_ref[:, hk * group * HEAD_DIM:(hk + 1) * group * HEAD_DIM])
        s_band = _dot_nt(q2, k_ref[pl.ds(start, band), kv])
        s_ctx = _dot_nt(q2, k_ref[pl.ds(t_lat, c_len), kv])
        p_band, p_ctx, denom = [], [], []
        for g in range(group):
            head = slice(g * tq, (g + 1) * tq)
            sb = jnp.where(valid, s_band[head, :], -jnp.inf)
            sc = s_ctx[head, :]
            sink = sink_ref[hk * group + g]
            m = jnp.maximum(jnp.maximum(jnp.max(sb, axis=-1, keepdims=True),
                                        jnp.max(sc, axis=-1, keepdims=True)), sink)
            pb = jnp.exp(sb - m)
            pc = jnp.exp(sc - m)
            denom.append(jnp.sum(pb, axis=-1, keepdims=True) + jnp.sum(pc, axis=-1, keepdims=True)
                         + jnp.exp(sink - m))
            p_band.append(pb.astype(MXU_DTYPE))
            p_ctx.append(pc.astype(MXU_DTYPE))
        o = (_dot(jnp.concatenate(p_band, axis=0), v_ref[pl.ds(start, band), kv])
             + _dot(jnp.concatenate(p_ctx, axis=0), v_ref[pl.ds(t_lat, c_len), kv]))
        o = o / jnp.concatenate(denom, axis=0)
        for g in range(group):
            cols = slice((hk * group + g) * HEAD_DIM, (hk * group + g + 1) * HEAD_DIM)
            o_ref[:, cols] = o[g * tq:(g + 1) * tq, :].astype(o_ref.dtype)


def _window_attention(qkv, sink, t_lat):
    b, s, _ = qkv.shape
    tq = ROW_TILE
    assert tq & (tq - 1) == 0 and t_lat >= tq + 2 * WINDOW
    q0 = (A_QW + 2 * A_KVW) // B_QW
    k0 = (A_QW + 2 * A_KVW + B_QW) // B_KVW
    v0 = k0 + 1
    return pl.pallas_call(
        functools.partial(_window_attn_kernel, t_lat=t_lat, tq=tq),
        grid=(b, s // tq),
        in_specs=[
            pl.BlockSpec(memory_space=pltpu.SMEM),
            pl.BlockSpec((None, tq, B_QW), lambda bi, i: (bi, i, q0)),
            pl.BlockSpec((None, s, B_KVW), lambda bi, i: (bi, 0, k0)),
            pl.BlockSpec((None, s, B_KVW), lambda bi, i: (bi, 0, v0)),
        ],
        out_specs=pl.BlockSpec((None, tq, B_QW), lambda bi, i: (bi, i, 0)),
        out_shape=jax.ShapeDtypeStruct((b, s, B_QW), MXU_DTYPE),
        compiler_params=_cparams(("parallel", "arbitrary"), 40),
        name="window_attention",
    )(sink, qkv, qkv, qkv)


def _log_sigmoid(x):
    return jnp.minimum(x, 0.0) - jnp.log(1.0 + jnp.exp(-jnp.abs(x)))


def _gla_proj_kernel(h_ref, modl_ref, modc_ref, w_ref, wa1_ref, wa2_ref, ba_ref, p_ref, ld_ref, *, t_lat):
    a = _modulated_rows(h_ref, modl_ref, modc_ref, t_lat)
    for cb in range(GLA_IN // MXU_N):
        z = _dot(a, w_ref[:, cb * MXU_N:(cb + 1) * MXU_N])
        if cb < GLA_KEY_DIM // MXU_N:
            z = z * (GLA_DK ** -0.5)
        p_ref[:, cb * MXU_N:(cb + 1) * MXU_N] = z.astype(p_ref.dtype)
    low = _dot(a, wa1_ref[...]).astype(MXU_DTYPE)
    for cb in range(2 * GLA_KEY_DIM // MXU_N):
        cols = slice(cb * MXU_N, (cb + 1) * MXU_N)
        logits = _dot(low, wa2_ref[:, cols]) + ba_ref[:, cols]
        ld_ref[:, cols] = _log_sigmoid(logits) * (LOG2_E / GLA_GATE_NORM)


def _gla_project(h, mods, layer, w_in, mixer, w_a1, w_a2, b_a, t_lat):
    b, s, d = h.shape
    tm = _proj_tile(s)
    return pl.pallas_call(
        functools.partial(_gla_proj_kernel, t_lat=t_lat),
        grid=(b, s // tm),
        in_specs=[
            pl.BlockSpec((None, tm, d), lambda bi, i: (bi, i, 0)),
            pl.BlockSpec((None, None, 6, d), lambda bi, i: (layer, bi, 0, 0)),
            pl.BlockSpec((None, None, 6, d), lambda bi, i: (layer, MOD_CTX_ROW, 0, 0)),
            pl.BlockSpec((None, d, GLA_IN), lambda bi, i: (mixer, 0, 0)),
            pl.BlockSpec((d, LANES), lambda bi, i: (0, 0)),
            pl.BlockSpec((LANES, 2 * GLA_KEY_DIM), lambda bi, i: (0, 0)),
            pl.BlockSpec((1, 2 * GLA_KEY_DIM), lambda bi, i: (0, 0)),
        ],
        out_specs=[
            pl.BlockSpec((None, tm, GLA_IN), lambda bi, i: (bi, i, 0)),
            pl.BlockSpec((None, tm, 2 * GLA_KEY_DIM), lambda bi, i: (bi, i, 0)),
        ],
        out_shape=[
            jax.ShapeDtypeStruct((b, s, GLA_IN), MXU_DTYPE),
            jax.ShapeDtypeStruct((b, s, 2 * GLA_KEY_DIM), F32),
        ],
        compiler_params=_cparams(("parallel", "parallel"), 48),
        name="gla_project",
    )(h, mods, mods, w_in, w_a1, w_a2, b_a)


def _cumsum_rows(x):
    rows, cols = x.shape
    sub = lax.broadcasted_iota(jnp.int32, x.shape, 0) & (SUBLANES - 1)
    shift = 1
    while shift < SUBLANES:
        x = x + jnp.where(sub >= shift, pltpu.roll(x, shift, 0), 0.0)
        shift *= 2
    x3 = x.reshape(rows // SUBLANES, SUBLANES, cols)
    carry = jnp.zeros((1, cols), F32)
    tiles = []
    for i in range(rows // SUBLANES):
        tile = x3[i] + carry
        tiles.append(tile)
        carry = tile[SUBLANES - 1:SUBLANES, :]
    return jnp.concatenate(tiles, axis=0)


def _block_anchor(p, n):
    rows, cols = p.shape
    half = n // 2
    if n >= SUBLANES:
        p3 = p.reshape(rows // n, n, cols)
        return jnp.broadcast_to(p3[:, half - 1:half, :], p3.shape).reshape(rows, cols)
    if n == 2:
        odd = (lax.broadcasted_iota(jnp.int32, p.shape, 0) & 1) == 1
        return jnp.where(odd, pltpu.roll(p, 1, 0), p)
    p3 = p.reshape(rows // SUBLANES, SUBLANES, cols)
    sub = lax.broadcasted_iota(jnp.int32, p3.shape, 1)
    a = None
    for b0 in range(0, SUBLANES, n):
        row = jnp.broadcast_to(p3[:, b0 + half - 1:b0 + half, :], p3.shape)
        a = row if a is None else jnp.where(sub >= b0, row, a)
    return a.reshape(rows, cols)


def _gla_chunk(q, k, v, ld, sign_ref, st_ref, reverse):
    chunk = q.shape[0]
    p_inc = _cumsum_rows(ld)
    p_tot = p_inc[chunk - 1:chunk, :]
    px = p_inc - ld if reverse else p_inc
    qf = q.astype(F32)
    kf = k.astype(F32)

    def decayed(x, exponent):
        return (x * jnp.exp2(exponent)).astype(MXU_DTYPE)

    row = lax.broadcasted_iota(jnp.int32, (chunk, chunk), 0)
    col = lax.broadcasted_iota(jnp.int32, (chunk, chunk), 1)
    differ = row ^ col
    att = jnp.broadcast_to(jnp.sum(qf * kf, axis=-1, keepdims=True), (chunk, chunk))
    n = 2
    while n <= chunk:
        level = n.bit_length() - 2
        exponent = (px - _block_anchor(p_inc, n)) * sign_ref[level]
        s_lvl = _dot_nt(decayed(qf, exponent), decayed(kf, exponent))
        att = jnp.where((differ >> level) == 1, s_lvl, att)
        n *= 2
    att = jnp.where(row <= col if reverse else row >= col, att, 0.0)

    if reverse:
        qd = decayed(qf, p_tot - px)
        kd = decayed(kf, px)
    else:
        qd = decayed(qf, px)
        kd = decayed(kf, p_tot - px)
    state = st_ref[...]
    o = _dot(att.astype(MXU_DTYPE), v) + _dot_nt(qd, state.astype(MXU_DTYPE))
    st_ref[...] = state * jnp.exp2(p_tot) + _dot_tn(v, kd)
    return o


def _gla_scan_kernel(sign_ref, qf_ref, kf_ref, vf_ref, ldf_ref, qr_ref, kr_ref, vr_ref, ldr_ref,
                     of_ref, or_ref, st_ref):
    @pl.when(pl.program_id(1) == 0)
    def _():
        st_ref[...] = jnp.zeros(st_ref.shape, F32)

    operands = ((qf_ref, kf_ref, vf_ref, ldf_ref, of_ref), (qr_ref, kr_ref, vr_ref, ldr_ref, or_ref))
    for direction, (q_ref, k_ref, v_ref, ld_ref, o_ref) in enumerate(operands):
        for hd in range(GLA_HEADS):
            kc = slice(hd * GLA_DK, (hd + 1) * GLA_DK)
            vc = slice(hd * GLA_DV, (hd + 1) * GLA_DV)
            o = _gla_chunk(q_ref[:, kc], k_ref[:, kc], v_ref[:, vc], ld_ref[:, kc], sign_ref,
                           st_ref.at[direction, hd], reverse=direction == 1)
            o_ref[:, vc] = o.astype(o_ref.dtype)


def _gla_scan(proj, ld, t_lat):
    b, s, _ = proj.shape
    ch = GLA_CHUNK
    n_chunks = s // ch
    lat_chunks = t_lat // ch

    def fwd(c):
        return (c + lat_chunks) % n_chunks

    def rev(c):
        return n_chunks - 1 - c

    def specs(chunk_of, direction):
        return [
            pl.BlockSpec((None, ch, GLA_KEY_DIM), lambda bi, c: (bi, chunk_of(c), 0)),
            pl.BlockSpec((None, ch, GLA_KEY_DIM), lambda bi, c: (bi, chunk_of(c), 1)),
            pl.BlockSpec((None, ch, GLA_VAL_DIM), lambda bi, c: (bi, chunk_of(c), 1)),
            pl.BlockSpec((None, ch, GLA_KEY_DIM), lambda bi, c: (bi, chunk_of(c), direction)),
        ]

    levels = ch.bit_length() - 1
    half_bit = 1 << jnp.arange(levels, dtype=jnp.int32)[:, None, None]
    in_upper = (jnp.arange(ch, dtype=jnp.int32)[None, :, None] & half_bit) != 0
    sign = jnp.broadcast_to(jnp.where(in_upper, 1.0, -1.0).astype(F32), (levels, ch, GLA_DK))
    o_shape = jax.ShapeDtypeStruct((b, s, GLA_VAL_DIM), F32)
    return pl.pallas_call(
        _gla_scan_kernel,
        grid=(b, n_chunks),
        in_specs=[pl.BlockSpec((levels, ch, GLA_DK), lambda bi, c: (0, 0, 0))] + specs(fwd, 0) + specs(rev, 1),
        out_specs=[
            pl.BlockSpec((None, ch, GLA_VAL_DIM), lambda bi, c: (bi, fwd(c), 0)),
            pl.BlockSpec((None, ch, GLA_VAL_DIM), lambda bi, c: (bi, rev(c), 0)),
        ],
        out_shape=[o_shape, o_shape],
        scratch_shapes=[pltpu.VMEM((2, GLA_HEADS, GLA_DV, GLA_DK), F32)],
        compiler_params=_cparams(("parallel", "arbitrary"), 32),
        name="gla_scan",
    )(sign, proj, proj, proj, ld, proj, proj, proj, ld)


def _with_halo(ref, next_ref):
    return jnp.concatenate([ref[...], next_ref[...]], axis=0)


def _mixer_out_norm(x, h_refs, mod_ref, wout_ref, lng_ref, lnb_ref, h1_scr, alpha):
    tm = h_refs[0].shape[0]

    @pl.when(pl.program_id(1) == 0)
    def _():
        h1_scr[0:HALO, :] = jnp.zeros((HALO, h1_scr.shape[1]), F32)

    @pl.when(pl.program_id(1) > 0)
    def _():
        h1_scr[0:HALO, :] = h1_scr[tm:tm + HALO, :]

    y = _dot(x, wout_ref[...])
    h1_scr[HALO:, :] = _residual_layer_norm(_with_halo(*h_refs), y, mod_ref[2:3, :], lng_ref[...], lnb_ref[...],
                                            alpha)


def _attn_ffn_kernel(oa_ref, oan_ref, ob_ref, obn_ref, h_ref, hn_ref, mod_ref,
                     wout_ref, ln1g_ref, ln1b_ref, wup_ref, cw_ref, cb_ref, wdn_ref, ln2g_ref, ln2b_ref,
                     out_ref, h1_scr, a_scr, u_scr, act_scr, *, t_lat, s_tot, alpha):
    x = jnp.concatenate([_with_halo(oa_ref, oan_ref), _with_halo(ob_ref, obn_ref)], axis=1)
    _mixer_out_norm(x, (h_ref, hn_ref), mod_ref, wout_ref, ln1g_ref, ln1b_ref, h1_scr, alpha)
    _conv_ffn_norm(h1_scr, mod_ref, wup_ref, cw_ref, cb_ref, wdn_ref, ln2g_ref, ln2b_ref, out_ref,
                   a_scr, u_scr, act_scr, t_lat=t_lat, s_tot=s_tot, alpha=alpha)


def _gla_ffn_kernel(of_ref, ofn_ref, or_ref, orn_ref, g_ref, gn_ref,
                    h_ref, hn_ref, mod_ref, hnorm_ref, wout_ref, ln1g_ref, ln1b_ref,
                    wup_ref, cw_ref, cb_ref, wdn_ref, ln2g_ref, ln2b_ref,
                    out_ref, h1_scr, a_scr, u_scr, act_scr, *, t_lat, s_tot, alpha):
    o = _with_halo(of_ref, ofn_ref) + _with_halo(or_ref, orn_ref)
    gate = _silu(_with_halo(g_ref, gn_ref).astype(F32))
    heads = []
    for hd in range(GLA_HEADS):
        cols = slice(hd * GLA_DV, (hd + 1) * GLA_DV)
        oh = o[:, cols]
        oh = oh * lax.rsqrt(jnp.mean(oh * oh, axis=-1, keepdims=True) + EPS) * hnorm_ref[...]
        heads.append((oh * gate[:, cols]).astype(MXU_DTYPE))
    x = jnp.concatenate(heads, axis=1)
    _mixer_out_norm(x, (h_ref, hn_ref), mod_ref, wout_ref, ln1g_ref, ln1b_ref, h1_scr, alpha)
    _conv_ffn_norm(h1_scr, mod_ref, wup_ref, cw_ref, cb_ref, wdn_ref, ln2g_ref, ln2b_ref, out_ref,
                   a_scr, u_scr, act_scr, t_lat=t_lat, s_tot=s_tot, alpha=alpha)


def _conv_ffn_norm(h1_scr, mod_ref, wup_ref, cw_ref, cb_ref, wdn_ref, lng_ref, lnb_ref,
                   out_ref, a_scr, u_scr, act_scr, *, t_lat, s_tot, alpha):
    tm = out_ref.shape[0]
    d_ff = wdn_ref.shape[0]
    row0 = pl.program_id(1) * tm
    at_start = (row0 == 0) | (row0 == t_lat)
    at_end = (row0 + tm == t_lat) | (row0 + tm == s_tot)
    shift = mod_ref[3:4, :]
    scale = 1.0 + mod_ref[4:5, :]
    a_all = h1_scr[...] * scale + shift
    a_scr[0:HALO, :] = jnp.where(at_start, 0.0, a_all[0:HALO, :]).astype(a_scr.dtype)
    a_scr[HALO:HALO + tm, :] = a_all[HALO:HALO + tm, :].astype(a_scr.dtype)
    a_scr[HALO + tm:2 * HALO + tm, :] = jnp.where(at_end, 0.0, a_all[HALO + tm:2 * HALO + tm, :]).astype(a_scr.dtype)
    a = a_scr[...]

    def conv(u_ref, c0):
        cols = slice(c0, c0 + FF_CHUNK)
        acc = cb_ref[:, cols] + cw_ref[0:1, cols] * u_ref[pl.ds(HALO - 1, tm), :]
        for j in range(1, CONV_W):
            acc = acc + cw_ref[j:j + 1, cols] * u_ref[pl.ds(HALO - 1 + j, tm), :]
        return acc

    def up_project(c):
        g0 = c * FF_CHUNK
        v0 = d_ff + c * FF_CHUNK
        u_scr[c % U_SLOTS, 0] = _dot(a, wup_ref[:, g0:g0 + FF_CHUNK])
        u_scr[c % U_SLOTS, 1] = _dot(a, wup_ref[:, v0:v0 + FF_CHUNK])

    def down_project(c):
        return _dot(act_scr[c % ACT_SLOTS], wdn_ref[c * FF_CHUNK:(c + 1) * FF_CHUNK, :])

    n_chunks = d_ff // FF_CHUNK
    ahead = U_SLOTS - 1
    y = jnp.zeros(out_ref.shape, F32)
    for c in range(min(ahead, n_chunks)):
        up_project(c)
    for c in range(n_chunks):
        if c + ahead < n_chunks:
            up_project(c + ahead)
        if c > 0:
            y = y + down_project(c - 1)
        g0 = c * FF_CHUNK
        act = _silu(conv(u_scr.at[c % U_SLOTS, 0], g0)) * conv(u_scr.at[c % U_SLOTS, 1], d_ff + g0)
        act_scr[c % ACT_SLOTS] = act.astype(act_scr.dtype)
    y = y + down_project(n_chunks - 1)
    h1 = h1_scr[HALO:HALO + tm, :]
    out_ref[...] = _residual_layer_norm(h1, y, mod_ref[5:6, :], lng_ref[...], lnb_ref[...], alpha)


def _mixer_ffn(kernel_fn, name, mixer_ins, h, mods, layer, extra_params, w_out, mixer, ln_g, ln_b,
               w_up, conv_w, conv_b, w_down, t_lat, n_rows, alpha):
    b, s, d = h.shape
    tm = ROW_TILE
    d_ff = w_down.shape[1]
    row = _mod_row(t_lat, tm)
    hb = tm // HALO
    n_halo = s // HALO

    def halo_specs(cols, cb):
        return [
            pl.BlockSpec((None, tm, cols), lambda bi, i: (bi, i, cb)),
            pl.BlockSpec((None, HALO, cols), lambda bi, i: (bi, jnp.minimum((i + 1) * hb, n_halo - 1), cb)),
        ]

    row_ins = list(mixer_ins) + [(h, d, 0)]
    in_specs, operands = [], []
    for arr, cols, cb in row_ins:
        in_specs += halo_specs(cols, cb)
        operands += [arr, arr]
    in_specs.append(pl.BlockSpec((None, None, 6, d), lambda bi, i: (layer, row(bi, i), 0, 0)))
    operands.append(mods)
    for p in extra_params:
        in_specs.append(pl.BlockSpec(p.shape, lambda bi, i: (0, 0)))
        operands.append(p)
    in_specs += [
        pl.BlockSpec((None,) + w_out.shape[1:], lambda bi, i: (mixer, 0, 0)),
        pl.BlockSpec((None, 1, d), lambda bi, i: (layer, 0, 0)),
        pl.BlockSpec((None, 1, d), lambda bi, i: (layer, 0, 0)),
        pl.BlockSpec((None, d, 2 * d_ff), lambda bi, i: (layer, 0, 0)),
        pl.BlockSpec((None, CONV_W, 2 * d_ff), lambda bi, i: (layer, 0, 0)),
        pl.BlockSpec((None, 1, 2 * d_ff), lambda bi, i: (layer, 0, 0)),
        pl.BlockSpec((None, d_ff, d), lambda bi, i: (layer, 0, 0)),
        pl.BlockSpec((None, 1, d), lambda bi, i: (layer, 0, 0)),
        pl.BlockSpec((None, 1, d), lambda bi, i: (layer, 0, 0)),
    ]
    operands += [w_out, ln_g[:, 0:1, :], ln_b[:, 0:1, :], w_up, conv_w, conv_b.reshape(-1, 1, 2 * d_ff), w_down,
                 ln_g[:, 1:2, :], ln_b[:, 1:2, :]]
    return pl.pallas_call(
        functools.partial(kernel_fn, t_lat=t_lat, s_tot=s, alpha=alpha),
        grid=(b, n_rows // tm),
        in_specs=in_specs,
        out_specs=pl.BlockSpec((None, tm, d), lambda bi, i: (bi, i, 0)),
        out_shape=jax.ShapeDtypeStruct((b, n_rows, d), F32),
        scratch_shapes=[
            pltpu.VMEM((tm + 2 * HALO, d), F32),
            pltpu.VMEM((tm + 2 * HALO, d), MXU_DTYPE),
            pltpu.VMEM((U_SLOTS, 2, tm + 2 * HALO, FF_CHUNK), F32),
            pltpu.VMEM((ACT_SLOTS, tm, FF_CHUNK), MXU_DTYPE),
        ],
        compiler_params=_cparams(("parallel", "arbitrary"), 60),
        name=name,
    )(*operands)


def _rope_tables(t_lat, c_len):
    rows = t_lat // GRID_W
    n_freq = ROPE_AXIS_DIM // 2
    inv_freq = jnp.power(ROPE_THETA, -jnp.arange(n_freq, dtype=F32) * 2.0 / ROPE_AXIS_DIM)
    ang_row = jnp.arange(rows, dtype=F32)[:, None] * inv_freq
    ang_col = jnp.arange(GRID_W, dtype=F32)[:, None] * inv_freq

    def table(fn, ctx_value):
        by_row = jnp.broadcast_to(fn(ang_row)[:, None, :], (rows, GRID_W, n_freq))
        by_col = jnp.broadcast_to(fn(ang_col)[None, :, :], (rows, GRID_W, n_freq))
        lat = jnp.concatenate([by_row, by_col], axis=-1).reshape(t_lat, HEAD_DIM // 2)
        return jnp.concatenate([lat, jnp.full((c_len, HEAD_DIM // 2), ctx_value, F32)], axis=0)

    cos, sin = table(jnp.cos, 1.0), table(jnp.sin, 0.0)
    return jnp.concatenate([cos, cos], axis=-1), jnp.concatenate([-sin, sin], axis=-1)


def kernel(x, c, ctx, c_ctx, ada_w, ada_b, ln_g, ln_b, ffn_w_up, ffn_conv_w, ffn_conv_b, ffn_w_down, attn_w_in, attn_q_norm, attn_k_norm, attn_sink, attn_w_out, gla_w_in, gla_w_a1, gla_w_a2, gla_b_a, gla_head_norm, gla_w_out):
    b, t_lat, d = x.shape
    c_len = ctx.shape[1]
    s = t_lat + c_len
    depth = ada_w.shape[0]
    assert b < MOD_CTX_ROW and t_lat % ROW_TILE == 0 and c_len % ROW_TILE == 0
    assert t_lat % GLA_CHUNK == 0 and c_len % GLA_CHUNK == 0 and s % MXU_N == 0
    alpha = (2 * depth) ** 0.25

    cond = jnp.zeros((MOD_ROWS, d), F32).at[:b].set(c).at[MOD_CTX_ROW].set(c_ctx)
    mods = _modulation(cond, ada_w, ada_b).reshape(depth, MOD_ROWS, 6, d)
    cos, sin = _rope_tables(t_lat, c_len)
    h = jnp.concatenate([x, ctx], axis=1)
    attn_w_in_c, attn_w_out_c = attn_w_in.astype(MXU_DTYPE), attn_w_out.astype(MXU_DTYPE)
    gla_w_in_c, gla_w_out_c = gla_w_in.astype(MXU_DTYPE), gla_w_out.astype(MXU_DTYPE)
    ffn_w_up_c, ffn_w_down_c = ffn_w_up.astype(MXU_DTYPE), ffn_w_down.astype(MXU_DTYPE)

    for i in range(depth):
        last = i == depth - 1
        n_rows = t_lat if last else s
        j = i // 2
        if i % 2 == 0:
            qkv = _attn_project(h, mods, i, attn_w_in_c, j, attn_q_norm[j], attn_k_norm[j],
                                cos, sin, t_lat)
            o_a = _global_attention(qkv, t_lat)
            o_b = _window_attention(qkv, attn_sink[j], t_lat)
            h = _mixer_ffn(_attn_ffn_kernel, "attn_out_ffn", [(o_a, A_QW, 0), (o_b, B_QW, 0)], h, mods, i, [],
                           attn_w_out_c, j, ln_g, ln_b, ffn_w_up_c, ffn_conv_w, ffn_conv_b, ffn_w_down_c,
                           t_lat, n_rows, alpha)
        else:
            w_a1 = jnp.concatenate([gla_w_a1[j, 0], gla_w_a1[j, 1]], axis=1)
            w_a1 = jnp.pad(w_a1, ((0, 0), (0, LANES - 2 * GLA_GATE_RANK))).astype(MXU_DTYPE)
            w_a2 = jnp.zeros((LANES, 2 * GLA_KEY_DIM), F32)
            w_a2 = w_a2.at[:GLA_GATE_RANK, :GLA_KEY_DIM].set(gla_w_a2[j, 0])
            w_a2 = w_a2.at[GLA_GATE_RANK:2 * GLA_GATE_RANK, GLA_KEY_DIM:].set(gla_w_a2[j, 1]).astype(MXU_DTYPE)
            proj, ld = _gla_project(h, mods, i, gla_w_in_c, j, w_a1, w_a2,
                                    gla_b_a[j].reshape(1, 2 * GLA_KEY_DIM), t_lat)
            o_f, o_r = _gla_scan(proj, ld, t_lat)
            g_blk = (2 * GLA_KEY_DIM + GLA_VAL_DIM) // GLA_VAL_DIM
            h = _mixer_ffn(_gla_ffn_kernel, "gla_out_ffn",
                           [(o_f, GLA_VAL_DIM, 0), (o_r, GLA_VAL_DIM, 0), (proj, GLA_VAL_DIM, g_blk)], h, mods, i,
                           [gla_head_norm[j].reshape(1, GLA_DV)], gla_w_out_c, j, ln_g, ln_b,
                           ffn_w_up_c, ffn_conv_w, ffn_conv_b, ffn_w_down_c, t_lat, n_rows, alpha)
    return h
```

```python
import functools

import jax
import jax.numpy as jnp
from jax import lax
from jax.experimental import pallas as pl
from jax.experimental.pallas import tpu as pltpu

GRID_W = 64
HEAD_DIM = 128
A_Q_HEADS = 4
A_KV_HEADS = 2
B_Q_HEADS = 4
B_KV_HEADS = 2
A_QW = A_Q_HEADS * HEAD_DIM
A_KVW = A_KV_HEADS * HEAD_DIM
B_QW = B_Q_HEADS * HEAD_DIM
B_KVW = B_KV_HEADS * HEAD_DIM
ATTN_IN = A_QW + 2 * A_KVW + B_QW + 2 * B_KVW
ATTN_SCALE = HEAD_DIM ** -0.5
WINDOW = 128
ROPE_THETA = 10000.0
ROPE_AXIS_DIM = HEAD_DIM // 2
GLA_HEADS = 4
GLA_DK = 128
GLA_DV = 256
GLA_KEY_DIM = GLA_HEADS * GLA_DK
GLA_VAL_DIM = GLA_HEADS * GLA_DV
GLA_IN = 2 * GLA_KEY_DIM + 2 * GLA_VAL_DIM
GLA_GATE_RANK = 16
GLA_GATE_NORM = 16.0
CONV_W = 3
EPS = 1e-6

LANES = 128
SUBLANES = 8
MXU_N = 256
VMEM_BYTES = 64 * 2 ** 20

MXU_DTYPE = jnp.bfloat16
F32 = jnp.float32

ROW_TILE = 256
PROJ_TILE = 768
HALO = 16
FF_CHUNK = 256
U_SLOTS = 3
ACT_SLOTS = 2
KV_CHUNK = 2816
KV_EDGE = 512
KV_SLOTS = 2
SOFTMAX_ROWS = 16
LOG2_E = 1.4426950408889634
GLA_CHUNK = 128
MOD_COLS = 1536


def _cparams(sem, vmem_mib):
    return pltpu.CompilerParams(dimension_semantics=sem, vmem_limit_bytes=vmem_mib * 2 ** 20)


def _dot(a, b):
    return jnp.dot(a, b, preferred_element_type=F32)


def _dot_nt(a, b):
    return lax.dot_general(a, b, (((1,), (1,)), ((), ())), preferred_element_type=F32)


def _dot_tn(a, b):
    return lax.dot_general(a, b, (((0,), (0,)), ((), ())), preferred_element_type=F32)


def _silu(x):
    return x / (1.0 + jnp.exp(-x))


def _residual_layer_norm(h, y, gate, ln_g, ln_b, alpha):
    r = alpha * h + gate * y
    mu = jnp.mean(r, axis=-1, keepdims=True)
    d = r - mu
    var = jnp.mean(d * d, axis=-1, keepdims=True)
    return d * lax.rsqrt(var + EPS) * ln_g + ln_b


def _modulated_rows(h_ref, modl_ref, modc_ref, t_lat):
    tm = h_ref.shape[0]
    rows = pl.program_id(1) * tm + lax.broadcasted_iota(jnp.int32, (tm, 1), 0)
    is_ctx = rows >= t_lat
    scale = 1.0 + jnp.where(is_ctx, modc_ref[1:2, :], modl_ref[1:2, :])
    shift = jnp.where(is_ctx, modc_ref[0:1, :], modl_ref[0:1, :])
    return (h_ref[...] * scale + shift).astype(MXU_DTYPE)


def _proj_tile(s_tot):
    return max(n for n in range(ROW_TILE, PROJ_TILE + 1, ROW_TILE) if s_tot % n == 0)


def _mod_row(t_lat, tm):
    n_lat = t_lat // tm

    def row(b, i):
        return jnp.where(i < n_lat, b, MOD_CTX_ROW)
    return row


MOD_ROWS = 8
MOD_CTX_ROW = MOD_ROWS - 1


def _mod_kernel(cond_ref, w_ref, b_ref, o_ref):
    a = _silu(cond_ref[...]).astype(MXU_DTYPE)
    o_ref[...] = _dot(a, w_ref[...].astype(MXU_DTYPE)) + b_ref[...]


def _modulation(cond, ada_w, ada_b):
    depth, d, n = ada_w.shape
    return pl.pallas_call(
        _mod_kernel,
        grid=(depth, n // MOD_COLS),
        in_specs=[
            pl.BlockSpec((MOD_ROWS, d), lambda l, j: (0, 0)),
            pl.BlockSpec((None, d, MOD_COLS), lambda l, j: (l, 0, j)),
            pl.BlockSpec((None, 1, MOD_COLS), lambda l, j: (l, 0, j)),
        ],
        out_specs=pl.BlockSpec((None, MOD_ROWS, MOD_COLS), lambda l, j: (l, 0, j)),
        out_shape=jax.ShapeDtypeStruct((depth, MOD_ROWS, n), F32),
        compiler_params=_cparams(("parallel", "parallel"), 32),
        name="ada_modulation",
    )(cond, ada_w, ada_b.reshape(depth, 1, n))


def _attn_proj_kernel(h_ref, modl_ref, modc_ref, w_ref, qn_ref, kn_ref, cos_ref, sin_ref, o_ref, *, t_lat):
    a = _modulated_rows(h_ref, modl_ref, modc_ref, t_lat)
    cos = cos_ref[...]
    sin = sin_ref[...]

    def rms(z, g):
        return z * lax.rsqrt(jnp.mean(z * z, axis=-1, keepdims=True) + EPS) * g

    def rope(z):
        return z * cos + pltpu.roll(z, HEAD_DIM // 2, 1) * sin

    a_k0 = A_QW // HEAD_DIM
    a_v0 = a_k0 + A_KV_HEADS
    b_q0 = a_v0 + A_KV_HEADS
    b_k0 = b_q0 + B_Q_HEADS
    b_v0 = b_k0 + B_KV_HEADS
    heads_per_dot = MXU_N // HEAD_DIM
    for cb in range(ATTN_IN // MXU_N):
        z2 = _dot(a, w_ref[:, cb * MXU_N:(cb + 1) * MXU_N])
        for half in range(heads_per_dot):
            hb = cb * heads_per_dot + half
            z = z2[:, half * HEAD_DIM:(half + 1) * HEAD_DIM]
            if hb < a_k0:
                z = rope(rms(z, qn_ref[...])) * (ATTN_SCALE * LOG2_E)
            elif hb < a_v0:
                z = rope(rms(z, kn_ref[...]))
            elif hb < b_q0:
                pass
            elif hb < b_k0:
                z = rope(z) * ATTN_SCALE
            elif hb < b_v0:
                z = rope(z)
            o_ref[:, hb * HEAD_DIM:(hb + 1) * HEAD_DIM] = z.astype(o_ref.dtype)


def _attn_project(h, mods, layer, w_in, mixer, q_norm, k_norm, cos, sin, t_lat):
    b, s, d = h.shape
    tm = ROW_TILE
    return pl.pallas_call(
        functools.partial(_attn_proj_kernel, t_lat=t_lat),
        grid=(b, s // tm),
        in_specs=[
            pl.BlockSpec((None, tm, d), lambda bi, i: (bi, i, 0)),
            pl.BlockSpec((None, None, 6, d), lambda bi, i: (layer, bi, 0, 0)),
            pl.BlockSpec((None, None, 6, d), lambda bi, i: (layer, MOD_CTX_ROW, 0, 0)),
            pl.BlockSpec((None, d, ATTN_IN), lambda bi, i: (mixer, 0, 0)),
            pl.BlockSpec((1, HEAD_DIM), lambda bi, i: (0, 0)),
            pl.BlockSpec((1, HEAD_DIM), lambda bi, i: (0, 0)),
            pl.BlockSpec((tm, HEAD_DIM), lambda bi, i: (i, 0)),
            pl.BlockSpec((tm, HEAD_DIM), lambda bi, i: (i, 0)),
        ],
        out_specs=pl.BlockSpec((None, tm, ATTN_IN), lambda bi, i: (bi, i, 0)),
        out_shape=jax.ShapeDtypeStruct((b, s, ATTN_IN), MXU_DTYPE),
        compiler_params=_cparams(("parallel", "parallel"), 40),
        name="attn_project",
    )(h, mods, mods, w_in, q_norm.reshape(1, HEAD_DIM), k_norm.reshape(1, HEAD_DIM), cos, sin)


def _stack_heads(q):
    g = q.shape[1] // HEAD_DIM
    return jnp.concatenate([q[:, i * HEAD_DIM:(i + 1) * HEAD_DIM] for i in range(g)], axis=0)


def _store_heads(o_ref, o, tq):
    for i in range(o.shape[0] // tq):
        o_ref[:, i * HEAD_DIM:(i + 1) * HEAD_DIM] = o[i * tq:(i + 1) * tq, :].astype(o_ref.dtype)


def _softmax_chunk(s_ref, m_scr, alpha_ref, p_ref):
    rows, n_keys = s_ref.shape
    for g in range(rows // SOFTMAX_ROWS):
        rg = slice(g * SOFTMAX_ROWS, (g + 1) * SOFTMAX_ROWS)
        blocks = [s_ref[rg, j * LANES:(j + 1) * LANES] for j in range(n_keys // LANES)]
        mx = blocks[0]
        for blk in blocks[1:]:
            mx = jnp.maximum(mx, blk)
        m_prev = m_scr[rg, :]
        m_new = jnp.maximum(m_prev, jnp.max(mx, axis=-1, keepdims=True))
        alpha_ref[rg, :] = jnp.exp2(m_prev - m_new)
        for j, blk in enumerate(blocks):
            p_ref[rg, j * LANES:(j + 1) * LANES] = jnp.exp2(blk - m_new).astype(p_ref.dtype)
        m_scr[rg, :] = m_new


def _global_attn_kernel(q_ref, k_ref, v_ref, o_ref, vext_scr, m_scr, acc_scr, s_scr, p_scr, alpha_scr,
                        *, t_lat, tq, kv_chunks):
    s_tot = k_ref.shape[0]
    c_len = s_tot - t_lat
    is_ctx = pl.program_id(2) * tq >= t_lat

    @pl.when(pl.program_id(2) == 0)
    def _():
        ones_col = jnp.where(lax.broadcasted_iota(jnp.int32, (MXU_N, HEAD_DIM), 1) == 0, 1.0, 0.0)

        def fill(blk, carry):
            rows = pl.ds(pl.multiple_of(blk * MXU_N, MXU_N), MXU_N)
            vext_scr[rows, 0:HEAD_DIM] = v_ref[rows, :]
            vext_scr[rows, HEAD_DIM:2 * HEAD_DIM] = ones_col.astype(vext_scr.dtype)
            return carry

        lax.fori_loop(0, s_tot // MXU_N, fill, 0)

    q2 = _stack_heads(q_ref[...])
    m_scr[...] = jnp.full(m_scr.shape, -jnp.inf, F32)
    acc_scr[...] = jnp.zeros(acc_scr.shape, F32)

    def keys(c):
        return slice(kv_chunks[c][0], kv_chunks[c][0] + kv_chunks[c][1])

    def width(c):
        return slice(0, kv_chunks[c][1])

    def accumulate(p, v_ext, alpha):
        acc_scr[...] = jnp.concatenate([alpha, alpha], axis=1) * acc_scr[...] + _dot(p, v_ext)

    @pl.when(jnp.logical_not(is_ctx))
    def _():
        n_chunks = len(kv_chunks)
        s_scr[0, :, width(0)] = _dot_nt(q2, k_ref[keys(0), :])
        for c in range(n_chunks):
            slot = c % KV_SLOTS
            if c + 1 < n_chunks:
                s_scr[(c + 1) % KV_SLOTS, :, width(c + 1)] = _dot_nt(q2, k_ref[keys(c + 1), :])
            if c > 0:
                prev = (c - 1) % KV_SLOTS
                accumulate(p_scr[prev, :, width(c - 1)], vext_scr[keys(c - 1), :], alpha_scr[prev])
            _softmax_chunk(s_scr.at[slot, :, width(c)], m_scr, alpha_scr.at[slot], p_scr.at[slot, :, width(c)])
        last = (n_chunks - 1) % KV_SLOTS
        accumulate(p_scr[last, :, width(n_chunks - 1)], vext_scr[keys(n_chunks - 1), :], alpha_scr[last])

    @pl.when(is_ctx)
    def _():
        s_scr[0, :, 0:c_len] = _dot_nt(q2, k_ref[t_lat:s_tot, :])
        _softmax_chunk(s_scr.at[0, :, 0:c_len], m_scr, alpha_scr.at[0], p_scr.at[0, :, 0:c_len])
        accumulate(p_scr[0, :, 0:c_len], vext_scr[t_lat:s_tot, :], alpha_scr[0])

    acc = acc_scr[...]
    _store_heads(o_ref, acc[:, 0:HEAD_DIM] / acc[:, HEAD_DIM:HEAD_DIM + 1], tq)


def _kv_chunks(s_tot):
    units = s_tot // MXU_N
    edge = KV_EDGE // MXU_N
    if units <= 2 * edge:
        sizes = [1] * units
    else:
        n_mid = -(-(units - 2 * edge) // (KV_CHUNK // MXU_N))
        base, extra = divmod(units - 2 * edge, n_mid)
        sizes = [edge] + [base + (i < extra) for i in range(n_mid)] + [edge]
    chunks, start = [], 0
    for n in sizes:
        chunks.append((start * MXU_N, n * MXU_N))
        start += n
    return tuple(chunks)


def _global_attention(qkv, t_lat):
    b, s, _ = qkv.shape
    tq = ROW_TILE
    group = A_Q_HEADS // A_KV_HEADS
    k0 = A_QW // HEAD_DIM
    v0 = k0 + A_KV_HEADS
    kv_chunks = _kv_chunks(s)
    kv_chunk = max(size for _, size in kv_chunks)
    assert s - t_lat <= kv_chunk
    return pl.pallas_call(
        functools.partial(_global_attn_kernel, t_lat=t_lat, tq=tq, kv_chunks=kv_chunks),
        grid=(b, A_KV_HEADS, s // tq),
        in_specs=[
            pl.BlockSpec((None, tq, group * HEAD_DIM), lambda bi, hk, i: (bi, i, hk)),
            pl.BlockSpec((None, s, HEAD_DIM), lambda bi, hk, i: (bi, 0, k0 + hk)),
            pl.BlockSpec((None, s, HEAD_DIM), lambda bi, hk, i: (bi, 0, v0 + hk)),
        ],
        out_specs=pl.BlockSpec((None, tq, group * HEAD_DIM), lambda bi, hk, i: (bi, i, hk)),
        out_shape=jax.ShapeDtypeStruct((b, s, A_QW), MXU_DTYPE),
        scratch_shapes=[
            pltpu.VMEM((s, 2 * HEAD_DIM), MXU_DTYPE),
            pltpu.VMEM((group * tq, LANES), F32),
            pltpu.VMEM((group * tq, 2 * HEAD_DIM), F32),
            pltpu.VMEM((KV_SLOTS, group * tq, kv_chunk), F32),
            pltpu.VMEM((KV_SLOTS, group * tq, kv_chunk), MXU_DTYPE),
            pltpu.VMEM((KV_SLOTS, group * tq, LANES), F32),
        ],
        compiler_params=_cparams(("parallel", "parallel", "arbitrary"), 40),
        name="global_attention",
    )(qkv, qkv, qkv)


def _window_attn_kernel(sink_ref, q_ref, k_ref, v_ref, o_ref, *, t_lat, tq):
    s_tot = k_ref.shape[0]
    c_len = s_tot - t_lat
    band = tq + 2 * WINDOW
    q0 = pl.program_id(1) * tq
    is_ctx = q0 >= t_lat
    start = pl.multiple_of(jnp.clip(q0 - WINDOW, 0, t_lat - band), WINDOW)
    group = B_Q_HEADS // B_KV_HEADS

    delta = (lax.broadcasted_iota(jnp.int32, (tq, band), 1) + (start - q0)
             - lax.broadcasted_iota(jnp.int32, (tq, band), 0))
    reach = jnp.where(is_ctx, -1, WINDOW)
    valid = jnp.abs(delta) <= reach
    for hk in range(B_KV_HEADS):
        kv = slice(hk * HEAD_DIM, (hk + 1) * HEAD_DIM)
        q2 = _stack_heads(q_ref[:, hk * group * HEAD_DIM:(hk + 1) * group * HEAD_DIM])
        s_band = _dot_nt(q2, k_ref[pl.ds(start, band), kv])
        s_ctx = _dot_nt(q2, k_ref[pl.ds(t_lat, c_len), kv])
        p_band, p_ctx, denom = [], [], []
        for g in range(group):
            head = slice(g * tq, (g + 1) * tq)
            sb = jnp.where(valid, s_band[head, :], -jnp.inf)
            sc = s_ctx[head, :]
            sink = sink_ref[hk * group + g]
            m = jnp.maximum(jnp.maximum(jnp.max(sb, axis=-1, keepdims=True),
                                        jnp.max(sc, axis=-1, keepdims=True)), sink)
            pb = jnp.exp(sb - m)
            pc = jnp.exp(sc - m)
            denom.append(jnp.sum(pb, axis=-1, keepdims=True) + jnp.sum(pc, axis=-1, keepdims=True)
                         + jnp.exp(sink - m))
            p_band.append(pb.astype(MXU_DTYPE))
            p_ctx.append(pc.astype(MXU_DTYPE))
        o = (_dot(jnp.concatenate(p_band, axis=0), v_ref[pl.ds(start, band), kv])
             + _dot(jnp.concatenate(p_ctx, axis=0), v_ref[pl.ds(t_lat, c_len), kv]))
        o = o / jnp.concatenate(denom, axis=0)
        for g in range(group):
            cols = slice((hk * group + g) * HEAD_DIM, (hk * group + g + 1) * HEAD_DIM)
            o_ref[:, cols] = o[g * tq:(g + 1) * tq, :].astype(o_ref.dtype)


def _window_attention(qkv, sink, t_lat):
    b, s, _ = qkv.shape
    tq = ROW_TILE
    assert tq & (tq - 1) == 0 and t_lat >= tq + 2 * WINDOW
    q0 = (A_QW + 2 * A_KVW) // B_QW
    k0 = (A_QW + 2 * A_KVW + B_QW) // B_KVW
    v0 = k0 + 1
    return pl.pallas_call(
        functools.partial(_window_attn_kernel, t_lat=t_lat, tq=tq),
        grid=(b, s // tq),
        in_specs=[
            pl.BlockSpec(memory_space=pltpu.SMEM),
            pl.BlockSpec((None, tq, B_QW), lambda bi, i: (bi, i, q0)),
            pl.BlockSpec((None, s, B_KVW), lambda bi, i: (bi, 0, k0)),
            pl.BlockSpec((None, s, B_KVW), lambda bi, i: (bi, 0, v0)),
        ],
        out_specs=pl.BlockSpec((None, tq, B_QW), lambda bi, i: (bi, i, 0)),
        out_shape=jax.ShapeDtypeStruct((b, s, B_QW), MXU_DTYPE),
        compiler_params=_cparams(("parallel", "arbitrary"), 40),
        name="window_attention",
    )(sink, qkv, qkv, qkv)


def _log_sigmoid(x):
    return jnp.minimum(x, 0.0) - jnp.log(1.0 + jnp.exp(-jnp.abs(x)))


def _gla_proj_kernel(h_ref, modl_ref, modc_ref, w_ref, wa1_ref, wa2_ref, ba_ref, p_ref, ld_ref, *, t_lat):
    a = _modulated_rows(h_ref, modl_ref, modc_ref, t_lat)
    for cb in range(GLA_IN // MXU_N):
        z = _dot(a, w_ref[:, cb * MXU_N:(cb + 1) * MXU_N])
        if cb < GLA_KEY_DIM // MXU_N:
            z = z * (GLA_DK ** -0.5)
        p_ref[:, cb * MXU_N:(cb + 1) * MXU_N] = z.astype(p_ref.dtype)
    low = _dot(a, wa1_ref[...]).astype(MXU_DTYPE)
    for cb in range(2 * GLA_KEY_DIM // MXU_N):
        cols = slice(cb * MXU_N, (cb + 1) * MXU_N)
        logits = _dot(low, wa2_ref[:, cols]) + ba_ref[:, cols]
        ld_ref[:, cols] = _log_sigmoid(logits) * (LOG2_E / GLA_GATE_NORM)


def _gla_project(h, mods, layer, w_in, mixer, w_a1, w_a2, b_a, t_lat):
    b, s, d = h.shape
    tm = _proj_tile(s)
    return pl.pallas_call(
        functools.partial(_gla_proj_kernel, t_lat=t_lat),
        grid=(b, s // tm),
        in_specs=[
            pl.BlockSpec((None, tm, d), lambda bi, i: (bi, i, 0)),
            pl.BlockSpec((None, None, 6, d), lambda bi, i: (layer, bi, 0, 0)),
            pl.BlockSpec((None, None, 6, d), lambda bi, i: (layer, MOD_CTX_ROW, 0, 0)),
            pl.BlockSpec((None, d, GLA_IN), lambda bi, i: (mixer, 0, 0)),
            pl.BlockSpec((d, LANES), lambda bi, i: (0, 0)),
            pl.BlockSpec((LANES, 2 * GLA_KEY_DIM), lambda bi, i: (0, 0)),
            pl.BlockSpec((1, 2 * GLA_KEY_DIM), lambda bi, i: (0, 0)),
        ],
        out_specs=[
            pl.BlockSpec((None, tm, GLA_IN), lambda bi, i: (bi, i, 0)),
            pl.BlockSpec((None, tm, 2 * GLA_KEY_DIM), lambda bi, i: (bi, i, 0)),
        ],
        out_shape=[
            jax.ShapeDtypeStruct((b, s, GLA_IN), MXU_DTYPE),
            jax.ShapeDtypeStruct((b, s, 2 * GLA_KEY_DIM), F32),
        ],
        compiler_params=_cparams(("parallel", "parallel"), 48),
        name="gla_project",
    )(h, mods, mods, w_in, w_a1, w_a2, b_a)


def _cumsum_rows(x):
    rows, cols = x.shape
    sub = lax.broadcasted_iota(jnp.int32, x.shape, 0) & (SUBLANES - 1)
    shift = 1
    while shift < SUBLANES:
        x = x + jnp.where(sub >= shift, pltpu.roll(x, shift, 0), 0.0)
        shift *= 2
    x3 = x.reshape(rows // SUBLANES, SUBLANES, cols)
    carry = jnp.zeros((1, cols), F32)
    tiles = []
    for i in range(rows // SUBLANES):
        tile = x3[i] + carry
        tiles.append(tile)
        carry = tile[SUBLANES - 1:SUBLANES, :]
    return jnp.concatenate(tiles, axis=0)


def _block_anchor(p, n):
    rows, cols = p.shape
    half = n // 2
    if n >= SUBLANES:
        p3 = p.reshape(rows // n, n, cols)
        return jnp.broadcast_to(p3[:, half - 1:half, :], p3.shape).reshape(rows, cols)
    if n == 2:
        odd = (lax.broadcasted_iota(jnp.int32, p.shape, 0) & 1) == 1
        return jnp.where(odd, pltpu.roll(p, 1, 0), p)
    p3 = p.reshape(rows // SUBLANES, SUBLANES, cols)
    sub = lax.broadcasted_iota(jnp.int32, p3.shape, 1)
    a = None
    for b0 in range(0, SUBLANES, n):
        row = jnp.broadcast_to(p3[:, b0 + half - 1:b0 + half, :], p3.shape)
        a = row if a is None else jnp.where(sub >= b0, row, a)
    return a.reshape(rows, cols)


def _gla_chunk(q, k, v, ld, sign_ref, st_ref, reverse):
    chunk = q.shape[0]
    p_inc = _cumsum_rows(ld)
    p_tot = p_inc[chunk - 1:chunk, :]
    px = p_inc - ld if reverse else p_inc
    qf = q.astype(F32)
    kf = k.astype(F32)

    def decayed(x, exponent):
        return (x * jnp.exp2(exponent)).astype(MXU_DTYPE)

    row = lax.broadcasted_iota(jnp.int32, (chunk, chunk), 0)
    col = lax.broadcasted_iota(jnp.int32, (chunk, chunk), 1)
    differ = row ^ col
    att = jnp.broadcast_to(jnp.sum(qf * kf, axis=-1, keepdims=True), (chunk, chunk))
    n = 2
    while n <= chunk:
        level = n.bit_length() - 2
        exponent = (px - _block_anchor(p_inc, n)) * sign_ref[level]
        s_lvl = _dot_nt(decayed(qf, exponent), decayed(kf, exponent))
        att = jnp.where((differ >> level) == 1, s_lvl, att)
        n *= 2
    att = jnp.where(row <= col if reverse else row >= col, att, 0.0)

    if reverse:
        qd = decayed(qf, p_tot - px)
        kd = decayed(kf, px)
    else:
        qd = decayed(qf, px)
        kd = decayed(kf, p_tot - px)
    state = st_ref[...]
    o = _dot(att.astype(MXU_DTYPE), v) + _dot_nt(qd, state.astype(MXU_DTYPE))
    st_ref[...] = state * jnp.exp2(p_tot) + _dot_tn(v, kd)
    return o


def _gla_scan_kernel(sign_ref, qf_ref, kf_ref, vf_ref, ldf_ref, qr_ref, kr_ref, vr_ref, ldr_ref,
                     of_ref, or_ref, st_ref):
    @pl.when(pl.program_id(1) == 0)
    def _():
        st_ref[...] = jnp.zeros(st_ref.shape, F32)

    operands = ((qf_ref, kf_ref, vf_ref, ldf_ref, of_ref), (qr_ref, kr_ref, vr_ref, ldr_ref, or_ref))
    for direction, (q_ref, k_ref, v_ref, ld_ref, o_ref) in enumerate(operands):
        for hd in range(GLA_HEADS):
            kc = slice(hd * GLA_DK, (hd + 1) * GLA_DK)
            vc = slice(hd * GLA_DV, (hd + 1) * GLA_DV)
            o = _gla_chunk(q_ref[:, kc], k_ref[:, kc], v_ref[:, vc], ld_ref[:, kc], sign_ref,
                           st_ref.at[direction, hd], reverse=direction == 1)
            o_ref[:, vc] = o.astype(o_ref.dtype)


def _gla_scan(proj, ld, t_lat):
    b, s, _ = proj.shape
    ch = GLA_CHUNK
    n_chunks = s // ch
    lat_chunks = t_lat // ch

    def fwd(c):
        return (c + lat_chunks) % n_chunks

    def rev(c):
        return n_chunks - 1 - c

    def specs(chunk_of, direction):
        return [
            pl.BlockSpec((None, ch, GLA_KEY_DIM), lambda bi, c: (bi, chunk_of(c), 0)),
            pl.BlockSpec((None, ch, GLA_KEY_DIM), lambda bi, c: (bi, chunk_of(c), 1)),
            pl.BlockSpec((None, ch, GLA_VAL_DIM), lambda bi, c: (bi, chunk_of(c), 1)),
            pl.BlockSpec((None, ch, GLA_KEY_DIM), lambda bi, c: (bi, chunk_of(c), direction)),
        ]

    levels = ch.bit_length() - 1
    half_bit = 1 << jnp.arange(levels, dtype=jnp.int32)[:, None, None]
    in_upper = (jnp.arange(ch, dtype=jnp.int32)[None, :, None] & half_bit) != 0
    sign = jnp.broadcast_to(jnp.where(in_upper, 1.0, -1.0).astype(F32), (levels, ch, GLA_DK))
    o_shape = jax.ShapeDtypeStruct((b, s, GLA_VAL_DIM), F32)
    return pl.pallas_call(
        _gla_scan_kernel,
        grid=(b, n_chunks),
        in_specs=[pl.BlockSpec((levels, ch, GLA_DK), lambda bi, c: (0, 0, 0))] + specs(fwd, 0) + specs(rev, 1),
        out_specs=[
            pl.BlockSpec((None, ch, GLA_VAL_DIM), lambda bi, c: (bi, fwd(c), 0)),
            pl.BlockSpec((None, ch, GLA_VAL_DIM), lambda bi, c: (bi, rev(c), 0)),
        ],
        out_shape=[o_shape, o_shape],
        scratch_shapes=[pltpu.VMEM((2, GLA_HEADS, GLA_DV, GLA_DK), F32)],
        compiler_params=_cparams(("parallel", "arbitrary"), 32),
        name="gla_scan",
    )(sign, proj, proj, proj, ld, proj, proj, proj, ld)


def _with_halo(ref, next_ref):
    return jnp.concatenate([ref[...], next_ref[...]], axis=0)


def _mixer_out_norm(x, h_refs, mod_ref, wout_ref, lng_ref, lnb_ref, h1_scr, alpha):
    y = _dot(x, wout_ref[...])
    h1_scr[...] = _residual_layer_norm(_with_halo(*h_refs), y, mod_ref[2:3, :], lng_ref[...], lnb_ref[...], alpha)


def _attn_ffn_kernel(oa_ref, oan_ref, ob_ref, obn_ref, h_ref, hn_ref, mod_ref,
                     wout_ref, ln1g_ref, ln1b_ref, wup_ref, cw_ref, cb_ref, wdn_ref, ln2g_ref, ln2b_ref,
                     out_ref, h1_scr, a_scr, u_scr, act_scr, edge_scr, *, t_lat, s_tot, alpha):
    x = jnp.concatenate([_with_halo(oa_ref, oan_ref), _with_halo(ob_ref, obn_ref)], axis=1)
    _mixer_out_norm(x, (h_ref, hn_ref), mod_ref, wout_ref, ln1g_ref, ln1b_ref, h1_scr, alpha)
    _conv_ffn_norm(h1_scr, mod_ref, wup_ref, cw_ref, cb_ref, wdn_ref, ln2g_ref, ln2b_ref, out_ref,
                   a_scr, u_scr, act_scr, edge_scr, t_lat=t_lat, s_tot=s_tot, alpha=alpha)


def _gla_ffn_kernel(of_ref, ofn_ref, or_ref, orn_ref, g_ref, gn_ref,
                    h_ref, hn_ref, mod_ref, hnorm_ref, wout_ref, ln1g_ref, ln1b_ref,
                    wup_ref, cw_ref, cb_ref, wdn_ref, ln2g_ref, ln2b_ref,
                    out_ref, h1_scr, a_scr, u_scr, act_scr, edge_scr, *, t_lat, s_tot, alpha):
    o = _with_halo(of_ref, ofn_ref) + _with_halo(or_ref, orn_ref)
    gate = _silu(_with_halo(g_ref, gn_ref).astype(F32))
    heads = []
    for hd in range(GLA_HEADS):
        cols = slice(hd * GLA_DV, (hd + 1) * GLA_DV)
        oh = o[:, cols]
        oh = oh * lax.rsqrt(jnp.mean(oh * oh, axis=-1, keepdims=True) + EPS) * hnorm_ref[...]
        heads.append((oh * gate[:, cols]).astype(MXU_DTYPE))
    x = jnp.concatenate(heads, axis=1)
    _mixer_out_norm(x, (h_ref, hn_ref), mod_ref, wout_ref, ln1g_ref, ln1b_ref, h1_scr, alpha)
    _conv_ffn_norm(h1_scr, mod_ref, wup_ref, cw_ref, cb_ref, wdn_ref, ln2g_ref, ln2b_ref, out_ref,
                   a_scr, u_scr, act_scr, edge_scr, t_lat=t_lat, s_tot=s_tot, alpha=alpha)


def _conv_ffn_norm(h1_scr, mod_ref, wup_ref, cw_ref, cb_ref, wdn_ref, lng_ref, lnb_ref,
                   out_ref, a_scr, u_scr, act_scr, edge_scr, *, t_lat, s_tot, alpha):
    tm = out_ref.shape[0]
    d_ff = wdn_ref.shape[0]
    row0 = pl.program_id(1) * tm
    at_start = (row0 == 0) | (row0 == t_lat)
    at_end = (row0 + tm == t_lat) | (row0 + tm == s_tot)

    @pl.when(pl.program_id(1) == 0)
    def _():
        edge_scr[...] = jnp.zeros(edge_scr.shape, F32)

    shift = mod_ref[3:4, :]
    scale = 1.0 + mod_ref[4:5, :]
    a_all = h1_scr[...] * scale + shift
    a_scr[0:tm, :] = a_all[0:tm, :].astype(a_scr.dtype)
    a_scr[tm:tm + HALO, :] = jnp.where(at_end, 0.0, a_all[tm:tm + HALO, :]).astype(a_scr.dtype)
    a = a_scr[...]

    def conv(u_ref, c0):
        cols = slice(c0, c0 + FF_CHUNK)
        acc = cb_ref[:, cols] + cw_ref[0:1, cols] * u_ref[pl.ds(HALO - 1, tm), :]
        for j in range(1, CONV_W):
            acc = acc + cw_ref[j:j + 1, cols] * u_ref[pl.ds(HALO - 1 + j, tm), :]
        return acc

    def up_project(c):
        for half, c0 in enumerate((c * FF_CHUNK, d_ff + c * FF_CHUNK)):
            u_ref = u_scr.at[c % U_SLOTS, half]
            u_ref[HALO:, :] = _dot(a, wup_ref[:, c0:c0 + FF_CHUNK])
            u_ref[HALO - 1:HALO, :] = jnp.where(at_start, 0.0, edge_scr[c, half, 0:1, :])
            edge_scr[c, half, 0:1, :] = u_ref[HALO + tm - 1:HALO + tm, :]

    def down_project(c):
        return _dot(act_scr[c % ACT_SLOTS], wdn_ref[c * FF_CHUNK:(c + 1) * FF_CHUNK, :])

    n_chunks = d_ff // FF_CHUNK
    ahead = U_SLOTS - 1
    y = jnp.zeros(out_ref.shape, F32)
    for c in range(min(ahead, n_chunks)):
        up_project(c)
    for c in range(n_chunks):
        if c + ahead < n_chunks:
            up_project(c + ahead)
        if c > 0:
            y = y + down_project(c - 1)
        g0 = c * FF_CHUNK
        act = _silu(conv(u_scr.at[c % U_SLOTS, 0], g0)) * conv(u_scr.at[c % U_SLOTS, 1], d_ff + g0)
        act_scr[c % ACT_SLOTS] = act.astype(act_scr.dtype)
    y = y + down_project(n_chunks - 1)
    out_ref[...] = _residual_layer_norm(h1_scr[0:tm, :], y, mod_ref[5:6, :], lng_ref[...], lnb_ref[...], alpha)


def _mixer_ffn(kernel_fn, name, mixer_ins, h, mods, layer, extra_params, w_out, mixer, ln_g, ln_b,
               w_up, conv_w, conv_b, w_down, t_lat, n_rows, alpha):
    b, s, d = h.shape
    tm = ROW_TILE
    d_ff = w_down.shape[1]
    row = _mod_row(t_lat, tm)
    hb = tm // HALO
    n_halo = s // HALO

    def halo_specs(cols, cb):
        return [
            pl.BlockSpec((None, tm, cols), lambda bi, i: (bi, i, cb)),
            pl.BlockSpec((None, HALO, cols), lambda bi, i: (bi, jnp.minimum((i + 1) * hb, n_halo - 1), cb)),
        ]

    row_ins = list(mixer_ins) + [(h, d, 0)]
    in_specs, operands = [], []
    for arr, cols, cb in row_ins:
        in_specs += halo_specs(cols, cb)
        operands += [arr, arr]
    in_specs.append(pl.BlockSpec((None, None, 6, d), lambda bi, i: (layer, row(bi, i), 0, 0)))
    operands.append(mods)
    for p in extra_params:
        in_specs.append(pl.BlockSpec(p.shape, lambda bi, i: (0, 0)))
        operands.append(p)
    in_specs += [
        pl.BlockSpec((None,) + w_out.shape[1:], lambda bi, i: (mixer, 0, 0)),
        pl.BlockSpec((None, 1, d), lambda bi, i: (layer, 0, 0)),
        pl.BlockSpec((None, 1, d), lambda bi, i: (layer, 0, 0)),
        pl.BlockSpec((None, d, 2 * d_ff), lambda bi, i: (layer, 0, 0)),
        pl.BlockSpec((None, CONV_W, 2 * d_ff), lambda bi, i: (layer, 0, 0)),
        pl.BlockSpec((None, 1, 2 * d_ff), lambda bi, i: (layer, 0, 0)),
        pl.BlockSpec((None, d_ff, d), lambda bi, i: (layer, 0, 0)),
        pl.BlockSpec((None, 1, d), lambda bi, i: (layer, 0, 0)),
        pl.BlockSpec((None, 1, d), lambda bi, i: (layer, 0, 0)),
    ]
    operands += [w_out, ln_g[:, 0:1, :], ln_b[:, 0:1, :], w_up, conv_w, conv_b.reshape(-1, 1, 2 * d_ff), w_down,
                 ln_g[:, 1:2, :], ln_b[:, 1:2, :]]
    return pl.pallas_call(
        functools.partial(kernel_fn, t_lat=t_lat, s_tot=s, alpha=alpha),
        grid=(b, n_rows // tm),
        in_specs=in_specs,
        out_specs=pl.BlockSpec((None, tm, d), lambda bi, i: (bi, i, 0)),
        out_shape=jax.ShapeDtypeStruct((b, n_rows, d), F32),
        scratch_shapes=[
            pltpu.VMEM((tm + HALO, d), F32),
            pltpu.VMEM((tm + HALO, d), MXU_DTYPE),
            pltpu.VMEM((U_SLOTS, 2, tm + 2 * HALO, FF_CHUNK), F32),
            pltpu.VMEM((ACT_SLOTS, tm, FF_CHUNK), MXU_DTYPE),
            pltpu.VMEM((d_ff // FF_CHUNK, 2, SUBLANES, FF_CHUNK), F32),
        ],
        compiler_params=_cparams(("parallel", "arbitrary"), 60),
        name=name,
    )(*operands)


def _rope_tables(t_lat, c_len):
    rows = t_lat // GRID_W
    n_freq = ROPE_AXIS_DIM // 2
    inv_freq = jnp.power(ROPE_THETA, -jnp.arange(n_freq, dtype=F32) * 2.0 / ROPE_AXIS_DIM)
    ang_row = jnp.arange(rows, dtype=F32)[:, None] * inv_freq
    ang_col = jnp.arange(GRID_W, dtype=F32)[:, None] * inv_freq

    def table(fn, ctx_value):
        by_row = jnp.broadcast_to(fn(ang_row)[:, None, :], (rows, GRID_W, n_freq))
        by_col = jnp.broadcast_to(fn(ang_col)[None, :, :], (rows, GRID_W, n_freq))
        lat = jnp.concatenate([by_row, by_col], axis=-1).reshape(t_lat, HEAD_DIM // 2)
        return jnp.concatenate([lat, jnp.full((c_len, HEAD_DIM // 2), ctx_value, F32)], axis=0)

    cos, sin = table(jnp.cos, 1.0), table(jnp.sin, 0.0)
    return jnp.concatenate([cos, cos], axis=-1), jnp.concatenate([-sin, sin], axis=-1)


def kernel(x, c, ctx, c_ctx, ada_w, ada_b, ln_g, ln_b, ffn_w_up, ffn_conv_w, ffn_conv_b, ffn_w_down, attn_w_in, attn_q_norm, attn_k_norm, attn_sink, attn_w_out, gla_w_in, gla_w_a1, gla_w_a2, gla_b_a, gla_head_norm, gla_w_out):
    b, t_lat, d = x.shape
    c_len = ctx.shape[1]
    s = t_lat + c_len
    depth = ada_w.shape[0]
    assert b < MOD_CTX_ROW and t_lat % ROW_TILE == 0 and c_len % ROW_TILE == 0
    assert t_lat % GLA_CHUNK == 0 and c_len % GLA_CHUNK == 0 and s % MXU_N == 0
    alpha = (2 * depth) ** 0.25

    cond = jnp.zeros((MOD_ROWS, d), F32).at[:b].set(c).at[MOD_CTX_ROW].set(c_ctx)
    mods = _modulation(cond, ada_w, ada_b).reshape(depth, MOD_ROWS, 6, d)
    cos, sin = _rope_tables(t_lat, c_len)
    h = jnp.concatenate([x, ctx], axis=1)
    attn_w_in_c, attn_w_out_c = attn_w_in.astype(MXU_DTYPE), attn_w_out.astype(MXU_DTYPE)
    gla_w_in_c, gla_w_out_c = gla_w_in.astype(MXU_DTYPE), gla_w_out.astype(MXU_DTYPE)
    ffn_w_up_c, ffn_w_down_c = ffn_w_up.astype(MXU_DTYPE), ffn_w_down.astype(MXU_DTYPE)

    for i in range(depth):
        last = i == depth - 1
        n_rows = t_lat if last else s
        j = i // 2
        if i % 2 == 0:
            qkv = _attn_project(h, mods, i, attn_w_in_c, j, attn_q_norm[j], attn_k_norm[j],
                                cos, sin, t_lat)
            o_a = _global_attention(qkv, t_lat)
            o_b = _window_attention(qkv, attn_sink[j], t_lat)
            h = _mixer_ffn(_attn_ffn_kernel, "attn_out_ffn", [(o_a, A_QW, 0), (o_b, B_QW, 0)], h, mods, i, [],
                           attn_w_out_c, j, ln_g, ln_b, ffn_w_up_c, ffn_conv_w, ffn_conv_b, ffn_w_down_c,
                           t_lat, n_rows, alpha)
        else:
            w_a1 = jnp.concatenate([gla_w_a1[j, 0], gla_w_a1[j, 1]], axis=1)
            w_a1 = jnp.pad(w_a1, ((0, 0), (0, LANES - 2 * GLA_GATE_RANK))).astype(MXU_DTYPE)
            w_a2 = jnp.zeros((LANES, 2 * GLA_KEY_DIM), F32)
            w_a2 = w_a2.at[:GLA_GATE_RANK, :GLA_KEY_DIM].set(gla_w_a2[j, 0])
            w_a2 = w_a2.at[GLA_GATE_RANK:2 * GLA_GATE_RANK, GLA_KEY_DIM:].set(gla_w_a2[j, 1]).astype(MXU_DTYPE)
            proj, ld = _gla_project(h, mods, i, gla_w_in_c, j, w_a1, w_a2,
                                    gla_b_a[j].reshape(1, 2 * GLA_KEY_DIM), t_lat)
            o_f, o_r = _gla_scan(proj, ld, t_lat)
            g_blk = (2 * GLA_KEY_DIM + GLA_VAL_DIM) // GLA_VAL_DIM
            h = _mixer_ffn(_gla_ffn_kernel, "gla_out_ffn",
                           [(o_f, GLA_VAL_DIM, 0), (o_r, GLA_VAL_DIM, 0), (proj, GLA_VAL_DIM, g_blk)], h, mods, i,
                           [gla_head_norm[j].reshape(1, GLA_DV)], gla_w_out_c, j, ln_g, ln_b,
                           ffn_w_up_c, ffn_conv_w, ffn_conv_b, ffn_w_down_c, t_lat, n_rows, alpha)
    return h
```

```python
import functools

import jax
import jax.numpy as jnp
from jax import lax
from jax.experimental import pallas as pl
from jax.experimental.pallas import tpu as pltpu

GRID_W = 64
HEAD_DIM = 128
A_Q_HEADS = 4
A_KV_HEADS = 2
B_Q_HEADS = 4
B_KV_HEADS = 2
A_QW = A_Q_HEADS * HEAD_DIM
A_KVW = A_KV_HEADS * HEAD_DIM
B_QW = B_Q_HEADS * HEAD_DIM
B_KVW = B_KV_HEADS * HEAD_DIM
ATTN_IN = A_QW + 2 * A_KVW + B_QW + 2 * B_KVW
ATTN_SCALE = HEAD_DIM ** -0.5
WINDOW = 128
ROPE_THETA = 10000.0
ROPE_AXIS_DIM = HEAD_DIM // 2
GLA_HEADS = 4
GLA_DK = 128
GLA_DV = 256
GLA_KEY_DIM = GLA_HEADS * GLA_DK
GLA_VAL_DIM = GLA_HEADS * GLA_DV
GLA_IN = 2 * GLA_KEY_DIM + 2 * GLA_VAL_DIM
GLA_GATE_RANK = 16
GLA_GATE_NORM = 16.0
CONV_W = 3
EPS = 1e-6

LANES = 128
SUBLANES = 8
MXU_N = 256
VMEM_BYTES = 64 * 2 ** 20

MXU_DTYPE = jnp.bfloat16
F32 = jnp.float32

ROW_TILE = 256
PROJ_TILE = 768
HALO = 16
FF_CHUNK = 256
U_SLOTS = 3
ACT_SLOTS = 2
KV_CHUNK = 2816
KV_EDGE = 512
KV_SLOTS = 2
SOFTMAX_ROWS = 16
LOG2_E = 1.4426950408889634
GLA_CHUNK = 128
MOD_COLS = 1536


def _cparams(sem, vmem_mib):
    return pltpu.CompilerParams(dimension_semantics=sem, vmem_limit_bytes=vmem_mib * 2 ** 20)


def _dot(a, b):
    return jnp.dot(a, b, preferred_element_type=F32)


def _dot_nt(a, b):
    return lax.dot_general(a, b, (((1,), (1,)), ((), ())), preferred_element_type=F32)


def _dot_tn(a, b):
    return lax.dot_general(a, b, (((0,), (0,)), ((), ())), preferred_element_type=F32)


def _silu(x):
    return x / (1.0 + jnp.exp(-x))


def _residual_layer_norm(h, y, gate, ln_g, ln_b, alpha):
    r = alpha * h + gate * y
    mu = jnp.mean(r, axis=-1, keepdims=True)
    d = r - mu
    var = jnp.mean(d * d, axis=-1, keepdims=True)
    return d * lax.rsqrt(var + EPS) * ln_g + ln_b


def _modulated_rows(h_ref, modl_ref, modc_ref, t_lat):
    tm = h_ref.shape[0]
    rows = pl.program_id(1) * tm + lax.broadcasted_iota(jnp.int32, (tm, 1), 0)
    is_ctx = rows >= t_lat
    scale = 1.0 + jnp.where(is_ctx, modc_ref[1:2, :], modl_ref[1:2, :])
    shift = jnp.where(is_ctx, modc_ref[0:1, :], modl_ref[0:1, :])
    return (h_ref[...] * scale + shift).astype(MXU_DTYPE)


def _proj_tile(s_tot):
    return max(n for n in range(ROW_TILE, PROJ_TILE + 1, ROW_TILE) if s_tot % n == 0)


def _mod_row(t_lat, tm):
    n_lat = t_lat // tm

    def row(b, i):
        return jnp.where(i < n_lat, b, MOD_CTX_ROW)
    return row


MOD_ROWS = 8
MOD_CTX_ROW = MOD_ROWS - 1


def _mod_kernel(cond_ref, w_ref, b_ref, o_ref):
    a = _silu(cond_ref[...]).astype(MXU_DTYPE)
    o_ref[...] = _dot(a, w_ref[...].astype(MXU_DTYPE)) + b_ref[...]


def _modulation(cond, ada_w, ada_b):
    depth, d, n = ada_w.shape
    return pl.pallas_call(
        _mod_kernel,
        grid=(depth, n // MOD_COLS),
        in_specs=[
            pl.BlockSpec((MOD_ROWS, d), lambda l, j: (0, 0)),
            pl.BlockSpec((None, d, MOD_COLS), lambda l, j: (l, 0, j)),
            pl.BlockSpec((None, 1, MOD_COLS), lambda l, j: (l, 0, j)),
        ],
        out_specs=pl.BlockSpec((None, MOD_ROWS, MOD_COLS), lambda l, j: (l, 0, j)),
        out_shape=jax.ShapeDtypeStruct((depth, MOD_ROWS, n), F32),
        compiler_params=_cparams(("parallel", "parallel"), 32),
        name="ada_modulation",
    )(cond, ada_w, ada_b.reshape(depth, 1, n))


def _attn_proj_kernel(h_ref, modl_ref, modc_ref, w_ref, qn_ref, kn_ref, cos_ref, sin_ref, o_ref, *, t_lat):
    a = _modulated_rows(h_ref, modl_ref, modc_ref, t_lat)
    cos = cos_ref[...]
    sin = sin_ref[...]

    def rms(z, g):
        return z * lax.rsqrt(jnp.mean(z * z, axis=-1, keepdims=True) + EPS) * g

    def rope(z):
        return z * cos + pltpu.roll(z, HEAD_DIM // 2, 1) * sin

    a_k0 = A_QW // HEAD_DIM
    a_v0 = a_k0 + A_KV_HEADS
    b_q0 = a_v0 + A_KV_HEADS
    b_k0 = b_q0 + B_Q_HEADS
    b_v0 = b_k0 + B_KV_HEADS
    heads_per_dot = MXU_N // HEAD_DIM
    for cb in range(ATTN_IN // MXU_N):
        z2 = _dot(a, w_ref[:, cb * MXU_N:(cb + 1) * MXU_N])
        for half in range(heads_per_dot):
            hb = cb * heads_per_dot + half
            z = z2[:, half * HEAD_DIM:(half + 1) * HEAD_DIM]
            if hb < a_k0:
                z = rope(rms(z, qn_ref[...])) * (ATTN_SCALE * LOG2_E)
            elif hb < a_v0:
                z = rope(rms(z, kn_ref[...]))
            elif hb < b_q0:
                pass
            elif hb < b_k0:
                z = rope(z) * ATTN_SCALE
            elif hb < b_v0:
                z = rope(z)
            o_ref[:, hb * HEAD_DIM:(hb + 1) * HEAD_DIM] = z.astype(o_ref.dtype)


def _attn_project(h, mods, layer, w_in, mixer, q_norm, k_norm, cos, sin, t_lat):
    b, s, d = h.shape
    tm = ROW_TILE
    return pl.pallas_call(
        functools.partial(_attn_proj_kernel, t_lat=t_lat),
        grid=(b, s // tm),
        in_specs=[
            pl.BlockSpec((None, tm, d), lambda bi, i: (bi, i, 0)),
            pl.BlockSpec((None, None, 6, d), lambda bi, i: (layer, bi, 0, 0)),
            pl.BlockSpec((None, None, 6, d), lambda bi, i: (layer, MOD_CTX_ROW, 0, 0)),
            pl.BlockSpec((None, d, ATTN_IN), lambda bi, i: (mixer, 0, 0)),
            pl.BlockSpec((1, HEAD_DIM), lambda bi, i: (0, 0)),
            pl.BlockSpec((1, HEAD_DIM), lambda bi, i: (0, 0)),
            pl.BlockSpec((tm, HEAD_DIM), lambda bi, i: (i, 0)),
            pl.BlockSpec((tm, HEAD_DIM), lambda bi, i: (i, 0)),
        ],
        out_specs=pl.BlockSpec((None, tm, ATTN_IN), lambda bi, i: (bi, i, 0)),
        out_shape=jax.ShapeDtypeStruct((b, s, ATTN_IN), MXU_DTYPE),
        compiler_params=_cparams(("parallel", "parallel"), 40),
        name="attn_project",
    )(h, mods, mods, w_in, q_norm.reshape(1, HEAD_DIM), k_norm.reshape(1, HEAD_DIM), cos, sin)


def _stack_heads(q):
    g = q.shape[1] // HEAD_DIM
    return jnp.concatenate([q[:, i * HEAD_DIM:(i + 1) * HEAD_DIM] for i in range(g)], axis=0)


def _store_heads(o_ref, o, tq):
    for i in range(o.shape[0] // tq):
        o_ref[:, i * HEAD_DIM:(i + 1) * HEAD_DIM] = o[i * tq:(i + 1) * tq, :].astype(o_ref.dtype)


def _softmax_chunk(s_ref, m_scr, alpha_ref, p_ref):
    rows, n_keys = s_ref.shape
    for g in range(rows // SOFTMAX_ROWS):
        rg = slice(g * SOFTMAX_ROWS, (g + 1) * SOFTMAX_ROWS)
        blocks = [s_ref[rg, j * LANES:(j + 1) * LANES] for j in range(n_keys // LANES)]
        mx = blocks[0]
        for blk in blocks[1:]:
            mx = jnp.maximum(mx, blk)
        m_prev = m_scr[rg, :]
        m_new = jnp.maximum(m_prev, jnp.max(mx, axis=-1, keepdims=True))
        alpha_ref[rg, :] = jnp.exp2(m_prev - m_new)
        for j, blk in enumerate(blocks):
            p_ref[rg, j * LANES:(j + 1) * LANES] = jnp.exp2(blk - m_new).astype(p_ref.dtype)
        m_scr[rg, :] = m_new


def _global_attn_kernel(q_ref, k_ref, v_ref, o_ref, vext_scr, m_scr, acc_scr, s_scr, p_scr, alpha_scr,
                        *, t_lat, tq, kv_chunks):
    s_tot = k_ref.shape[0]
    c_len = s_tot - t_lat
    is_ctx = pl.program_id(2) * tq >= t_lat

    @pl.when(pl.program_id(2) == 0)
    def _():
        ones_col = jnp.where(lax.broadcasted_iota(jnp.int32, (MXU_N, HEAD_DIM), 1) == 0, 1.0, 0.0)

        def fill(blk, carry):
            rows = pl.ds(pl.multiple_of(blk * MXU_N, MXU_N), MXU_N)
            vext_scr[rows, 0:HEAD_DIM] = v_ref[rows, :]
            vext_scr[rows, HEAD_DIM:2 * HEAD_DIM] = ones_col.astype(vext_scr.dtype)
            return carry

        lax.fori_loop(0, s_tot // MXU_N, fill, 0)

    q2 = _stack_heads(q_ref[...])
    m_scr[...] = jnp.full(m_scr.shape, -jnp.inf, F32)
    acc_scr[...] = jnp.zeros(acc_scr.shape, F32)

    def keys(c):
        return slice(kv_chunks[c][0], kv_chunks[c][0] + kv_chunks[c][1])

    def width(c):
        return slice(0, kv_chunks[c][1])

    def accumulate(p, v_ext, alpha):
        acc_scr[...] = jnp.concatenate([alpha, alpha], axis=1) * acc_scr[...] + _dot(p, v_ext)

    @pl.when(jnp.logical_not(is_ctx))
    def _():
        n_chunks = len(kv_chunks)
        s_scr[0, :, width(0)] = _dot_nt(q2, k_ref[keys(0), :])
        for c in range(n_chunks):
            slot = c % KV_SLOTS
            if c + 1 < n_chunks:
                s_scr[(c + 1) % KV_SLOTS, :, width(c + 1)] = _dot_nt(q2, k_ref[keys(c + 1), :])
            if c > 0:
                prev = (c - 1) % KV_SLOTS
                accumulate(p_scr[prev, :, width(c - 1)], vext_scr[keys(c - 1), :], alpha_scr[prev])
            _softmax_chunk(s_scr.at[slot, :, width(c)], m_scr, alpha_scr.at[slot], p_scr.at[slot, :, width(c)])
        last = (n_chunks - 1) % KV_SLOTS
        accumulate(p_scr[last, :, width(n_chunks - 1)], vext_scr[keys(n_chunks - 1), :], alpha_scr[last])

    @pl.when(is_ctx)
    def _():
        s_scr[0, :, 0:c_len] = _dot_nt(q2, k_ref[t_lat:s_tot, :])
        _softmax_chunk(s_scr.at[0, :, 0:c_len], m_scr, alpha_scr.at[0], p_scr.at[0, :, 0:c_len])
        accumulate(p_scr[0, :, 0:c_len], vext_scr[t_lat:s_tot, :], alpha_scr[0])

    acc = acc_scr[...]
    _store_heads(o_ref, acc[:, 0:HEAD_DIM] / acc[:, HEAD_DIM:HEAD_DIM + 1], tq)


def _kv_chunks(s_tot):
    units = s_tot // MXU_N
    edge = KV_EDGE // MXU_N
    if units <= 2 * edge:
        sizes = [1] * units
    else:
        n_mid = -(-(units - 2 * edge) // (KV_CHUNK // MXU_N))
        base, extra = divmod(units - 2 * edge, n_mid)
        sizes = [edge] + [base + (i < extra) for i in range(n_mid)] + [edge]
    chunks, start = [], 0
    for n in sizes:
        chunks.append((start * MXU_N, n * MXU_N))
        start += n
    return tuple(chunks)


def _global_attention(qkv, t_lat):
    b, s, _ = qkv.shape
    tq = ROW_TILE
    group = A_Q_HEADS // A_KV_HEADS
    k0 = A_QW // HEAD_DIM
    v0 = k0 + A_KV_HEADS
    kv_chunks = _kv_chunks(s)
    kv_chunk = max(size for _, size in kv_chunks)
    assert s - t_lat <= kv_chunk
    return pl.pallas_call(
        functools.partial(_global_attn_kernel, t_lat=t_lat, tq=tq, kv_chunks=kv_chunks),
        grid=(b, A_KV_HEADS, s // tq),
        in_specs=[
            pl.BlockSpec((None, tq, group * HEAD_DIM), lambda bi, hk, i: (bi, i, hk)),
            pl.BlockSpec((None, s, HEAD_DIM), lambda bi, hk, i: (bi, 0, k0 + hk)),
            pl.BlockSpec((None, s, HEAD_DIM), lambda bi, hk, i: (bi, 0, v0 + hk)),
        ],
        out_specs=pl.BlockSpec((None, tq, group * HEAD_DIM), lambda bi, hk, i: (bi, i, hk)),
        out_shape=jax.ShapeDtypeStruct((b, s, A_QW), MXU_DTYPE),
        scratch_shapes=[
            pltpu.VMEM((s, 2 * HEAD_DIM), MXU_DTYPE),
            pltpu.VMEM((group * tq, LANES), F32),
            pltpu.VMEM((group * tq, 2 * HEAD_DIM), F32),
            pltpu.VMEM((KV_SLOTS, group * tq, kv_chunk), F32),
            pltpu.VMEM((KV_SLOTS, group * tq, kv_chunk), MXU_DTYPE),
            pltpu.VMEM((KV_SLOTS, group * tq, LANES), F32),
        ],
        compiler_params=_cparams(("parallel", "parallel", "arbitrary"), 40),
        name="global_attention",
    )(qkv, qkv, qkv)


def _window_attn_kernel(sink_ref, q_ref, k_ref, v_ref, o_ref, *, t_lat, tq):
    s_tot = k_ref.shape[0]
    c_len = s_tot - t_lat
    band = tq + 2 * WINDOW
    q0 = pl.program_id(1) * tq
    is_ctx = q0 >= t_lat
    start = pl.multiple_of(jnp.clip(q0 - WINDOW, 0, t_lat - band), WINDOW)
    group = B_Q_HEADS // B_KV_HEADS

    delta = (lax.broadcasted_iota(jnp.int32, (tq, band), 1) + (start - q0)
             - lax.broadcasted_iota(jnp.int32, (tq, band), 0))
    reach = jnp.where(is_ctx, -1, WINDOW)
    valid = jnp.abs(delta) <= reach
    for hk in range(B_KV_HEADS):
        kv = slice(hk * HEAD_DIM, (hk + 1) * HEAD_DIM)
        q2 = _stack_heads(q_ref[:, hk * group * HEAD_DIM:(hk + 1) * group * HEAD_DIM])
        s_band = _dot_nt(q2, k_ref[pl.ds(start, band), kv])
        s_ctx = _dot_nt(q2, k_ref[pl.ds(t_lat, c_len), kv])
        p_band, p_ctx, denom = [], [], []
        for g in range(group):
            head = slice(g * tq, (g + 1) * tq)
            sb = jnp.where(valid, s_band[head, :], -jnp.inf)
            sc = s_ctx[head, :]
            sink = sink_ref[hk * group + g]
            m = jnp.maximum(jnp.maximum(jnp.max(sb, axis=-1, keepdims=True),
                                        jnp.max(sc, axis=-1, keepdims=True)), sink)
            pb = jnp.exp(sb - m)
            pc = jnp.exp(sc - m)
            denom.append(jnp.sum(pb, axis=-1, keepdims=True) + jnp.sum(pc, axis=-1, keepdims=True)
                         + jnp.exp(sink - m))
            p_band.append(pb.astype(MXU_DTYPE))
            p_ctx.append(pc.astype(MXU_DTYPE))
        o = (_dot(jnp.concatenate(p_band, axis=0), v_ref[pl.ds(start, band), kv])
             + _dot(jnp.concatenate(p_ctx, axis=0), v_ref[pl.ds(t_lat, c_len), kv]))
        o = o / jnp.concatenate(denom, axis=0)
        for g in range(group):
            cols = slice((hk * group + g) * HEAD_DIM, (hk * group + g + 1) * HEAD_DIM)
            o_ref[:, cols] = o[g * tq:(g + 1) * tq, :].astype(o_ref.dtype)


def _window_attention(qkv, sink, t_lat):
    b, s, _ = qkv.shape
    tq = ROW_TILE
    assert tq & (tq - 1) == 0 and t_lat >= tq + 2 * WINDOW
    q0 = (A_QW + 2 * A_KVW) // B_QW
    k0 = (A_QW + 2 * A_KVW + B_QW) // B_KVW
    v0 = k0 + 1
    return pl.pallas_call(
        functools.partial(_window_attn_kernel, t_lat=t_lat, tq=tq),
        grid=(b, s // tq),
        in_specs=[
            pl.BlockSpec(memory_space=pltpu.SMEM),
            pl.BlockSpec((None, tq, B_QW), lambda bi, i: (bi, i, q0)),
            pl.BlockSpec((None, s, B_KVW), lambda bi, i: (bi, 0, k0)),
            pl.BlockSpec((None, s, B_KVW), lambda bi, i: (bi, 0, v0)),
        ],
        out_specs=pl.BlockSpec((None, tq, B_QW), lambda bi, i: (bi, i, 0)),
        out_shape=jax.ShapeDtypeStruct((b, s, B_QW), MXU_DTYPE),
        compiler_params=_cparams(("parallel", "arbitrary"), 40),
        name="window_attention",
    )(sink, qkv, qkv, qkv)


def _log_sigmoid(x):
    return jnp.minimum(x, 0.0) - jnp.log(1.0 + jnp.exp(-jnp.abs(x)))


def _gla_proj_kernel(h_ref, modl_ref, modc_ref, w_ref, wa1_ref, wa2_ref, ba_ref, p_ref, ld_ref, *, t_lat):
    a = _modulated_rows(h_ref, modl_ref, modc_ref, t_lat)
    for cb in range(GLA_IN // MXU_N):
        z = _dot(a, w_ref[:, cb * MXU_N:(cb + 1) * MXU_N])
        if cb < GLA_KEY_DIM // MXU_N:
            z = z * (GLA_DK ** -0.5)
        p_ref[:, cb * MXU_N:(cb + 1) * MXU_N] = z.astype(p_ref.dtype)
    low = _dot(a, wa1_ref[...]).astype(MXU_DTYPE)
    for cb in range(2 * GLA_KEY_DIM // MXU_N):
        cols = slice(cb * MXU_N, (cb + 1) * MXU_N)
        logits = _dot(low, wa2_ref[:, cols]) + ba_ref[:, cols]
        ld_ref[:, cols] = _log_sigmoid(logits) * (LOG2_E / GLA_GATE_NORM)


def _gla_project(h, mods, layer, w_in, mixer, w_a1, w_a2, b_a, t_lat):
    b, s, d = h.shape
    tm = _proj_tile(s)
    return pl.pallas_call(
        functools.partial(_gla_proj_kernel, t_lat=t_lat),
        grid=(b, s // tm),
        in_specs=[
            pl.BlockSpec((None, tm, d), lambda bi, i: (bi, i, 0)),
            pl.BlockSpec((None, None, 6, d), lambda bi, i: (layer, bi, 0, 0)),
            pl.BlockSpec((None, None, 6, d), lambda bi, i: (layer, MOD_CTX_ROW, 0, 0)),
            pl.BlockSpec((None, d, GLA_IN), lambda bi, i: (mixer, 0, 0)),
            pl.BlockSpec((d, LANES), lambda bi, i: (0, 0)),
            pl.BlockSpec((LANES, 2 * GLA_KEY_DIM), lambda bi, i: (0, 0)),
            pl.BlockSpec((1, 2 * GLA_KEY_DIM), lambda bi, i: (0, 0)),
        ],
        out_specs=[
            pl.BlockSpec((None, tm, GLA_IN), lambda bi, i: (bi, i, 0)),
            pl.BlockSpec((None, tm, 2 * GLA_KEY_DIM), lambda bi, i: (bi, i, 0)),
        ],
        out_shape=[
            jax.ShapeDtypeStruct((b, s, GLA_IN), MXU_DTYPE),
            jax.ShapeDtypeStruct((b, s, 2 * GLA_KEY_DIM), F32),
        ],
        compiler_params=_cparams(("parallel", "parallel"), 48),
        name="gla_project",
    )(h, mods, mods, w_in, w_a1, w_a2, b_a)


def _cumsum_rows(x):
    rows, cols = x.shape
    sub = lax.broadcasted_iota(jnp.int32, x.shape, 0) & (SUBLANES - 1)
    shift = 1
    while shift < SUBLANES:
        x = x + jnp.where(sub >= shift, pltpu.roll(x, shift, 0), 0.0)
        shift *= 2
    x3 = x.reshape(rows // SUBLANES, SUBLANES, cols)
    carry = jnp.zeros((1, cols), F32)
    tiles = []
    for i in range(rows // SUBLANES):
        tile = x3[i] + carry
        tiles.append(tile)
        carry = tile[SUBLANES - 1:SUBLANES, :]
    return jnp.concatenate(tiles, axis=0)


def _block_anchor(p, n):
    rows, cols = p.shape
    half = n // 2
    if n >= SUBLANES:
        p3 = p.reshape(rows // n, n, cols)
        return jnp.broadcast_to(p3[:, half - 1:half, :], p3.shape).reshape(rows, cols)
    if n == 2:
        odd = (lax.broadcasted_iota(jnp.int32, p.shape, 0) & 1) == 1
        return jnp.where(odd, pltpu.roll(p, 1, 0), p)
    p3 = p.reshape(rows // SUBLANES, SUBLANES, cols)
    sub = lax.broadcasted_iota(jnp.int32, p3.shape, 1)
    a = None
    for b0 in range(0, SUBLANES, n):
        row = jnp.broadcast_to(p3[:, b0 + half - 1:b0 + half, :], p3.shape)
        a = row if a is None else jnp.where(sub >= b0, row, a)
    return a.reshape(rows, cols)


def _gla_chunk(q, k, v, ld, sign_ref, pair_ref, st_ref, reverse):
    chunk = q.shape[0]
    p_inc = _cumsum_rows(ld)
    p_tot = p_inc[chunk - 1:chunk, :]
    px = p_inc - ld if reverse else p_inc
    qf = q.astype(F32)
    kf = k.astype(F32)

    def decayed(x, exponent):
        return (x * jnp.exp2(exponent)).astype(MXU_DTYPE)

    row = lax.broadcasted_iota(jnp.int32, (chunk, chunk), 0)
    col = lax.broadcasted_iota(jnp.int32, (chunk, chunk), 1)
    att = jnp.where(row == col, jnp.sum(qf * kf, axis=-1, keepdims=True), 0.0)
    n = 2
    while n <= chunk:
        level = n.bit_length() - 2
        exponent = (px - _block_anchor(p_inc, n)) * sign_ref[level]
        s_lvl = _dot_nt(decayed(qf, exponent), decayed(kf, exponent))
        att = att + s_lvl * pair_ref[1 if reverse else 0, level]
        n *= 2

    if reverse:
        qd = decayed(qf, p_tot - px)
        kd = decayed(kf, px)
    else:
        qd = decayed(qf, px)
        kd = decayed(kf, p_tot - px)
    state = st_ref[...]
    o = _dot(att.astype(MXU_DTYPE), v) + _dot_nt(qd, state.astype(MXU_DTYPE))
    st_ref[...] = state * jnp.exp2(p_tot) + _dot_tn(v, kd)
    return o


def _gla_scan_kernel(sign_ref, pair_ref, qf_ref, kf_ref, vf_ref, ldf_ref, qr_ref, kr_ref, vr_ref, ldr_ref,
                     of_ref, or_ref, st_ref):
    @pl.when(pl.program_id(1) == 0)
    def _():
        st_ref[...] = jnp.zeros(st_ref.shape, F32)

    operands = ((qf_ref, kf_ref, vf_ref, ldf_ref, of_ref), (qr_ref, kr_ref, vr_ref, ldr_ref, or_ref))
    for direction, (q_ref, k_ref, v_ref, ld_ref, o_ref) in enumerate(operands):
        for hd in range(GLA_HEADS):
            kc = slice(hd * GLA_DK, (hd + 1) * GLA_DK)
            vc = slice(hd * GLA_DV, (hd + 1) * GLA_DV)
            o = _gla_chunk(q_ref[:, kc], k_ref[:, kc], v_ref[:, vc], ld_ref[:, kc], sign_ref, pair_ref,
                           st_ref.at[direction, hd], reverse=direction == 1)
            o_ref[:, vc] = o.astype(o_ref.dtype)


def _gla_scan(proj, ld, t_lat):
    b, s, _ = proj.shape
    ch = GLA_CHUNK
    n_chunks = s // ch
    lat_chunks = t_lat // ch

    def fwd(c):
        return (c + lat_chunks) % n_chunks

    def rev(c):
        return n_chunks - 1 - c

    def specs(chunk_of, direction):
        return [
            pl.BlockSpec((None, ch, GLA_KEY_DIM), lambda bi, c: (bi, chunk_of(c), 0)),
            pl.BlockSpec((None, ch, GLA_KEY_DIM), lambda bi, c: (bi, chunk_of(c), 1)),
            pl.BlockSpec((None, ch, GLA_VAL_DIM), lambda bi, c: (bi, chunk_of(c), 1)),
            pl.BlockSpec((None, ch, GLA_KEY_DIM), lambda bi, c: (bi, chunk_of(c), direction)),
        ]

    levels = ch.bit_length() - 1
    half_bit = 1 << jnp.arange(levels, dtype=jnp.int32)[:, None, None]
    in_upper = (jnp.arange(ch, dtype=jnp.int32)[None, :, None] & half_bit) != 0
    sign = jnp.broadcast_to(jnp.where(in_upper, 1.0, -1.0).astype(F32), (levels, ch, GLA_DK))
    t_idx = jnp.arange(ch, dtype=jnp.int32)[:, None]
    s_idx = jnp.arange(ch, dtype=jnp.int32)[None, :]
    at_level = ((t_idx ^ s_idx)[None] >> jnp.arange(levels, dtype=jnp.int32)[:, None, None]) == 1
    pair = jnp.stack([at_level & (t_idx > s_idx), at_level & (t_idx < s_idx)]).astype(F32)
    o_shape = jax.ShapeDtypeStruct((b, s, GLA_VAL_DIM), F32)
    return pl.pallas_call(
        _gla_scan_kernel,
        grid=(b, n_chunks),
        in_specs=[pl.BlockSpec((levels, ch, GLA_DK), lambda bi, c: (0, 0, 0)),
                  pl.BlockSpec((2, levels, ch, ch), lambda bi, c: (0, 0, 0, 0))] + specs(fwd, 0) + specs(rev, 1),
        out_specs=[
            pl.BlockSpec((None, ch, GLA_VAL_DIM), lambda bi, c: (bi, fwd(c), 0)),
            pl.BlockSpec((None, ch, GLA_VAL_DIM), lambda bi, c: (bi, rev(c), 0)),
        ],
        out_shape=[o_shape, o_shape],
        scratch_shapes=[pltpu.VMEM((2, GLA_HEADS, GLA_DV, GLA_DK), F32)],
        compiler_params=_cparams(("parallel", "arbitrary"), 32),
        name="gla_scan",
    )(sign, pair, proj, proj, proj, ld, proj, proj, proj, ld)


def _with_halo(ref, next_ref):
    return jnp.concatenate([ref[...], next_ref[...]], axis=0)


def _mixer_out_norm(x, h_refs, mod_ref, wout_ref, lng_ref, lnb_ref, h1_scr, alpha):
    y = _dot(x, wout_ref[...])
    h1_scr[...] = _residual_layer_norm(_with_halo(*h_refs), y, mod_ref[2:3, :], lng_ref[...], lnb_ref[...], alpha)


def _attn_ffn_kernel(oa_ref, oan_ref, ob_ref, obn_ref, h_ref, hn_ref, mod_ref,
                     wout_ref, ln1g_ref, ln1b_ref, wup_ref, cw_ref, cb_ref, wdn_ref, ln2g_ref, ln2b_ref,
                     out_ref, h1_scr, a_scr, u_scr, act_scr, edge_scr, *, t_lat, s_tot, alpha):
    x = jnp.concatenate([_with_halo(oa_ref, oan_ref), _with_halo(ob_ref, obn_ref)], axis=1)
    _mixer_out_norm(x, (h_ref, hn_ref), mod_ref, wout_ref, ln1g_ref, ln1b_ref, h1_scr, alpha)
    _conv_ffn_norm(h1_scr, mod_ref, wup_ref, cw_ref, cb_ref, wdn_ref, ln2g_ref, ln2b_ref, out_ref,
                   a_scr, u_scr, act_scr, edge_scr, t_lat=t_lat, s_tot=s_tot, alpha=alpha)


def _gla_ffn_kernel(of_ref, ofn_ref, or_ref, orn_ref, g_ref, gn_ref,
                    h_ref, hn_ref, mod_ref, hnorm_ref, wout_ref, ln1g_ref, ln1b_ref,
                    wup_ref, cw_ref, cb_ref, wdn_ref, ln2g_ref, ln2b_ref,
                    out_ref, h1_scr, a_scr, u_scr, act_scr, edge_scr, *, t_lat, s_tot, alpha):
    o = _with_halo(of_ref, ofn_ref) + _with_halo(or_ref, orn_ref)
    gate = _silu(_with_halo(g_ref, gn_ref).astype(F32))
    heads = []
    for hd in range(GLA_HEADS):
        cols = slice(hd * GLA_DV, (hd + 1) * GLA_DV)
        oh = o[:, cols]
        oh = oh * lax.rsqrt(jnp.mean(oh * oh, axis=-1, keepdims=True) + EPS) * hnorm_ref[...]
        heads.append((oh * gate[:, cols]).astype(MXU_DTYPE))
    x = jnp.concatenate(heads, axis=1)
    _mixer_out_norm(x, (h_ref, hn_ref), mod_ref, wout_ref, ln1g_ref, ln1b_ref, h1_scr, alpha)
    _conv_ffn_norm(h1_scr, mod_ref, wup_ref, cw_ref, cb_ref, wdn_ref, ln2g_ref, ln2b_ref, out_ref,
                   a_scr, u_scr, act_scr, edge_scr, t_lat=t_lat, s_tot=s_tot, alpha=alpha)


def _conv_ffn_norm(h1_scr, mod_ref, wup_ref, cw_ref, cb_ref, wdn_ref, lng_ref, lnb_ref,
                   out_ref, a_scr, u_scr, act_scr, edge_scr, *, t_lat, s_tot, alpha):
    tm = out_ref.shape[0]
    d_ff = wdn_ref.shape[0]
    row0 = pl.program_id(1) * tm
    at_start = (row0 == 0) | (row0 == t_lat)
    at_end = (row0 + tm == t_lat) | (row0 + tm == s_tot)

    @pl.when(pl.program_id(1) == 0)
    def _():
        edge_scr[...] = jnp.zeros(edge_scr.shape, F32)

    shift = mod_ref[3:4, :]
    scale = 1.0 + mod_ref[4:5, :]
    a_all = h1_scr[...] * scale + shift
    a_scr[0:tm, :] = a_all[0:tm, :].astype(a_scr.dtype)
    a_scr[tm:tm + HALO, :] = jnp.where(at_end, 0.0, a_all[tm:tm + HALO, :]).astype(a_scr.dtype)
    a = a_scr[...]

    def conv(u_ref, c0):
        cols = slice(c0, c0 + FF_CHUNK)
        acc = cb_ref[:, cols] + cw_ref[0:1, cols] * u_ref[pl.ds(HALO - 1, tm), :]
        for j in range(1, CONV_W):
            acc = acc + cw_ref[j:j + 1, cols] * u_ref[pl.ds(HALO - 1 + j, tm), :]
        return acc

    def up_project(c):
        for half, c0 in enumerate((c * FF_CHUNK, d_ff + c * FF_CHUNK)):
            u_ref = u_scr.at[c % U_SLOTS, half]
            u_ref[HALO:, :] = _dot(a, wup_ref[:, c0:c0 + FF_CHUNK])
            u_ref[HALO - 1:HALO, :] = jnp.where(at_start, 0.0, edge_scr[c, half, 0:1, :])
            edge_scr[c, half, 0:1, :] = u_ref[HALO + tm - 1:HALO + tm, :]

    def down_project(c):
        return _dot(act_scr[c % ACT_SLOTS], wdn_ref[c * FF_CHUNK:(c + 1) * FF_CHUNK, :])

    n_chunks = d_ff // FF_CHUNK
    ahead = U_SLOTS - 1
    y = jnp.zeros(out_ref.shape, F32)
    for c in range(min(ahead, n_chunks)):
        up_project(c)
    for c in range(n_chunks):
        if c + ahead < n_chunks:
            up_project(c + ahead)
        if c > 0:
            y = y + down_project(c - 1)
        g0 = c * FF_CHUNK
        act = _silu(conv(u_scr.at[c % U_SLOTS, 0], g0)) * conv(u_scr.at[c % U_SLOTS, 1], d_ff + g0)
        act_scr[c % ACT_SLOTS] = act.astype(act_scr.dtype)
    y = y + down_project(n_chunks - 1)
    out_ref[...] = _residual_layer_norm(h1_scr[0:tm, :], y, mod_ref[5:6, :], lng_ref[...], lnb_ref[...], alpha)


def _mixer_ffn(kernel_fn, name, mixer_ins, h, mods, layer, extra_params, w_out, mixer, ln_g, ln_b,
               w_up, conv_w, conv_b, w_down, t_lat, n_rows, alpha):
    b, s, d = h.shape
    tm = ROW_TILE
    d_ff = w_down.shape[1]
    row = _mod_row(t_lat, tm)
    hb = tm // HALO
    n_halo = s // HALO

    def halo_specs(cols, cb):
        return [
            pl.BlockSpec((None, tm, cols), lambda bi, i: (bi, i, cb)),
            pl.BlockSpec((None, HALO, cols), lambda bi, i: (bi, jnp.minimum((i + 1) * hb, n_halo - 1), cb)),
        ]

    row_ins = list(mixer_ins) + [(h, d, 0)]
    in_specs, operands = [], []
    for arr, cols, cb in row_ins:
        in_specs += halo_specs(cols, cb)
        operands += [arr, arr]
    in_specs.append(pl.BlockSpec((None, None, 6, d), lambda bi, i: (layer, row(bi, i), 0, 0)))
    operands.append(mods)
    for p in extra_params:
        in_specs.append(pl.BlockSpec(p.shape, lambda bi, i: (0, 0)))
        operands.append(p)
    in_specs += [
        pl.BlockSpec((None,) + w_out.shape[1:], lambda bi, i: (mixer, 0, 0)),
        pl.BlockSpec((None, 1, d), lambda bi, i: (layer, 0, 0)),
        pl.BlockSpec((None, 1, d), lambda bi, i: (layer, 0, 0)),
        pl.BlockSpec((None, d, 2 * d_ff), lambda bi, i: (layer, 0, 0)),
        pl.BlockSpec((None, CONV_W, 2 * d_ff), lambda bi, i: (layer, 0, 0)),
        pl.BlockSpec((None, 1, 2 * d_ff), lambda bi, i: (layer, 0, 0)),
        pl.BlockSpec((None, d_ff, d), lambda bi, i: (layer, 0, 0)),
        pl.BlockSpec((None, 1, d), lambda bi, i: (layer, 0, 0)),
        pl.BlockSpec((None, 1, d), lambda bi, i: (layer, 0, 0)),
    ]
    operands += [w_out, ln_g[:, 0:1, :], ln_b[:, 0:1, :], w_up, conv_w, conv_b.reshape(-1, 1, 2 * d_ff), w_down,
                 ln_g[:, 1:2, :], ln_b[:, 1:2, :]]
    return pl.pallas_call(
        functools.partial(kernel_fn, t_lat=t_lat, s_tot=s, alpha=alpha),
        grid=(b, n_rows // tm),
        in_specs=in_specs,
        out_specs=pl.BlockSpec((None, tm, d), lambda bi, i: (bi, i, 0)),
        out_shape=jax.ShapeDtypeStruct((b, n_rows, d), F32),
        scratch_shapes=[
            pltpu.VMEM((tm + HALO, d), F32),
            pltpu.VMEM((tm + HALO, d), MXU_DTYPE),
            pltpu.VMEM((U_SLOTS, 2, tm + 2 * HALO, FF_CHUNK), F32),
            pltpu.VMEM((ACT_SLOTS, tm, FF_CHUNK), MXU_DTYPE),
            pltpu.VMEM((d_ff // FF_CHUNK, 2, SUBLANES, FF_CHUNK), F32),
        ],
        compiler_params=_cparams(("parallel", "arbitrary"), 60),
        name=name,
    )(*operands)


def _rope_tables(t_lat, c_len):
    rows = t_lat // GRID_W
    n_freq = ROPE_AXIS_DIM // 2
    inv_freq = jnp.power(ROPE_THETA, -jnp.arange(n_freq, dtype=F32) * 2.0 / ROPE_AXIS_DIM)
    ang_row = jnp.arange(rows, dtype=F32)[:, None] * inv_freq
    ang_col = jnp.arange(GRID_W, dtype=F32)[:, None] * inv_freq

    def table(fn, ctx_value):
        by_row = jnp.broadcast_to(fn(ang_row)[:, None, :], (rows, GRID_W, n_freq))
        by_col = jnp.broadcast_to(fn(ang_col)[None, :, :], (rows, GRID_W, n_freq))
        lat = jnp.concatenate([by_row, by_col], axis=-1).reshape(t_lat, HEAD_DIM // 2)
        return jnp.concatenate([lat, jnp.full((c_len, HEAD_DIM // 2), ctx_value, F32)], axis=0)

    cos, sin = table(jnp.cos, 1.0), table(jnp.sin, 0.0)
    return jnp.concatenate([cos, cos], axis=-1), jnp.concatenate([-sin, sin], axis=-1)


def kernel(x, c, ctx, c_ctx, ada_w, ada_b, ln_g, ln_b, ffn_w_up, ffn_conv_w, ffn_conv_b, ffn_w_down, attn_w_in, attn_q_norm, attn_k_norm, attn_sink, attn_w_out, gla_w_in, gla_w_a1, gla_w_a2, gla_b_a, gla_head_norm, gla_w_out):
    b, t_lat, d = x.shape
    c_len = ctx.shape[1]
    s = t_lat + c_len
    depth = ada_w.shape[0]
    assert b < MOD_CTX_ROW and t_lat % ROW_TILE == 0 and c_len % ROW_TILE == 0
    assert t_lat % GLA_CHUNK == 0 and c_len % GLA_CHUNK == 0 and s % MXU_N == 0
    alpha = (2 * depth) ** 0.25

    cond = jnp.zeros((MOD_ROWS, d), F32).at[:b].set(c).at[MOD_CTX_ROW].set(c_ctx)
    mods = _modulation(cond, ada_w, ada_b).reshape(depth, MOD_ROWS, 6, d)
    cos, sin = _rope_tables(t_lat, c_len)
    h = jnp.concatenate([x, ctx], axis=1)
    attn_w_in_c, attn_w_out_c = attn_w_in.astype(MXU_DTYPE), attn_w_out.astype(MXU_DTYPE)
    gla_w_in_c, gla_w_out_c = gla_w_in.astype(MXU_DTYPE), gla_w_out.astype(MXU_DTYPE)
    ffn_w_up_c, ffn_w_down_c = ffn_w_up.astype(MXU_DTYPE), ffn_w_down.astype(MXU_DTYPE)

    for i in range(depth):
        last = i == depth - 1
        n_rows = t_lat if last else s
        j = i // 2
        if i % 2 == 0:
            qkv = _attn_project(h, mods, i, attn_w_in_c, j, attn_q_norm[j], attn_k_norm[j],
                                cos, sin, t_lat)
            o_a = _global_attention(qkv, t_lat)
            o_b = _window_attention(qkv, attn_sink[j], t_lat)
            h = _mixer_ffn(_attn_ffn_kernel, "attn_out_ffn", [(o_a, A_QW, 0), (o_b, B_QW, 0)], h, mods, i, [],
                           attn_w_out_c, j, ln_g, ln_b, ffn_w_up_c, ffn_conv_w, ffn_conv_b, ffn_w_down_c,
                           t_lat, n_rows, alpha)
        else:
            w_a1 = jnp.concatenate([gla_w_a1[j, 0], gla_w_a1[j, 1]], axis=1)
            w_a1 = jnp.pad(w_a1, ((0, 0), (0, LANES - 2 * GLA_GATE_RANK))).astype(MXU_DTYPE)
            w_a2 = jnp.zeros((LANES, 2 * GLA_KEY_DIM), F32)
            w_a2 = w_a2.at[:GLA_GATE_RANK, :GLA_KEY_DIM].set(gla_w_a2[j, 0])
            w_a2 = w_a2.at[GLA_GATE_RANK:2 * GLA_GATE_RANK, GLA_KEY_DIM:].set(gla_w_a2[j, 1]).astype(MXU_DTYPE)
            proj, ld = _gla_project(h, mods, i, gla_w_in_c, j, w_a1, w_a2,
                                    gla_b_a[j].reshape(1, 2 * GLA_KEY_DIM), t_lat)
            o_f, o_r = _gla_scan(proj, ld, t_lat)
            g_blk = (2 * GLA_KEY_DIM + GLA_VAL_DIM) // GLA_VAL_DIM
            h = _mixer_ffn(_gla_ffn_kernel, "gla_out_ffn",
                           [(o_f, GLA_VAL_DIM, 0), (o_r, GLA_VAL_DIM, 0), (proj, GLA_VAL_DIM, g_blk)], h, mods, i,
                           [gla_head_norm[j].reshape(1, GLA_DV)], gla_w_out_c, j, ln_g, ln_b,
                           ffn_w_up_c, ffn_conv_w, ffn_conv_b, ffn_w_down_c, t_lat, n_rows, alpha)
    return h
```
